```python
import math
import jax, jax.numpy as jnp
from jax import lax
import numpy as np

D_MODEL = 1024
BATCH = 8
SEQ = 2048
DEPTH = 1

CHUNK = 64
Q_BLOCK = 128
MEM_LEN = 256
SB_HEADS = 8
SB_HEAD_DIM = 64
SB_W = SB_HEADS * SB_HEAD_DIM
RET_HEADS = 8
RET_QK_DIM = 64
RET_V_DIM = 128
RET_QK_W = RET_HEADS * RET_QK_DIM
RET_V_W = RET_HEADS * RET_V_DIM
ROPE_BASE = 10000.0
IN_SPLITS = (SB_W, SB_W, SB_W, RET_QK_W, RET_QK_W, RET_V_W, RET_V_W, D_MODEL, D_MODEL)
IN_W = sum(IN_SPLITS)
X_HEADS = 4
X_HEAD_DIM = D_MODEL // X_HEADS
N_EXPERTS = 32
TOP_K = 4
D_FF = D_MODEL
SWIGLU_LIMIT = 7.0
SWIGLU_ALPHA = 1.702
EXPERT_BLOCK = 128
NORM_EPS = 1e-5

kernel_name = "hybrid_stickbreak_retention_xattn_moe"


def rmsnorm(x, g):
    xf = x.astype(jnp.float32)
    y = xf * lax.rsqrt(jnp.mean(xf * xf, axis=-1, keepdims=True) + NORM_EPS)
    return (y * g.astype(jnp.float32)).astype(x.dtype)


def rotary(x):
    S, d = x.shape[1], x.shape[3]
    inv = ROPE_BASE ** (-jnp.arange(0, d, 2, dtype=jnp.float32) / d)
    ang = jnp.arange(S, dtype=jnp.float32)[:, None] * inv[None, :]
    cos = jnp.cos(ang)[None, :, None, :]
    sin = jnp.sin(ang)[None, :, None, :]
    xf = x.astype(jnp.float32)
    x1, x2 = xf[..., : d // 2], xf[..., d // 2 :]
    return jnp.concatenate([x1 * cos - x2 * sin, x1 * sin + x2 * cos], axis=-1).astype(x.dtype)


def stick_breaking_attention(q, k, v):
    S, d = q.shape[1], q.shape[3]
    scale = d ** -0.5
    outs = []
    for i in range(S // Q_BLOCK):
        start, end = i * Q_BLOCK, (i + 1) * Q_BLOCK
        z = jnp.einsum('bqhd,bkhd->bhqk', q[:, start:end], k[:, :end]).astype(jnp.float32) * scale
        t_pos = jnp.arange(start, end)[:, None]
        s_pos = jnp.arange(end)[None, :]
        mask = s_pos < t_pos
        log_1m = jnp.where(mask, jax.nn.log_sigmoid(-z), 0.0)
        between = lax.cumsum(log_1m, axis=3, reverse=True) - log_1m
        w = jnp.where(mask, jnp.exp(jax.nn.log_sigmoid(z) + between), 0.0)
        outs.append(jnp.einsum('bhqk,bkhd->bqhd', w.astype(v.dtype), v[:, :end]))
    return jnp.concatenate(outs, axis=1)


def retention(q, k, v):
    B, S, H, dk = q.shape
    dv = v.shape[3]
    N = S // CHUNK
    log_g = jnp.log1p(-jnp.exp2(-5.0 - jnp.arange(H, dtype=jnp.float32)))
    qf = q.astype(jnp.float32).reshape(B, N, CHUNK, H, dk)
    kf = (k.astype(jnp.float32) * dk ** -0.5).reshape(B, N, CHUNK, H, dk)
    vf = v.astype(jnp.float32).reshape(B, N, CHUNK, H, dv)
    pos = jnp.arange(CHUNK, dtype=jnp.float32)
    diff = pos[:, None] - pos[None, :]
    decay = jnp.where(diff >= 0, jnp.exp(log_g[:, None, None] * jnp.maximum(diff, 0.0)), 0.0)
    scores = jnp.einsum('bnchd,bnshd->bnhcs', qf, kf) * decay
    o_intra = jnp.einsum('bnhcs,bnshe->bnche', scores, vf)
    k_dec = jnp.exp(log_g[:, None] * (CHUNK - 1 - pos)[None, :])
    kv = jnp.einsum('bnshd,hs,bnshe->nbhde', kf, k_dec, vf)
    chunk_dec = jnp.exp(log_g * CHUNK)[None, :, None, None]

    def step(state, kv_n):
        return state * chunk_dec + kv_n, state

    _, s_prev = lax.scan(step, jnp.zeros((B, H, dk, dv), jnp.float32), kv)
    q_dec = jnp.exp(log_g[:, None] * (pos + 1.0)[None, :])
    o_cross = jnp.einsum('bnchd,hc,nbhde->bnche', qf, q_dec, s_prev)
    return (o_intra + o_cross).reshape(B, S, H, dv)


def head_groupnorm(y):
    mu = jnp.mean(y, axis=-1, keepdims=True)
    var = jnp.mean(jnp.square(y - mu), axis=-1, keepdims=True)
    return (y - mu) * lax.rsqrt(var + NORM_EPS)


def token_mixer(h, w_in, w_sb_o, w_ret_o, w_mix_out):
    B, S, _ = h.shape
    proj = h @ w_in
    idx = list(np.cumsum(IN_SPLITS)[:-1])
    sb_q, sb_k, sb_v, r_q, r_k, r_v, r_g, g_sb, g_ret = jnp.split(proj, idx, axis=-1)
    y_sb = stick_breaking_attention(sb_q.reshape(B, S, SB_HEADS, SB_HEAD_DIM),
                                    sb_k.reshape(B, S, SB_HEADS, SB_HEAD_DIM),
                                    sb_v.reshape(B, S, SB_HEADS, SB_HEAD_DIM)).reshape(B, S, SB_W)
    y_sb = y_sb @ w_sb_o
    rq = rotary(r_q.reshape(B, S, RET_HEADS, RET_QK_DIM))
    rk = rotary(r_k.reshape(B, S, RET_HEADS, RET_QK_DIM))
    y_ret = head_groupnorm(retention(rq, rk, r_v.reshape(B, S, RET_HEADS, RET_V_DIM))).reshape(B, S, RET_V_W)
    y_ret = (jax.nn.silu(r_g.astype(jnp.float32)) * y_ret).astype(h.dtype) @ w_ret_o
    merged = jax.nn.sigmoid(g_sb) * y_sb + jax.nn.sigmoid(g_ret) * y_ret
    return merged @ w_mix_out


def memory_cross_attention(h, m, w_cq, w_ckv, w_co):
    B, S, D = h.shape
    q = (h @ w_cq).reshape(B, S, X_HEADS, X_HEAD_DIM)
    kv = m @ w_ckv
    k = kv[..., :D].reshape(B, -1, X_HEADS, X_HEAD_DIM)
    v = kv[..., D:].reshape(B, -1, X_HEADS, X_HEAD_DIM)
    s = jnp.einsum('bqhd,bmhd->bhqm', q, k).astype(jnp.float32) * X_HEAD_DIM ** -0.5
    p = jax.nn.softmax(s, axis=-1).astype(v.dtype)
    o = jnp.einsum('bhqm,bmhd->bqhd', p, v).reshape(B, S, D)
    return o @ w_co


def moe_ffn(h, w_router, b_router, w1, b1, w2, b2):
    B, S, D = h.shape
    T = B * S
    A = T * TOP_K
    ht = h.reshape(T, D)
    logits = ht.astype(jnp.float32) @ w_router.astype(jnp.float32) + b_router.astype(jnp.float32)
    top_val, top_idx = lax.top_k(logits, TOP_K)
    gate = jax.nn.softmax(top_val, axis=-1)
    flat_e = top_idx.reshape(A)
    flat_tok = jnp.arange(A, dtype=jnp.int32) // TOP_K
    order = jnp.argsort(flat_e)
    sorted_e = flat_e[order]
    sorted_tok = flat_tok[order]
    sorted_gate = gate.reshape(A)[order]
    counts = jnp.bincount(flat_e, length=N_EXPERTS)
    padded = ((counts + EXPERT_BLOCK - 1) // EXPERT_BLOCK) * EXPERT_BLOCK
    pad_end = jnp.cumsum(padded)
    pad_start = pad_end - padded
    cnt_start = jnp.cumsum(counts) - counts
    row = pad_start[sorted_e] + (jnp.arange(A) - cnt_start[sorted_e])
    n_blocks = (A + N_EXPERTS * (EXPERT_BLOCK - 1) + EXPERT_BLOCK - 1) // EXPERT_BLOCK
    P = n_blocks * EXPERT_BLOCK
    row_tok = jnp.zeros((P,), jnp.int32).at[row].set(sorted_tok)
    block_e = jnp.minimum(jnp.searchsorted(pad_end, jnp.arange(n_blocks) * EXPERT_BLOCK, side='right'),
                          N_EXPERTS - 1)
    xin = ht[row_tok].reshape(n_blocks, EXPERT_BLOCK, D)

    def expert_block(args):
        xb, e = args
        hc = xb @ w1[e] + b1[e]
        glu = jnp.minimum(hc[..., :D_FF], SWIGLU_LIMIT)
        lin = jnp.clip(hc[..., D_FF:], -SWIGLU_LIMIT, SWIGLU_LIMIT)
        act = glu * jax.nn.sigmoid(SWIGLU_ALPHA * glu) * (lin + 1.0)
        return act @ w2[e] + b2[e]

    yrows = lax.map(expert_block, (xin, block_e)).reshape(P, D)
    y_assign = yrows[row] * sorted_gate[:, None].astype(yrows.dtype)
    out = jax.ops.segment_sum(y_assign, sorted_tok, num_segments=T)
    return out.reshape(B, S, D)


def setup_inputs(seed: int = 0) -> dict:
    key = jax.random.key(seed)
    ks = jax.random.split(key, 20)
    f32 = jnp.float32

    def w(k, shape, fan_in):
        return jax.random.normal(k, shape, f32) * fan_in ** -0.5

    def gain(k, shape):
        return 1.0 + 0.02 * jax.random.normal(k, shape, f32)

    L = DEPTH
    return {
        "x": jax.random.normal(ks[0], (BATCH, SEQ, D_MODEL), f32),
        "mem": jax.random.normal(ks[1], (BATCH, MEM_LEN, D_MODEL), f32),
        "g_mix": gain(ks[2], (L, D_MODEL)),
        "w_in": w(ks[3], (L, D_MODEL, IN_W), D_MODEL),
        "w_sb_o": w(ks[4], (L, SB_W, D_MODEL), SB_W),
        "w_ret_o": w(ks[5], (L, RET_V_W, D_MODEL), RET_V_W),
        "w_mix_out": w(ks[6], (L, D_MODEL, D_MODEL), D_MODEL),
        "g_xattn": gain(ks[7], (L, D_MODEL)),
        "g_mem": gain(ks[8], (L, D_MODEL)),
        "w_cq": w(ks[9], (L, D_MODEL, D_MODEL), D_MODEL),
        "w_ckv": w(ks[10], (L, D_MODEL, 2 * D_MODEL), D_MODEL),
        "w_co": w(ks[11], (L, D_MODEL, D_MODEL), D_MODEL),
        "g_moe": gain(ks[12], (L, D_MODEL)),
        "w_router": w(ks[13], (L, D_MODEL, N_EXPERTS), D_MODEL),
        "b_router": 0.01 * jax.random.normal(ks[14], (L, N_EXPERTS), f32),
        "w_exp_in": w(ks[15], (L, N_EXPERTS, D_MODEL, 2 * D_FF), D_MODEL),
        "b_exp_in": 0.01 * jax.random.normal(ks[16], (L, N_EXPERTS, 2 * D_FF), f32),
        "w_exp_out": w(ks[17], (L, N_EXPERTS, D_FF, D_MODEL), D_FF),
        "b_exp_out": 0.01 * jax.random.normal(ks[18], (L, N_EXPERTS, D_MODEL), f32),
        "g_final": gain(ks[19], (D_MODEL,)),
    }


def reference(x, mem, g_mix, w_in, w_sb_o, w_ret_o, w_mix_out, g_xattn, g_mem, w_cq, w_ckv, w_co,
              g_moe, w_router, b_router, w_exp_in, b_exp_in, w_exp_out, b_exp_out, g_final):
    for l in range(DEPTH):
        x = x + token_mixer(rmsnorm(x, g_mix[l]), w_in[l], w_sb_o[l], w_ret_o[l], w_mix_out[l])
        x = x + memory_cross_attention(rmsnorm(x, g_xattn[l]), rmsnorm(mem, g_mem[l]),
                                       w_cq[l], w_ckv[l], w_co[l])
        x = x + moe_ffn(rmsnorm(x, g_moe[l]), w_router[l], b_router[l],
                        w_exp_in[l], b_exp_in[l], w_exp_out[l], b_exp_out[l])
    return rmsnorm(x, g_final)
```

```python
import functools
import math

import numpy as np
import jax
import jax.numpy as jnp
from jax import lax
from jax.experimental import pallas as pl
from jax.experimental.pallas import tpu as pltpu

F32 = jnp.float32
BF16 = jnp.bfloat16

D_MODEL = 1024
SB_HEADS = 8
SB_HEAD_DIM = 64
RET_HEADS = 8
RET_QK_DIM = 64
RET_V_DIM = 128
ROPE_BASE = 10000.0
X_HEADS = 4
X_HEAD_DIM = 256
N_EXPERTS = 32
TOP_K = 4
D_FF = 1024
SWIGLU_LIMIT = 7.0
SWIGLU_ALPHA = 1.702
NORM_EPS = 1e-5

_IN_GROUPS = (("r_v", 2560, 1024), ("r_g", 3584, 1024), ("g_sb", 4608, 1024), ("g_ret", 5632, 1024),
              ("sb_q", 0, 512), ("sb_k", 512, 512), ("sb_v", 1024, 512),
              ("r_q", 1536, 512), ("r_k", 2048, 512))
OFF_R_V, OFF_R_G, OFF_G_SB, OFF_G_RET = 0, 1024, 2048, 3072
OFF_SB_Q, OFF_SB_K, OFF_SB_V, OFF_R_Q, OFF_R_K = 4096, 4608, 5120, 5632, 6144
IN_W = 6656

LANES = 128
VMEM_LIMIT = 56 * 1024 * 1024

SB_TILE = 128
SB_LOG_UNDERFLOW = -88.0
RET_CHUNK = 256
FFN_BLOCK = 256


def _rms(x, g):
    return x * lax.rsqrt(jnp.mean(x * x, axis=-1, keepdims=True) + NORM_EPS) * g


def _params(sem):
    return pltpu.CompilerParams(dimension_semantics=sem, vmem_limit_bytes=VMEM_LIMIT)


def _inproj_kernel(x_ref, g_ref, w_ref, o_ref, h_scr):
    @pl.when(pl.program_id(1) == 0)
    def _():
        h_scr[...] = _rms(x_ref[...], g_ref[...]).astype(BF16)

    o_ref[...] = jnp.dot(h_scr[...], w_ref[...], preferred_element_type=F32).astype(o_ref.dtype)


def _inproj(x2d, g, w_bf16):
    T = x2d.shape[0]
    tm = min(1024, T)
    tn = 512
    return pl.pallas_call(
        _inproj_kernel,
        grid=(T // tm, IN_W // tn),
        in_specs=[
            pl.BlockSpec((tm, D_MODEL), lambda i, j: (i, 0)),
            pl.BlockSpec((1, D_MODEL), lambda i, j: (0, 0)),
            pl.BlockSpec((D_MODEL, tn), lambda i, j: (0, j)),
        ],
        out_specs=pl.BlockSpec((tm, tn), lambda i, j: (i, j)),
        out_shape=jax.ShapeDtypeStruct((T, IN_W), BF16),
        scratch_shapes=[pltpu.VMEM((tm, D_MODEL), BF16)],
        compiler_params=_params(("parallel", "arbitrary")),
        name="inproj",
    )(x2d, g, w_bf16)


def _sb_kernel(q_ref, k_ref, v_ref, o_ref):
    n = SB_TILE
    qi = pl.program_id(2)
    q = q_ref[...]
    lane = lax.broadcasted_iota(jnp.int32, (n, LANES), 1)
    row = lax.broadcasted_iota(jnp.int32, (n, n), 0)
    col = lax.broadcasted_iota(jnp.int32, (n, n), 1)
    strict = col < row
    jj = lax.broadcasted_iota(jnp.int32, (2 * n, 2 * n), 0) & (n - 1)
    cc = lax.broadcasted_iota(jnp.int32, (2 * n, 2 * n), 1)
    suffix = jnp.where((cc >= n) | (jj > cc), 1.0, 0.0).astype(BF16)
    scale = SB_HEAD_DIM ** -0.5

    def tile(qh, kt, carry, acc, diagonal):
        ks = pl.multiple_of(kt * n, n)
        kk = k_ref[pl.ds(ks, n), :]
        vv = v_ref[pl.ds(ks, n), :]
        z = lax.dot_general(qh, kk, (((1,), (1,)), ((), ())), preferred_element_type=F32)
        sp = jnp.maximum(z, 0.0) + jnp.log(1.0 + jnp.exp(-jnp.abs(z)))
        if diagonal:
            sp = jnp.where(strict, sp, 0.0)
        hi = sp.astype(BF16)
        lo = (sp - hi.astype(F32)).astype(BF16)
        cs = jnp.dot(jnp.concatenate([hi, lo], axis=1), suffix, preferred_element_type=F32)
        logw = (z - sp) - cs[:, :n] + carry
        w = jnp.exp(logw)
        if diagonal:
            w = jnp.where(strict, w, 0.0)
        acc = acc + jnp.dot(w.astype(BF16), vv, preferred_element_type=F32)
        return carry - cs[:, n:], acc

    def head(in_head):
        qh = (jnp.where(in_head, q, jnp.zeros_like(q)).astype(F32) * scale).astype(BF16)
        zero = jnp.zeros((n, LANES), F32)
        carry, acc = tile(qh, qi, zero, zero, True)

        def cond(st):
            kt, carry, _ = st
            return jnp.logical_and(kt >= 0, jnp.max(carry) > SB_LOG_UNDERFLOW)

        def body(st):
            kt, carry, acc = st
            carry, acc = tile(qh, kt, carry, acc, False)
            return kt - 1, carry, acc

        _, _, acc = lax.while_loop(cond, body, (qi - 1, carry, acc))
        return acc

    first = lane < SB_HEAD_DIM
    o_ref[...] = jnp.where(first, head(first), head(jnp.logical_not(first))).astype(o_ref.dtype)


def _sb_attention(proj3):
    B, S, _ = proj3.shape
    n = SB_TILE
    pairs = SB_HEADS * SB_HEAD_DIM // LANES
    qb, kb, vb = OFF_SB_Q // LANES, OFF_SB_K // LANES, OFF_SB_V // LANES
    return pl.pallas_call(
        _sb_kernel,
        grid=(B, pairs, S // n),
        in_specs=[
            pl.BlockSpec((None, n, LANES), lambda b, p, i: (b, i, qb + p)),
            pl.BlockSpec((None, S, LANES), lambda b, p, i: (b, 0, kb + p)),
            pl.BlockSpec((None, S, LANES), lambda b, p, i: (b, 0, vb + p)),
        ],
        out_specs=pl.BlockSpec((None, n, LANES), lambda b, p, i: (b, i, p)),
        out_shape=jax.ShapeDtypeStruct((B, S, SB_HEADS * SB_HEAD_DIM), BF16),
        compiler_params=_params(("parallel", "parallel", "arbitrary")),
        name="sb",
    )(proj3, proj3, proj3)


def _ret_tables(S):
    c = RET_CHUNK
    half = RET_QK_DIM // 2
    inv = ROPE_BASE ** (-jnp.arange(0, RET_QK_DIM, 2, dtype=F32) / RET_QK_DIM)
    ang = jnp.arange(S, dtype=F32)[:, None] * inv[None, :]
    cos, sin = jnp.cos(ang), jnp.sin(ang)
    reps = LANES // RET_QK_DIM
    cos_t = jnp.tile(jnp.concatenate([cos, cos], axis=1), (1, reps))
    sin_t = jnp.tile(jnp.concatenate([-sin, sin], axis=1), (1, reps))
    log_g = jnp.log1p(-jnp.exp2(-5.0 - jnp.arange(RET_HEADS, dtype=F32)))
    pos = jnp.arange(c, dtype=F32)
    diff = pos[:, None] - pos[None, :]
    decay = jnp.where(diff >= 0, jnp.exp(log_g[:, None, None] * jnp.maximum(diff, 0.0)), 0.0)
    q_dec = jnp.exp(log_g[:, None] * (pos + 1.0)[None, :])
    k_dec = jnp.exp(log_g[:, None] * (c - 1 - pos)[None, :])
    q_dec = jnp.broadcast_to(q_dec[:, :, None], (RET_HEADS, c, LANES))
    k_dec = jnp.broadcast_to(k_dec[:, :, None], (RET_HEADS, c, LANES))
    c_dec = jnp.broadcast_to(jnp.exp(log_g * c)[:, None, None], (RET_HEADS, LANES, LANES))
    del half
    return cos_t, sin_t, decay, q_dec, k_dec, c_dec


def _ret_kernel(q_ref, k_ref, v_ref, g_ref, cos_ref, sin_ref, dm_ref, qd_ref, kd_ref, cd_ref,
                o_ref, state):
    c = RET_CHUNK

    @pl.when(pl.program_id(2) == 0)
    def _():
        state[...] = jnp.zeros_like(state)

    lane = lax.broadcasted_iota(jnp.int32, (c, LANES), 1)
    low_half = (lane & (RET_QK_DIM - 1)) < RET_QK_DIM // 2
    cos = cos_ref[...]
    sin = sin_ref[...]

    def rotary(x):
        swapped = jnp.where(low_half, pltpu.roll(x, LANES - RET_QK_DIM // 2, 1),
                            pltpu.roll(x, RET_QK_DIM // 2, 1))
        return x * cos + swapped * sin

    rq = rotary(q_ref[...].astype(F32))
    rk = rotary(k_ref[...].astype(F32)) * RET_QK_DIM ** -0.5
    rkb = rk.astype(BF16)
    for j in range(2):
        in_head = (lane < RET_QK_DIM) if j == 0 else (lane >= RET_QK_DIM)
        qh = jnp.where(in_head, rq, 0.0)
        vh = v_ref[:, j * RET_V_DIM:(j + 1) * RET_V_DIM]
        s = lax.dot_general(qh.astype(BF16), rkb, (((1,), (1,)), ((), ())),
                            preferred_element_type=F32) * dm_ref[j]
        o = jnp.dot(s.astype(BF16), vh, preferred_element_type=F32)
        o = o + jnp.dot((qh * qd_ref[j]).astype(BF16), state[j].astype(BF16),
                        preferred_element_type=F32)
        kv_new = lax.dot_general((rk * kd_ref[j]).astype(BF16), vh, (((0,), (0,)), ((), ())),
                                 preferred_element_type=F32)
        state[j] = state[j] * cd_ref[j] + kv_new
        mu = jnp.mean(o, axis=-1, keepdims=True)
        d = o - mu
        y = d * lax.rsqrt(jnp.mean(d * d, axis=-1, keepdims=True) + NORM_EPS)
        g = g_ref[:, j * RET_V_DIM:(j + 1) * RET_V_DIM].astype(F32)
        o_ref[:, j * RET_V_DIM:(j + 1) * RET_V_DIM] = (g * jax.nn.sigmoid(g) * y).astype(o_ref.dtype)


def _retention(proj3):
    B, S, _ = proj3.shape
    c = RET_CHUNK
    pairs = RET_HEADS * RET_QK_DIM // LANES
    vw = 2 * RET_V_DIM
    qb, kb, vb, gb = OFF_R_Q // LANES, OFF_R_K // LANES, OFF_R_V // vw, OFF_R_G // vw
    cos_t, sin_t, decay, q_dec, k_dec, c_dec = _ret_tables(S)
    return pl.pallas_call(
        _ret_kernel,
        grid=(B, pairs, S // c),
        in_specs=[
            pl.BlockSpec((None, c, LANES), lambda b, p, i: (b, i, qb + p)),
            pl.BlockSpec((None, c, LANES), lambda b, p, i: (b, i, kb + p)),
            pl.BlockSpec((None, c, vw), lambda b, p, i: (b, i, vb + p)),
            pl.BlockSpec((None, c, vw), lambda b, p, i: (b, i, gb + p)),
            pl.BlockSpec((c, LANES), lambda b, p, i: (i, 0)),
            pl.BlockSpec((c, LANES), lambda b, p, i: (i, 0)),
            pl.BlockSpec((2, c, c), lambda b, p, i: (p, 0, 0)),
            pl.BlockSpec((2, c, LANES), lambda b, p, i: (p, 0, 0)),
            pl.BlockSpec((2, c, LANES), lambda b, p, i: (p, 0, 0)),
            pl.BlockSpec((2, LANES, LANES), lambda b, p, i: (p, 0, 0)),
        ],
        out_specs=pl.BlockSpec((None, c, vw), lambda b, p, i: (b, i, p)),
        out_shape=jax.ShapeDtypeStruct((B, S, RET_HEADS * RET_V_DIM), BF16),
        scratch_shapes=[pltpu.VMEM((2, LANES, LANES), F32)],
        compiler_params=_params(("parallel", "parallel", "arbitrary")),
        name="ret",
    )(proj3, proj3, proj3, proj3, cos_t, sin_t, decay, q_dec, k_dec, c_dec)


def _memkv_kernel(m_ref, g_ref, w_ref, o_ref):
    h = _rms(m_ref[...], g_ref[...]).astype(BF16)
    o_ref[...] = jnp.dot(h, w_ref[...], preferred_element_type=F32).astype(o_ref.dtype)


def _memkv(mem, g, w_bf16):
    B, M, _ = mem.shape
    return pl.pallas_call(
        _memkv_kernel,
        grid=(B,),
        in_specs=[
            pl.BlockSpec((None, M, D_MODEL), lambda b: (b, 0, 0)),
            pl.BlockSpec((1, D_MODEL), lambda b: (0, 0)),
            pl.BlockSpec((D_MODEL, 2 * D_MODEL), lambda b: (0, 0)),
        ],
        out_specs=pl.BlockSpec((None, M, 2 * D_MODEL), lambda b: (b, 0, 0)),
        out_shape=jax.ShapeDtypeStruct((B, M, 2 * D_MODEL), BF16),
        compiler_params=_params(("parallel",)),
        name="memkv",
    )(mem, g, w_bf16)


def _lane_pack(cols, rows, dtype):
    lane = lax.broadcasted_iota(jnp.int32, (rows, LANES), 1)
    out = jnp.zeros((rows, LANES), dtype)
    for k, c in enumerate(cols):
        out = jnp.where(lane == k, c.astype(dtype), out)
    return out


def _mix_kernel(x_ref, ysb_ref, yret_ref, gsb_ref, gret_ref, kv_ref,
                wsbo_ref, wreto_ref, wmix_ref, wcq_ref, wco_ref,
                gx_ref, gm_ref, wr_ref, br_ref,
                x2_ref, h3_ref, idx_ref, gate_ref, rank_ref, cnt_ref, cnt_scr):
    tm = x_ref.shape[0]

    @pl.when(jnp.logical_and(pl.program_id(0) == 0, pl.program_id(1) == 0))
    def _():
        cnt_scr[...] = jnp.zeros_like(cnt_scr)

    a = jnp.dot(ysb_ref[...], wsbo_ref[...], preferred_element_type=F32)
    b = jnp.dot(yret_ref[...], wreto_ref[...], preferred_element_type=F32)
    merged = (jax.nn.sigmoid(gsb_ref[...].astype(F32)) * a
              + jax.nn.sigmoid(gret_ref[...].astype(F32)) * b)
    x1 = x_ref[...] + jnp.dot(merged.astype(BF16), wmix_ref[...], preferred_element_type=F32)

    h2 = _rms(x1, gx_ref[...]).astype(BF16)
    q = (jnp.dot(h2, wcq_ref[...], preferred_element_type=F32) * X_HEAD_DIM ** -0.5).astype(BF16)
    heads = []
    for hh in range(X_HEADS):
        lo = hh * X_HEAD_DIM
        kh = kv_ref[:, lo:lo + X_HEAD_DIM]
        vh = kv_ref[:, D_MODEL + lo:D_MODEL + lo + X_HEAD_DIM]
        s = lax.dot_general(q[:, lo:lo + X_HEAD_DIM], kh, (((1,), (1,)), ((), ())),
                            preferred_element_type=F32)
        p = jnp.exp(s - jnp.max(s, axis=-1, keepdims=True))
        p = p * (1.0 / jnp.sum(p, axis=-1, keepdims=True))
        heads.append(jnp.dot(p.astype(BF16), vh, preferred_element_type=F32).astype(BF16))
    o = jnp.concatenate(heads, axis=1)
    x2 = x1 + jnp.dot(o, wco_ref[...], preferred_element_type=F32)
    x2_ref[...] = x2

    h3 = _rms(x2, gm_ref[...])
    h3_ref[...] = h3
    logits = jnp.dot(h3, wr_ref[...], preferred_element_type=F32,
                     precision=lax.Precision.HIGHEST) + br_ref[...]
    e_iota = lax.broadcasted_iota(jnp.int32, (tm, N_EXPERTS), 1).astype(F32)
    vals, idxs = [], []
    rem = logits
    for _ in range(TOP_K):
        m = jnp.max(rem, axis=-1, keepdims=True)
        ik = jnp.min(jnp.where(rem == m, e_iota, float(N_EXPERTS)), axis=-1, keepdims=True)
        vals.append(m)
        idxs.append(ik)
        rem = jnp.where(e_iota == ik, -jnp.inf, rem)
    ex = [jnp.exp(v - vals[0]) for v in vals]
    inv_den = 1.0 / (ex[0] + ex[1] + ex[2] + ex[3])
    gates = [e * inv_den for e in ex]

    chosen = [(e_iota == ik) for ik in idxs]
    member = jnp.zeros((tm, N_EXPERTS), F32)
    for ch in chosen:
        member = member + jnp.where(ch, 1.0, 0.0)
    r_i = lax.broadcasted_iota(jnp.int32, (tm, tm), 0)
    c_i = lax.broadcasted_iota(jnp.int32, (tm, tm), 1)
    before = jnp.where(c_i < r_i, 1.0, 0.0).astype(BF16)
    prefix = jnp.dot(before, member.astype(BF16), preferred_element_type=F32) + cnt_scr[...]
    ranks = [jnp.sum(jnp.where(ch, prefix, 0.0), axis=-1, keepdims=True) for ch in chosen]
    cnt_scr[...] = cnt_scr[...] + jnp.sum(member, axis=0, keepdims=True)
    cnt_ref[...] = cnt_scr[...]

    idx_ref[...] = _lane_pack(idxs, tm, jnp.int32)
    rank_ref[...] = _lane_pack(ranks, tm, jnp.int32)
    gate_ref[...] = _lane_pack(gates, tm, F32)


def _mix(x, proj3, y_sb, y_ret, kv, wsbo, wreto, wmix, wcq, wco, gx, gm, wr, br):
    B, S, _ = x.shape
    tm = min(512, S)
    M = kv.shape[1]
    gsb_b, gret_b = OFF_G_SB // D_MODEL, OFF_G_RET // D_MODEL
    tok = lambda b, i: (b, i, 0)
    const2 = lambda b, i: (0, 0)
    row_out = lambda w, dt: jax.ShapeDtypeStruct((B, S, w), dt)
    outs = pl.pallas_call(
        _mix_kernel,
        grid=(B, S // tm),
        in_specs=[
            pl.BlockSpec((None, tm, D_MODEL), tok),
            pl.BlockSpec((None, tm, SB_HEADS * SB_HEAD_DIM), tok),
            pl.BlockSpec((None, tm, D_MODEL), tok),
            pl.BlockSpec((None, tm, D_MODEL), lambda b, i: (b, i, gsb_b)),
            pl.BlockSpec((None, tm, D_MODEL), lambda b, i: (b, i, gret_b)),
            pl.BlockSpec((None, M, 2 * D_MODEL), lambda b, i: (b, 0, 0)),
            pl.BlockSpec((SB_HEADS * SB_HEAD_DIM, D_MODEL), const2),
            pl.BlockSpec((D_MODEL, D_MODEL), const2),
            pl.BlockSpec((D_MODEL, D_MODEL), const2),
            pl.BlockSpec((D_MODEL, D_MODEL), const2),
            pl.BlockSpec((D_MODEL, D_MODEL), const2),
            pl.BlockSpec((1, D_MODEL), const2),
            pl.BlockSpec((1, D_MODEL), const2),
            pl.BlockSpec((D_MODEL, N_EXPERTS), const2),
            pl.BlockSpec((1, N_EXPERTS), const2),
        ],
        out_specs=[
            pl.BlockSpec((None, tm, D_MODEL), tok),
            pl.BlockSpec((None, tm, D_MODEL), tok),
            pl.BlockSpec((None, tm, LANES), tok),
            pl.BlockSpec((None, tm, LANES), tok),
            pl.BlockSpec((None, tm, LANES), tok),
            pl.BlockSpec((1, N_EXPERTS), const2),
        ],
        out_shape=[
            row_out(D_MODEL, F32), row_out(D_MODEL, F32),
            row_out(LANES, jnp.int32), row_out(LANES, F32), row_out(LANES, jnp.int32),
            jax.ShapeDtypeStruct((1, N_EXPERTS), F32),
        ],
        scratch_shapes=[pltpu.VMEM((1, N_EXPERTS), F32)],
        compiler_params=_params(("arbitrary", "arbitrary")),
        name="mix",
    )(x, y_sb, y_ret, proj3, proj3, kv, wsbo, wreto, wmix, wcq, wco, gx, gm, wr, br)
    return outs


def _row_copy(src, s, dst, d, sem):
    return pltpu.make_async_copy(src.at[pl.ds(s, 1)], dst.at[pl.ds(d, 1)], sem)


def _dispatch_kernel(pos_ref, h_ref, zero_ref, xin_ref, sem):
    del zero_ref
    tm = h_ref.shape[0]

    def issue(t, carry):
        for k in range(TOP_K):
            _row_copy(h_ref, t, xin_ref, pos_ref[0, 0, t * TOP_K + k], sem).start()
        return carry

    lax.fori_loop(0, tm, issue, 0)

    def drain(i, carry):
        _row_copy(h_ref, 0, xin_ref, 0, sem).wait()
        return carry

    lax.fori_loop(0, tm * TOP_K, drain, 0)


def _dispatch(h3, pos, n_rows):
    T = h3.shape[0]
    tm = min(256, T)
    pos3 = pos.reshape(T // tm, 1, tm * TOP_K)
    zeros = jnp.zeros((n_rows, D_MODEL), F32)
    return pl.pallas_call(
        _dispatch_kernel,
        grid=(T // tm,),
        in_specs=[
            pl.BlockSpec((1, 1, tm * TOP_K), lambda i: (i, 0, 0), memory_space=pltpu.SMEM),
            pl.BlockSpec((tm, D_MODEL), lambda i: (i, 0)),
            pl.BlockSpec(memory_space=pl.ANY),
        ],
        out_specs=pl.BlockSpec(memory_space=pl.ANY),
        out_shape=jax.ShapeDtypeStruct((n_rows, D_MODEL), F32),
        scratch_shapes=[pltpu.SemaphoreType.DMA(())],
        input_output_aliases={2: 0},
        compiler_params=_params(("arbitrary",)),
        name="dispatch",
    )(pos3, h3, zeros)


def _ffn_kernel(be_ref, nreal_ref, x_ref, w1_ref, b1_ref, w2_ref, b2_ref, o_ref, w1b, w2b):
    b = pl.program_id(0)
    e = be_ref[b]
    prev = be_ref[jnp.maximum(b - 1, 0)]

    @pl.when(jnp.logical_or(b == 0, e != prev))
    def _():
        w1b[...] = w1_ref[...].astype(BF16)
        w2b[...] = w2_ref[...].astype(BF16)

    @pl.when(b < nreal_ref[0])
    def _():
        hc = jnp.dot(x_ref[...].astype(BF16), w1b[...], preferred_element_type=F32) + b1_ref[...]
        glu = jnp.minimum(hc[:, :D_FF], SWIGLU_LIMIT)
        lin = jnp.clip(hc[:, D_FF:], -SWIGLU_LIMIT, SWIGLU_LIMIT)
        act = glu * jax.nn.sigmoid(SWIGLU_ALPHA * glu) * (lin + 1.0)
        o_ref[...] = jnp.dot(act.astype(BF16), w2b[...], preferred_element_type=F32) + b2_ref[...]

    @pl.when(b >= nreal_ref[0])
    def _():
        o_ref[...] = jnp.zeros_like(o_ref)


def _ffn(xin, block_e, n_real, w1, b1, w2, b2):
    P = xin.shape[0]
    tb = FFN_BLOCK
    nb = P // tb
    grid_spec = pltpu.PrefetchScalarGridSpec(
        num_scalar_prefetch=2,
        grid=(nb,),
        in_specs=[
            pl.BlockSpec((tb, D_MODEL), lambda b, be, nr: (jnp.minimum(b, nr[0] - 1), 0)),
            pl.BlockSpec((None, D_MODEL, 2 * D_FF), lambda b, be, nr: (be[b], 0, 0)),
            pl.BlockSpec((None, 1, 2 * D_FF), lambda b, be, nr: (be[b], 0, 0)),
            pl.BlockSpec((None, D_FF, D_MODEL), lambda b, be, nr: (be[b], 0, 0)),
            pl.BlockSpec((None, 1, D_MODEL), lambda b, be, nr: (be[b], 0, 0)),
        ],
        out_specs=pl.BlockSpec((tb, D_MODEL), lambda b, be, nr: (b, 0)),
        scratch_shapes=[pltpu.VMEM((D_MODEL, 2 * D_FF), BF16), pltpu.VMEM((D_FF, D_MODEL), BF16)],
    )
    return pl.pallas_call(
        _ffn_kernel,
        grid_spec=grid_spec,
        out_shape=jax.ShapeDtypeStruct((P, D_MODEL), F32),
        compiler_params=_params(("arbitrary",)),
        name="ffn",
    )(block_e, n_real, xin, w1, b1.reshape(N_EXPERTS, 1, 2 * D_FF), w2, b2.reshape(N_EXPERTS, 1, D_MODEL))


def _combine_kernel(pos_ref, x2_ref, gate_ref, g_ref, y_ref, o_ref, buf, sem):
    tm = x2_ref.shape[0]

    def issue(t, carry):
        for k in range(TOP_K):
            _row_copy(y_ref, pos_ref[0, 0, t * TOP_K + k], buf.at[k], t, sem).start()
        return carry

    lax.fori_loop(0, tm, issue, 0)

    def drain(i, carry):
        _row_copy(y_ref, 0, buf.at[0], 0, sem).wait()
        return carry

    lax.fori_loop(0, tm * TOP_K, drain, 0)

    acc = x2_ref[...]
    gate = gate_ref[...]
    for k in range(TOP_K):
        acc = acc + gate[:, k:k + 1] * buf[k]
    o_ref[...] = _rms(acc, g_ref[...])


def _combine(x2, gate, pos, yrows, g_final):
    T = x2.shape[0]
    tm = min(256, T)
    pos3 = pos.reshape(T // tm, 1, tm * TOP_K)
    return pl.pallas_call(
        _combine_kernel,
        grid=(T // tm,),
        in_specs=[
            pl.BlockSpec((1, 1, tm * TOP_K), lambda i: (i, 0, 0), memory_space=pltpu.SMEM),
            pl.BlockSpec((tm, D_MODEL), lambda i: (i, 0)),
            pl.BlockSpec((tm, LANES), lambda i: (i, 0)),
            pl.BlockSpec((1, D_MODEL), lambda i: (0, 0)),
            pl.BlockSpec(memory_space=pl.ANY),
        ],
        out_specs=pl.BlockSpec((tm, D_MODEL), lambda i: (i, 0)),
        out_shape=jax.ShapeDtypeStruct((T, D_MODEL), F32),
        scratch_shapes=[pltpu.VMEM((TOP_K, tm, D_MODEL), F32), pltpu.SemaphoreType.DMA(())],
        compiler_params=_params(("arbitrary",)),
        name="combine",
    )(pos3, x2, gate, g_final, yrows)


def kernel(x, mem, g_mix, w_in, w_sb_o, w_ret_o, w_mix_out, g_xattn, g_mem, w_cq, w_ckv, w_co,
           g_moe, w_router, b_router, w_exp_in, b_exp_in, w_exp_out, b_exp_out, g_final):
    B, S, _ = x.shape
    T = B * S
    assert w_in.shape[0] == 1, "one layer"
    row = lambda v: v.reshape(1, -1)

    w_in_k = jnp.concatenate([w_in[0][:, o:o + w] for _, o, w in _IN_GROUPS], axis=1).astype(BF16)
    proj = _inproj(x.reshape(T, D_MODEL), row(g_mix[0]), w_in_k)
    proj3 = proj.reshape(B, S, IN_W)
    y_sb = _sb_attention(proj3)
    y_ret = _retention(proj3)
    kv = _memkv(mem, row(g_mem[0]), w_ckv[0].astype(BF16))
    x2, h3, idx, gate, rank, cnt = _mix(
        x, proj3, y_sb, y_ret, kv,
        w_sb_o[0].astype(BF16), w_ret_o[0].astype(BF16), w_mix_out[0].astype(BF16),
        w_cq[0].astype(BF16), w_co[0].astype(BF16),
        row(g_xattn[0]), row(g_moe[0]), w_router[0], row(b_router[0]))

    tb = FFN_BLOCK
    n_blocks = (T * TOP_K + N_EXPERTS * (tb - 1) + tb - 1) // tb
    counts = cnt[0].astype(jnp.int32)
    padded = ((counts + tb - 1) // tb) * tb
    pad_end = jnp.cumsum(padded)
    pad_start = pad_end - padded
    idx4 = idx.reshape(T, LANES)[:, :TOP_K]
    pos = (pad_start[idx4] + rank.reshape(T, LANES)[:, :TOP_K]).astype(jnp.int32).reshape(T * TOP_K)
    n_real = (pad_end[-1:] // tb).astype(jnp.int32)
    block_start = jnp.arange(n_blocks, dtype=jnp.int32) * tb
    block_e = jnp.minimum(jnp.sum(pad_end[None, :] <= block_start[:, None], axis=1),
                          N_EXPERTS - 1).astype(jnp.int32)

    xin = _dispatch(h3.reshape(T, D_MODEL), pos, n_blocks * tb)
    yrows = _ffn(xin, block_e, n_real, w_exp_in[0], b_exp_in[0], w_exp_out[0], b_exp_out[0])
    out = _combine(x2.reshape(T, D_MODEL), gate.reshape(T, LANES), pos, yrows, row(g_final))
    return out.reshape(B, S, D_MODEL)
```

```python
import functools
import math

import numpy as np
import jax
import jax.numpy as jnp
from jax import lax
from jax.experimental import pallas as pl
from jax.experimental.pallas import tpu as pltpu

F32 = jnp.float32
BF16 = jnp.bfloat16

D_MODEL = 1024
SB_HEADS = 8
SB_HEAD_DIM = 64
RET_HEADS = 8
RET_QK_DIM = 64
RET_V_DIM = 128
ROPE_BASE = 10000.0
X_HEADS = 4
X_HEAD_DIM = 256
N_EXPERTS = 32
TOP_K = 4
D_FF = 1024
SWIGLU_LIMIT = 7.0
SWIGLU_ALPHA = 1.702
NORM_EPS = 1e-5

_IN_GROUPS = (("r_v", 2560, 1024), ("r_g", 3584, 1024), ("g_sb", 4608, 1024), ("g_ret", 5632, 1024),
              ("sb_q", 0, 512), ("sb_k", 512, 512), ("sb_v", 1024, 512),
              ("r_q", 1536, 512), ("r_k", 2048, 512))
OFF_R_V, OFF_R_G, OFF_G_SB, OFF_G_RET = 0, 1024, 2048, 3072
OFF_SB_Q, OFF_SB_K, OFF_SB_V, OFF_R_Q, OFF_R_K = 4096, 4608, 5120, 5632, 6144
IN_W = 6656

LANES = 128
VMEM_LIMIT = 56 * 1024 * 1024

SB_TILE = 128
SB_LOG_UNDERFLOW = -88.0
RET_CHUNK = 256
FFN_BLOCK = 512


def _rms(x, g):
    return x * lax.rsqrt(jnp.mean(x * x, axis=-1, keepdims=True) + NORM_EPS) * g


def _params(sem):
    return pltpu.CompilerParams(dimension_semantics=sem, vmem_limit_bytes=VMEM_LIMIT)


def _inproj_kernel(x_ref, g_ref, w_ref, o_ref, h_scr):
    @pl.when(pl.program_id(1) == 0)
    def _():
        h_scr[...] = _rms(x_ref[...], g_ref[...]).astype(BF16)

    o_ref[...] = jnp.dot(h_scr[...], w_ref[...], preferred_element_type=F32).astype(o_ref.dtype)


def _inproj(x2d, g, w_bf16):
    T = x2d.shape[0]
    tm = min(1024, T)
    tn = 512
    return pl.pallas_call(
        _inproj_kernel,
        grid=(T // tm, IN_W // tn),
        in_specs=[
            pl.BlockSpec((tm, D_MODEL), lambda i, j: (i, 0)),
            pl.BlockSpec((1, D_MODEL), lambda i, j: (0, 0)),
            pl.BlockSpec((D_MODEL, tn), lambda i, j: (0, j)),
        ],
        out_specs=pl.BlockSpec((tm, tn), lambda i, j: (i, j)),
        out_shape=jax.ShapeDtypeStruct((T, IN_W), BF16),
        scratch_shapes=[pltpu.VMEM((tm, D_MODEL), BF16)],
        compiler_params=_params(("parallel", "arbitrary")),
        name="inproj",
    )(x2d, g, w_bf16)


SB_PAIRS = SB_HEADS * SB_HEAD_DIM // LANES


def _sb_kernel(q_ref, k_ref, v_ref, o_ref, q2_ref, carry_ref, acc_ref):
    n = SB_TILE
    qi = pl.program_id(1)
    lane = lax.broadcasted_iota(jnp.int32, (n, LANES), 1)
    first = lane < SB_HEAD_DIM
    row = lax.broadcasted_iota(jnp.int32, (2 * n, n), 0) & (n - 1)
    col = lax.broadcasted_iota(jnp.int32, (2 * n, n), 1)
    strict = col < row
    jj = lax.broadcasted_iota(jnp.int32, (2 * n, 2 * n), 0) & (n - 1)
    cc = lax.broadcasted_iota(jnp.int32, (2 * n, 2 * n), 1)
    suffix = jnp.where((cc >= n) | (jj > cc), 1.0, 0.0).astype(BF16)
    scale = SB_HEAD_DIM ** -0.5

    for p in range(SB_PAIRS):
        qs = (q_ref[:, p * LANES:(p + 1) * LANES].astype(F32) * scale).astype(BF16)
        zeros = jnp.zeros_like(qs)
        q2_ref[p * 2 * n:p * 2 * n + n, :] = jnp.where(first, qs, zeros)
        q2_ref[p * 2 * n + n:(p + 1) * 2 * n, :] = jnp.where(first, zeros, qs)

    def step(kt, diagonal):
        ks = pl.multiple_of(kt * n, n)
        zs, sps = [], []
        for p in range(SB_PAIRS):
            kk = k_ref[pl.ds(ks, n), p * LANES:(p + 1) * LANES]
            z = lax.dot_general(q2_ref[p * 2 * n:(p + 1) * 2 * n, :], kk, (((1,), (1,)), ((), ())),
                                preferred_element_type=F32)
            sp = jnp.maximum(z, 0.0) + jnp.log(1.0 + jnp.exp(-jnp.abs(z)))
            if diagonal:
                sp = jnp.where(strict, sp, 0.0)
            zs.append(z)
            sps.append(sp)
        z = jnp.concatenate(zs, axis=0)
        sp = jnp.concatenate(sps, axis=0)
        hi = sp.astype(BF16)
        lo = (sp - hi.astype(F32)).astype(BF16)
        cs = jnp.dot(jnp.concatenate([hi, lo], axis=1), suffix, preferred_element_type=F32)
        logw = (z - sp) - cs[:, :n]
        if not diagonal:
            logw = logw + carry_ref[...]
        w = jnp.exp(logw)
        if diagonal:
            w = jnp.where(jnp.concatenate([strict] * SB_PAIRS, axis=0), w, 0.0)
        wb = w.astype(BF16)
        for p in range(SB_PAIRS):
            rows = slice(p * 2 * n, (p + 1) * 2 * n)
            vv = v_ref[pl.ds(ks, n), p * LANES:(p + 1) * LANES]
            pv = jnp.dot(wb[rows], vv, preferred_element_type=F32)
            acc_ref[rows, :] = pv if diagonal else acc_ref[rows, :] + pv
        carry = -cs[:, n:] if diagonal else carry_ref[...] - cs[:, n:]
        carry_ref[...] = carry
        return jnp.max(carry)

    def cond(st):
        kt, cmax = st
        return jnp.logical_and(kt >= 0, cmax > SB_LOG_UNDERFLOW)

    def body(st):
        kt, _ = st
        return kt - 1, step(kt, False)

    lax.while_loop(cond, body, (qi - 1, step(qi, True)))

    for p in range(SB_PAIRS):
        o_ref[:, p * LANES:(p + 1) * LANES] = jnp.where(
            first, acc_ref[p * 2 * n:p * 2 * n + n, :], acc_ref[p * 2 * n + n:(p + 1) * 2 * n, :]
        ).astype(o_ref.dtype)


def _sb_attention(proj3):
    B, S, _ = proj3.shape
    n = SB_TILE
    w = SB_HEADS * SB_HEAD_DIM
    qb, kb, vb = OFF_SB_Q // w, OFF_SB_K // w, OFF_SB_V // w
    rows = SB_PAIRS * 2 * n
    return pl.pallas_call(
        _sb_kernel,
        grid=(B, S // n),
        in_specs=[
            pl.BlockSpec((None, n, w), lambda b, i: (b, i, qb)),
            pl.BlockSpec((None, S, w), lambda b, i: (b, 0, kb)),
            pl.BlockSpec((None, S, w), lambda b, i: (b, 0, vb)),
        ],
        out_specs=pl.BlockSpec((None, n, w), lambda b, i: (b, i, 0)),
        out_shape=jax.ShapeDtypeStruct((B, S, w), BF16),
        scratch_shapes=[pltpu.VMEM((rows, LANES), BF16), pltpu.VMEM((rows, LANES), F32),
                        pltpu.VMEM((rows, LANES), F32)],
        compiler_params=_params(("parallel", "arbitrary")),
        name="sb",
    )(proj3, proj3, proj3)


def _ret_tables(S):
    c = RET_CHUNK
    half = RET_QK_DIM // 2
    inv = ROPE_BASE ** (-jnp.arange(0, RET_QK_DIM, 2, dtype=F32) / RET_QK_DIM)
    ang = jnp.arange(S, dtype=F32)[:, None] * inv[None, :]
    cos, sin = jnp.cos(ang), jnp.sin(ang)
    reps = LANES // RET_QK_DIM
    cos_t = jnp.tile(jnp.concatenate([cos, cos], axis=1), (1, reps))
    sin_t = jnp.tile(jnp.concatenate([-sin, sin], axis=1), (1, reps))
    log_g = jnp.log1p(-jnp.exp2(-5.0 - jnp.arange(RET_HEADS, dtype=F32)))
    pos = jnp.arange(c, dtype=F32)
    diff = pos[:, None] - pos[None, :]
    decay = jnp.where(diff >= 0, jnp.exp(log_g[:, None, None] * jnp.maximum(diff, 0.0)), 0.0)
    q_dec = jnp.exp(log_g[:, None] * (pos + 1.0)[None, :])
    k_dec = jnp.exp(log_g[:, None] * (c - 1 - pos)[None, :])
    q_dec = jnp.broadcast_to(q_dec[:, :, None], (RET_HEADS, c, LANES))
    k_dec = jnp.broadcast_to(k_dec[:, :, None], (RET_HEADS, c, LANES))
    c_dec = jnp.broadcast_to(jnp.exp(log_g * c)[:, None, None], (RET_HEADS, LANES, LANES))
    del half
    return cos_t, sin_t, decay, q_dec, k_dec, c_dec


def _ret_kernel(q_ref, k_ref, v_ref, g_ref, cos_ref, sin_ref, dm_ref, qd_ref, kd_ref, cd_ref,
                o_ref, state):
    c = RET_CHUNK

    @pl.when(pl.program_id(1) == 0)
    def _():
        state[...] = jnp.zeros_like(state)

    lane = lax.broadcasted_iota(jnp.int32, (c, LANES), 1)
    low_half = (lane & (RET_QK_DIM - 1)) < RET_QK_DIM // 2
    cos = cos_ref[...]
    sin = sin_ref[...]

    def rotary(x):
        swapped = jnp.where(low_half, pltpu.roll(x, LANES - RET_QK_DIM // 2, 1),
                            pltpu.roll(x, RET_QK_DIM // 2, 1))
        return x * cos + swapped * sin

    for p in range(RET_HEADS * RET_QK_DIM // LANES):
        rq = rotary(q_ref[:, p * LANES:(p + 1) * LANES].astype(F32))
        rk = rotary(k_ref[:, p * LANES:(p + 1) * LANES].astype(F32)) * RET_QK_DIM ** -0.5
        rkb = rk.astype(BF16)
        for j in range(2):
            h = 2 * p + j
            cols = slice(h * RET_V_DIM, (h + 1) * RET_V_DIM)
            in_head = (lane < RET_QK_DIM) if j == 0 else (lane >= RET_QK_DIM)
            qh = jnp.where(in_head, rq, 0.0)
            vh = v_ref[:, cols]
            s = lax.dot_general(qh.astype(BF16), rkb, (((1,), (1,)), ((), ())),
                                preferred_element_type=F32) * dm_ref[h]
            o = jnp.dot(s.astype(BF16), vh, preferred_element_type=F32)
            o = o + jnp.dot((qh * qd_ref[h]).astype(BF16), state[h].astype(BF16),
                            preferred_element_type=F32)
            kv_new = lax.dot_general((rk * kd_ref[h]).astype(BF16), vh, (((0,), (0,)), ((), ())),
                                     preferred_element_type=F32)
            state[h] = state[h] * cd_ref[h] + kv_new
            mu = jnp.mean(o, axis=-1, keepdims=True)
            d = o - mu
            y = d * lax.rsqrt(jnp.mean(d * d, axis=-1, keepdims=True) + NORM_EPS)
            g = g_ref[:, cols].astype(F32)
            o_ref[:, cols] = (g * jax.nn.sigmoid(g) * y).astype(o_ref.dtype)


def _retention(proj3):
    B, S, _ = proj3.shape
    c = RET_CHUNK
    qkw, vw = RET_HEADS * RET_QK_DIM, RET_HEADS * RET_V_DIM
    qb, kb, vb, gb = OFF_R_Q // qkw, OFF_R_K // qkw, OFF_R_V // vw, OFF_R_G // vw
    cos_t, sin_t, decay, q_dec, k_dec, c_dec = _ret_tables(S)
    const3 = lambda b, i: (0, 0, 0)
    return pl.pallas_call(
        _ret_kernel,
        grid=(B, S // c),
        in_specs=[
            pl.BlockSpec((None, c, qkw), lambda b, i: (b, i, qb)),
            pl.BlockSpec((None, c, qkw), lambda b, i: (b, i, kb)),
            pl.BlockSpec((None, c, vw), lambda b, i: (b, i, vb)),
            pl.BlockSpec((None, c, vw), lambda b, i: (b, i, gb)),
            pl.BlockSpec((c, LANES), lambda b, i: (i, 0)),
            pl.BlockSpec((c, LANES), lambda b, i: (i, 0)),
            pl.BlockSpec((RET_HEADS, c, c), const3),
            pl.BlockSpec((RET_HEADS, c, LANES), const3),
            pl.BlockSpec((RET_HEADS, c, LANES), const3),
            pl.BlockSpec((RET_HEADS, LANES, LANES), const3),
        ],
        out_specs=pl.BlockSpec((None, c, vw), lambda b, i: (b, i, 0)),
        out_shape=jax.ShapeDtypeStruct((B, S, vw), BF16),
        scratch_shapes=[pltpu.VMEM((RET_HEADS, LANES, LANES), F32)],
        compiler_params=_params(("parallel", "arbitrary")),
        name="ret",
    )(proj3, proj3, proj3, proj3, cos_t, sin_t, decay, q_dec, k_dec, c_dec)


def _memkv_kernel(m_ref, g_ref, w_ref, o_ref):
    h = _rms(m_ref[...], g_ref[...]).astype(BF16)
    o_ref[...] = jnp.dot(h, w_ref[...], preferred_element_type=F32).astype(o_ref.dtype)


def _memkv(mem, g, w_bf16):
    B, M, _ = mem.shape
    return pl.pallas_call(
        _memkv_kernel,
        grid=(B,),
        in_specs=[
            pl.BlockSpec((None, M, D_MODEL), lambda b: (b, 0, 0)),
            pl.BlockSpec((1, D_MODEL), lambda b: (0, 0)),
            pl.BlockSpec((D_MODEL, 2 * D_MODEL), lambda b: (0, 0)),
        ],
        out_specs=pl.BlockSpec((None, M, 2 * D_MODEL), lambda b: (b, 0, 0)),
        out_shape=jax.ShapeDtypeStruct((B, M, 2 * D_MODEL), BF16),
        compiler_params=_params(("parallel",)),
        name="memkv",
    )(mem, g, w_bf16)


def _lane_pack(cols, rows, dtype):
    lane = lax.broadcasted_iota(jnp.int32, (rows, LANES), 1)
    out = jnp.zeros((rows, LANES), dtype)
    for k, c in enumerate(cols):
        out = jnp.where(lane == k, c.astype(dtype), out)
    return out


def _mix_kernel(x_ref, ysb_ref, yret_ref, gsb_ref, gret_ref, kv_ref,
                wsbo_ref, wreto_ref, wmix_ref, wcq_ref, wco_ref,
                gx_ref, gm_ref, wr_ref, br_ref,
                x2_ref, h3_ref, idx_ref, gate_ref, rank_ref, cnt_ref, cnt_scr):
    tm = x_ref.shape[0]

    @pl.when(jnp.logical_and(pl.program_id(0) == 0, pl.program_id(1) == 0))
    def _():
        cnt_scr[...] = jnp.zeros_like(cnt_scr)

    a = jnp.dot(ysb_ref[...], wsbo_ref[...], preferred_element_type=F32)
    b = jnp.dot(yret_ref[...], wreto_ref[...], preferred_element_type=F32)
    merged = (jax.nn.sigmoid(gsb_ref[...].astype(F32)) * a
              + jax.nn.sigmoid(gret_ref[...].astype(F32)) * b)
    x1 = x_ref[...] + jnp.dot(merged.astype(BF16), wmix_ref[...], preferred_element_type=F32)

    h2 = _rms(x1, gx_ref[...]).astype(BF16)
    q = (jnp.dot(h2, wcq_ref[...], preferred_element_type=F32) * X_HEAD_DIM ** -0.5).astype(BF16)
    heads = []
    for hh in range(X_HEADS):
        lo = hh * X_HEAD_DIM
        kh = kv_ref[:, lo:lo + X_HEAD_DIM]
        vh = kv_ref[:, D_MODEL + lo:D_MODEL + lo + X_HEAD_DIM]
        s = lax.dot_general(q[:, lo:lo + X_HEAD_DIM], kh, (((1,), (1,)), ((), ())),
                            preferred_element_type=F32)
        p = jnp.exp(s - jnp.max(s, axis=-1, keepdims=True))
        p = p * (1.0 / jnp.sum(p, axis=-1, keepdims=True))
        heads.append(jnp.dot(p.astype(BF16), vh, preferred_element_type=F32).astype(BF16))
    o = jnp.concatenate(heads, axis=1)
    x2 = x1 + jnp.dot(o, wco_ref[...], preferred_element_type=F32)
    x2_ref[...] = x2

    h3 = _rms(x2, gm_ref[...])
    h3_ref[...] = h3
    logits = jnp.dot(h3, wr_ref[...], preferred_element_type=F32,
                     precision=lax.Precision.HIGHEST) + br_ref[...]
    e_iota = lax.broadcasted_iota(jnp.int32, (tm, N_EXPERTS), 1).astype(F32)
    vals, idxs = [], []
    rem = logits
    for _ in range(TOP_K):
        m = jnp.max(rem, axis=-1, keepdims=True)
        ik = jnp.min(jnp.where(rem == m, e_iota, float(N_EXPERTS)), axis=-1, keepdims=True)
        vals.append(m)
        idxs.append(ik)
        rem = jnp.where(e_iota == ik, -jnp.inf, rem)
    ex = [jnp.exp(v - vals[0]) for v in vals]
    inv_den = 1.0 / (ex[0] + ex[1] + ex[2] + ex[3])
    gates = [e * inv_den for e in ex]

    chosen = [(e_iota == ik) for ik in idxs]
    member = jnp.zeros((tm, N_EXPERTS), F32)
    for ch in chosen:
        member = member + jnp.where(ch, 1.0, 0.0)
    r_i = lax.broadcasted_iota(jnp.int32, (tm, tm), 0)
    c_i = lax.broadcasted_iota(jnp.int32, (tm, tm), 1)
    before = jnp.where(c_i < r_i, 1.0, 0.0).astype(BF16)
    prefix = jnp.dot(before, member.astype(BF16), preferred_element_type=F32) + cnt_scr[...]
    ranks = [jnp.sum(jnp.where(ch, prefix, 0.0), axis=-1, keepdims=True) for ch in chosen]
    cnt_scr[...] = cnt_scr[...] + jnp.sum(member, axis=0, keepdims=True)
    cnt_ref[...] = cnt_scr[...]

    idx_ref[...] = _lane_pack(idxs, tm, jnp.int32)
    rank_ref[...] = _lane_pack(ranks, tm, jnp.int32)
    gate_ref[...] = _lane_pack(gates, tm, F32)


def _mix(x, proj3, y_sb, y_ret, kv, wsbo, wreto, wmix, wcq, wco, gx, gm, wr, br):
    B, S, _ = x.shape
    tm = min(512, S)
    M = kv.shape[1]
    gsb_b, gret_b = OFF_G_SB // D_MODEL, OFF_G_RET // D_MODEL
    tok = lambda b, i: (b, i, 0)
    const2 = lambda b, i: (0, 0)
    row_out = lambda w, dt: jax.ShapeDtypeStruct((B, S, w), dt)
    outs = pl.pallas_call(
        _mix_kernel,
        grid=(B, S // tm),
        in_specs=[
            pl.BlockSpec((None, tm, D_MODEL), tok),
            pl.BlockSpec((None, tm, SB_HEADS * SB_HEAD_DIM), tok),
            pl.BlockSpec((None, tm, D_MODEL), tok),
            pl.BlockSpec((None, tm, D_MODEL), lambda b, i: (b, i, gsb_b)),
            pl.BlockSpec((None, tm, D_MODEL), lambda b, i: (b, i, gret_b)),
            pl.BlockSpec((None, M, 2 * D_MODEL), lambda b, i: (b, 0, 0)),
            pl.BlockSpec((SB_HEADS * SB_HEAD_DIM, D_MODEL), const2),
            pl.BlockSpec((D_MODEL, D_MODEL), const2),
            pl.BlockSpec((D_MODEL, D_MODEL), const2),
            pl.BlockSpec((D_MODEL, D_MODEL), const2),
            pl.BlockSpec((D_MODEL, D_MODEL), const2),
            pl.BlockSpec((1, D_MODEL), const2),
            pl.BlockSpec((1, D_MODEL), const2),
            pl.BlockSpec((D_MODEL, N_EXPERTS), const2),
            pl.BlockSpec((1, N_EXPERTS), const2),
        ],
        out_specs=[
            pl.BlockSpec((None, tm, D_MODEL), tok),
            pl.BlockSpec((None, tm, D_MODEL), tok),
            pl.BlockSpec((None, tm, LANES), tok),
            pl.BlockSpec((None, tm, LANES), tok),
            pl.BlockSpec((None, tm, LANES), tok),
            pl.BlockSpec((1, N_EXPERTS), const2),
        ],
        out_shape=[
            row_out(D_MODEL, F32), row_out(D_MODEL, F32),
            row_out(LANES, jnp.int32), row_out(LANES, F32), row_out(LANES, jnp.int32),
            jax.ShapeDtypeStruct((1, N_EXPERTS), F32),
        ],
        scratch_shapes=[pltpu.VMEM((1, N_EXPERTS), F32)],
        compiler_params=_params(("arbitrary", "arbitrary")),
        name="mix",
    )(x, y_sb, y_ret, proj3, proj3, kv, wsbo, wreto, wmix, wcq, wco, gx, gm, wr, br)
    return outs


def _row_copy(src, s, dst, d, sem):
    return pltpu.make_async_copy(src.at[pl.ds(s, 1)], dst.at[pl.ds(d, 1)], sem)


DMA_UNROLL = 8


ZERO_ROWS = 256
SUBLANES = 8


def _dispatch_kernel(lo_ref, hi_ref, pos_ref, h_ref, xin_ref, zbuf, sem, zsem):
    tm = h_ref.shape[0]

    def zero_fill(act):
        def per_expert(e, carry):
            lo = lo_ref[e]
            hi = hi_ref[e]
            head_end = jnp.minimum((lo + SUBLANES - 1) // SUBLANES * SUBLANES, hi)

            def single(r, c):
                act(_row_copy(zbuf, 0, xin_ref, r, zsem))
                return c

            lax.fori_loop(lo, head_end, single, 0)

            def chunk(i, off):
                act(pltpu.make_async_copy(zbuf, xin_ref.at[pl.ds(pl.multiple_of(off, SUBLANES), ZERO_ROWS)], zsem))
                return off + ZERO_ROWS

            off = lax.fori_loop(0, (hi - head_end) // ZERO_ROWS, chunk, head_end)
            rest = hi - off
            size = ZERO_ROWS // 2
            while size >= SUBLANES:
                @pl.when((rest & size) != 0)
                def _(off=off, size=size):
                    act(pltpu.make_async_copy(
                        zbuf.at[pl.ds(0, size)],
                        xin_ref.at[pl.ds(pl.multiple_of(off, SUBLANES), size)], zsem))
                off = off + (rest & size)
                size //= 2
            return carry

        lax.fori_loop(0, N_EXPERTS, per_expert, 0)

    @pl.when(pl.program_id(0) == 0)
    def _():
        zbuf[...] = jnp.zeros_like(zbuf)
        zero_fill(lambda cp: cp.start())
        zero_fill(lambda cp: cp.wait())

    def issue(t, carry):
        for k in range(TOP_K):
            _row_copy(h_ref, t, xin_ref, pos_ref[0, 0, t * TOP_K + k], sem).start()
        return carry

    lax.fori_loop(0, tm, issue, 0, unroll=DMA_UNROLL)
    for _ in range(TOP_K):
        pltpu.make_async_copy(h_ref, xin_ref.at[pl.ds(0, tm)], sem).wait()


def _dispatch(h3, pos, pad_lo, pad_hi, n_rows):
    T = h3.shape[0]
    tm = min(256, T)
    pos3 = pos.reshape(T // tm, 1, tm * TOP_K)
    grid_spec = pltpu.PrefetchScalarGridSpec(
        num_scalar_prefetch=2,
        grid=(T // tm,),
        in_specs=[
            pl.BlockSpec((1, 1, tm * TOP_K), lambda i, lo, hi: (i, 0, 0), memory_space=pltpu.SMEM),
            pl.BlockSpec((tm, D_MODEL), lambda i, lo, hi: (i, 0)),
        ],
        out_specs=pl.BlockSpec(memory_space=pl.ANY),
        scratch_shapes=[pltpu.VMEM((ZERO_ROWS, D_MODEL), F32), pltpu.SemaphoreType.DMA(()),
                        pltpu.SemaphoreType.DMA(())],
    )
    return pl.pallas_call(
        _dispatch_kernel,
        grid_spec=grid_spec,
        out_shape=jax.ShapeDtypeStruct((n_rows, D_MODEL), F32),
        compiler_params=_params(("arbitrary",)),
        name="dispatch",
    )(pad_lo, pad_hi, pos3, h3)


def _ffn_kernel(be_ref, nreal_ref, x_ref, w1_ref, b1_ref, w2_ref, b2_ref, o_ref, w1b, w2b):
    b = pl.program_id(0)
    e = be_ref[b]
    prev = be_ref[jnp.maximum(b - 1, 0)]

    @pl.when(jnp.logical_or(b == 0, e != prev))
    def _():
        w1b[...] = w1_ref[...].astype(BF16)
        w2b[...] = w2_ref[...].astype(BF16)

    @pl.when(b < nreal_ref[0])
    def _():
        hc = jnp.dot(x_ref[...].astype(BF16), w1b[...], preferred_element_type=F32) + b1_ref[...]
        glu = jnp.minimum(hc[:, :D_FF], SWIGLU_LIMIT)
        lin = jnp.clip(hc[:, D_FF:], -SWIGLU_LIMIT, SWIGLU_LIMIT)
        act = glu * jax.nn.sigmoid(SWIGLU_ALPHA * glu) * (lin + 1.0)
        o_ref[...] = jnp.dot(act.astype(BF16), w2b[...], preferred_element_type=F32) + b2_ref[...]

    @pl.when(b >= nreal_ref[0])
    def _():
        o_ref[...] = jnp.zeros_like(o_ref)


def _ffn(xin, block_e, n_real, w1, b1, w2, b2):
    P = xin.shape[0]
    tb = FFN_BLOCK
    nb = P // tb
    grid_spec = pltpu.PrefetchScalarGridSpec(
        num_scalar_prefetch=2,
        grid=(nb,),
        in_specs=[
            pl.BlockSpec((tb, D_MODEL), lambda b, be, nr: (jnp.minimum(b, nr[0] - 1), 0)),
            pl.BlockSpec((None, D_MODEL, 2 * D_FF), lambda b, be, nr: (be[b], 0, 0)),
            pl.BlockSpec((None, 1, 2 * D_FF), lambda b, be, nr: (be[b], 0, 0)),
            pl.BlockSpec((None, D_FF, D_MODEL), lambda b, be, nr: (be[b], 0, 0)),
            pl.BlockSpec((None, 1, D_MODEL), lambda b, be, nr: (be[b], 0, 0)),
        ],
        out_specs=pl.BlockSpec((tb, D_MODEL), lambda b, be, nr: (b, 0)),
        scratch_shapes=[pltpu.VMEM((D_MODEL, 2 * D_FF), BF16), pltpu.VMEM((D_FF, D_MODEL), BF16)],
    )
    return pl.pallas_call(
        _ffn_kernel,
        grid_spec=grid_spec,
        out_shape=jax.ShapeDtypeStruct((P, D_MODEL), F32),
        compiler_params=_params(("arbitrary",)),
        name="ffn",
    )(block_e, n_real, xin, w1, b1.reshape(N_EXPERTS, 1, 2 * D_FF), w2, b2.reshape(N_EXPERTS, 1, D_MODEL))


def _combine_kernel(pos_ref, x2_ref, gate_ref, g_ref, y_ref, o_ref, buf, sem):
    tm = x2_ref.shape[0]

    def issue(t, carry):
        for k in range(TOP_K):
            _row_copy(y_ref, pos_ref[0, 0, t * TOP_K + k], buf.at[k], t, sem).start()
        return carry

    lax.fori_loop(0, tm, issue, 0, unroll=DMA_UNROLL)
    for k in range(TOP_K):
        pltpu.make_async_copy(y_ref.at[pl.ds(0, tm)], buf.at[k], sem).wait()

    acc = x2_ref[...]
    gate = gate_ref[...]
    for k in range(TOP_K):
        acc = acc + gate[:, k:k + 1] * buf[k]
    o_ref[...] = _rms(acc, g_ref[...])


def _combine(x2, gate, pos, yrows, g_final):
    T = x2.shape[0]
    tm = min(256, T)
    pos3 = pos.reshape(T // tm, 1, tm * TOP_K)
    return pl.pallas_call(
        _combine_kernel,
        grid=(T // tm,),
        in_specs=[
            pl.BlockSpec((1, 1, tm * TOP_K), lambda i: (i, 0, 0), memory_space=pltpu.SMEM),
            pl.BlockSpec((tm, D_MODEL), lambda i: (i, 0)),
            pl.BlockSpec((tm, LANES), lambda i: (i, 0)),
            pl.BlockSpec((1, D_MODEL), lambda i: (0, 0)),
            pl.BlockSpec(memory_space=pl.ANY),
        ],
        out_specs=pl.BlockSpec((tm, D_MODEL), lambda i: (i, 0)),
        out_shape=jax.ShapeDtypeStruct((T, D_MODEL), F32),
        scratch_shapes=[pltpu.VMEM((TOP_K, tm, D_MODEL), F32), pltpu.SemaphoreType.DMA(())],
        compiler_params=_params(("arbitrary",)),
        name="combine",
    )(pos3, x2, gate, g_final, yrows)


def kernel(x, mem, g_mix, w_in, w_sb_o, w_ret_o, w_mix_out, g_xattn, g_mem, w_cq, w_ckv, w_co,
           g_moe, w_router, b_router, w_exp_in, b_exp_in, w_exp_out, b_exp_out, g_final):
    B, S, _ = x.shape
    T = B * S
    assert w_in.shape[0] == 1, "one layer"
    row = lambda v: v.reshape(1, -1)

    w_in_k = jnp.concatenate([w_in[0][:, o:o + w] for _, o, w in _IN_GROUPS], axis=1).astype(BF16)
    proj = _inproj(x.reshape(T, D_MODEL), row(g_mix[0]), w_in_k)
    proj3 = proj.reshape(B, S, IN_W)
    y_sb = _sb_attention(proj3)
    y_ret = _retention(proj3)
    kv = _memkv(mem, row(g_mem[0]), w_ckv[0].astype(BF16))
    x2, h3, idx, gate, rank, cnt = _mix(
        x, proj3, y_sb, y_ret, kv,
        w_sb_o[0].astype(BF16), w_ret_o[0].astype(BF16), w_mix_out[0].astype(BF16),
        w_cq[0].astype(BF16), w_co[0].astype(BF16),
        row(g_xattn[0]), row(g_moe[0]), w_router[0], row(b_router[0]))

    tb = FFN_BLOCK
    n_blocks = (T * TOP_K + N_EXPERTS * (tb - 1) + tb - 1) // tb
    counts = cnt[0].astype(jnp.int32)
    padded = ((counts + tb - 1) // tb) * tb
    pad_end = jnp.cumsum(padded)
    pad_start = pad_end - padded
    idx4 = idx.reshape(T, LANES)[:, :TOP_K]
    pos = (pad_start[idx4] + rank.reshape(T, LANES)[:, :TOP_K]).astype(jnp.int32).reshape(T * TOP_K)
    n_real = (pad_end[-1:] // tb).astype(jnp.int32)
    block_start = jnp.arange(n_blocks, dtype=jnp.int32) * tb
    block_e = jnp.minimum(jnp.sum(pad_end[None, :] <= block_start[:, None], axis=1),
                          N_EXPERTS - 1).astype(jnp.int32)

    pad_lo = (pad_start + counts).astype(jnp.int32)
    pad_hi = jnp.where(jnp.arange(N_EXPERTS) == N_EXPERTS - 1, n_blocks * tb, pad_end).astype(jnp.int32)
    xin = _dispatch(h3.reshape(T, D_MODEL), pos, pad_lo, pad_hi, n_blocks * tb)
    yrows = _ffn(xin, block_e, n_real, w_exp_in[0], b_exp_in[0], w_exp_out[0], b_exp_out[0])
    out = _combine(x2.reshape(T, D_MODEL), gate.reshape(T, LANES), pos, yrows, row(g_final))
    return out.reshape(B, S, D_MODEL)
```

```python
import functools
import math

import numpy as np
import jax
import jax.numpy as jnp
from jax import lax
from jax.experimental import pallas as pl
from jax.experimental.pallas import tpu as pltpu

F32 = jnp.float32
BF16 = jnp.bfloat16

D_MODEL = 1024
SB_HEADS = 8
SB_HEAD_DIM = 64
RET_HEADS = 8
RET_QK_DIM = 64
RET_V_DIM = 128
ROPE_BASE = 10000.0
X_HEADS = 4
X_HEAD_DIM = 256
N_EXPERTS = 32
TOP_K = 4
D_FF = 1024
SWIGLU_LIMIT = 7.0
SWIGLU_ALPHA = 1.702
NORM_EPS = 1e-5

_IN_GROUPS = (("r_v", 2560, 1024), ("r_g", 3584, 1024), ("g_sb", 4608, 1024), ("g_ret", 5632, 1024),
              ("sb_q", 0, 512), ("sb_k", 512, 512), ("sb_v", 1024, 512),
              ("r_q", 1536, 512), ("r_k", 2048, 512))
OFF_R_V, OFF_R_G, OFF_G_SB, OFF_G_RET = 0, 1024, 2048, 3072
OFF_SB_Q, OFF_SB_K, OFF_SB_V, OFF_R_Q, OFF_R_K = 4096, 4608, 5120, 5632, 6144
IN_W = 6656

LANES = 128
VMEM_LIMIT = 56 * 1024 * 1024

SB_TILE = 128
SB_LOG_UNDERFLOW = -88.0
RET_CHUNK = 256
FFN_BLOCK = 512


def _rms(x, g):
    return x * lax.rsqrt(jnp.mean(x * x, axis=-1, keepdims=True) + NORM_EPS) * g


def _params(sem):
    return pltpu.CompilerParams(dimension_semantics=sem, vmem_limit_bytes=VMEM_LIMIT)


INPROJ_COLS = 512


def _inproj_kernel(x_ref, g_ref, w_ref, o_ref):
    h = _rms(x_ref[...], g_ref[...]).astype(BF16)
    for j in range(IN_W // INPROJ_COLS):
        cols = slice(j * INPROJ_COLS, (j + 1) * INPROJ_COLS)
        o_ref[:, cols] = jnp.dot(h, w_ref[:, cols], preferred_element_type=F32).astype(o_ref.dtype)


def _inproj(x2d, g, w_bf16):
    T = x2d.shape[0]
    tm = min(512, T)
    return pl.pallas_call(
        _inproj_kernel,
        grid=(T // tm,),
        in_specs=[
            pl.BlockSpec((tm, D_MODEL), lambda i: (i, 0)),
            pl.BlockSpec((1, D_MODEL), lambda i: (0, 0)),
            pl.BlockSpec((D_MODEL, IN_W), lambda i: (0, 0)),
        ],
        out_specs=pl.BlockSpec((tm, IN_W), lambda i: (i, 0)),
        out_shape=jax.ShapeDtypeStruct((T, IN_W), BF16),
        compiler_params=_params(("parallel",)),
        name="inproj",
    )(x2d, g, w_bf16)


SB_PAIRS = SB_HEADS * SB_HEAD_DIM // LANES


def _sb_kernel(q_ref, k_ref, v_ref, o_ref, q2_ref, carry_ref, acc_ref):
    n = SB_TILE
    qi = pl.program_id(1)
    lane = lax.broadcasted_iota(jnp.int32, (n, LANES), 1)
    first = lane < SB_HEAD_DIM
    row = lax.broadcasted_iota(jnp.int32, (2 * n, n), 0) & (n - 1)
    col = lax.broadcasted_iota(jnp.int32, (2 * n, n), 1)
    strict = col < row
    jj = lax.broadcasted_iota(jnp.int32, (2 * n, 2 * n), 0) & (n - 1)
    cc = lax.broadcasted_iota(jnp.int32, (2 * n, 2 * n), 1)
    suffix = jnp.where((cc >= n) | (jj > cc), 1.0, 0.0).astype(BF16)
    scale = SB_HEAD_DIM ** -0.5

    for p in range(SB_PAIRS):
        qs = (q_ref[:, p * LANES:(p + 1) * LANES].astype(F32) * scale).astype(BF16)
        zeros = jnp.zeros_like(qs)
        q2_ref[p * 2 * n:p * 2 * n + n, :] = jnp.where(first, qs, zeros)
        q2_ref[p * 2 * n + n:(p + 1) * 2 * n, :] = jnp.where(first, zeros, qs)

    def step(kt, diagonal):
        ks = pl.multiple_of(kt * n, n)
        zs, sps = [], []
        for p in range(SB_PAIRS):
            kk = k_ref[pl.ds(ks, n), p * LANES:(p + 1) * LANES]
            z = lax.dot_general(q2_ref[p * 2 * n:(p + 1) * 2 * n, :], kk, (((1,), (1,)), ((), ())),
                                preferred_element_type=F32)
            sp = jnp.maximum(z, 0.0) + jnp.log(1.0 + jnp.exp(-jnp.abs(z)))
            if diagonal:
                sp = jnp.where(strict, sp, 0.0)
            zs.append(z)
            sps.append(sp)
        z = jnp.concatenate(zs, axis=0)
        sp = jnp.concatenate(sps, axis=0)
        hi = sp.astype(BF16)
        lo = (sp - hi.astype(F32)).astype(BF16)
        cs = jnp.dot(jnp.concatenate([hi, lo], axis=1), suffix, preferred_element_type=F32)
        logw = (z - sp) - cs[:, :n]
        if not diagonal:
            logw = logw + carry_ref[...]
        w = jnp.exp(logw)
        if diagonal:
            w = jnp.where(jnp.concatenate([strict] * SB_PAIRS, axis=0), w, 0.0)
        wb = w.astype(BF16)
        for p in range(SB_PAIRS):
            rows = slice(p * 2 * n, (p + 1) * 2 * n)
            vv = v_ref[pl.ds(ks, n), p * LANES:(p + 1) * LANES]
            pv = jnp.dot(wb[rows], vv, preferred_element_type=F32)
            acc_ref[rows, :] = pv if diagonal else acc_ref[rows, :] + pv
        carry = -cs[:, n:] if diagonal else carry_ref[...] - cs[:, n:]
        carry_ref[...] = carry
        return jnp.max(carry)

    def cond(st):
        kt, cmax = st
        return jnp.logical_and(kt >= 0, cmax > SB_LOG_UNDERFLOW)

    def body(st):
        kt, _ = st
        return kt - 1, step(kt, False)

    lax.while_loop(cond, body, (qi - 1, step(qi, True)))

    for p in range(SB_PAIRS):
        o_ref[:, p * LANES:(p + 1) * LANES] = jnp.where(
            first, acc_ref[p * 2 * n:p * 2 * n + n, :], acc_ref[p * 2 * n + n:(p + 1) * 2 * n, :]
        ).astype(o_ref.dtype)


def _sb_attention(proj3):
    B, S, _ = proj3.shape
    n = SB_TILE
    w = SB_HEADS * SB_HEAD_DIM
    qb, kb, vb = OFF_SB_Q // w, OFF_SB_K // w, OFF_SB_V // w
    rows = SB_PAIRS * 2 * n
    return pl.pallas_call(
        _sb_kernel,
        grid=(B, S // n),
        in_specs=[
            pl.BlockSpec((None, n, w), lambda b, i: (b, i, qb)),
            pl.BlockSpec((None, S, w), lambda b, i: (b, 0, kb)),
            pl.BlockSpec((None, S, w), lambda b, i: (b, 0, vb)),
        ],
        out_specs=pl.BlockSpec((None, n, w), lambda b, i: (b, i, 0)),
        out_shape=jax.ShapeDtypeStruct((B, S, w), BF16),
        scratch_shapes=[pltpu.VMEM((rows, LANES), BF16), pltpu.VMEM((rows, LANES), F32),
                        pltpu.VMEM((rows, LANES), F32)],
        compiler_params=_params(("parallel", "arbitrary")),
        name="sb",
    )(proj3, proj3, proj3)


def _ret_tables(S):
    c = RET_CHUNK
    f = np.float32
    inv = f(ROPE_BASE) ** (-np.arange(0, RET_QK_DIM, 2, dtype=f) / f(RET_QK_DIM))
    ang = np.arange(S, dtype=f)[:, None] * inv[None, :]
    cos, sin = np.cos(ang), np.sin(ang)
    reps = LANES // RET_QK_DIM
    cos_t = np.tile(np.concatenate([cos, cos], axis=1), (1, reps))
    sin_t = np.tile(np.concatenate([-sin, sin], axis=1), (1, reps))
    log_g = np.log1p(-np.exp2(-5.0 - np.arange(RET_HEADS, dtype=np.float64)))
    pos = np.arange(c, dtype=np.float64)
    diff = pos[:, None] - pos[None, :]
    decay = np.where(diff >= 0, np.exp(log_g[:, None, None] * np.maximum(diff, 0.0)), 0.0).astype(f)
    q_dec = np.exp(log_g[:, None] * (pos + 1.0)[None, :]).astype(f)
    k_dec = np.exp(log_g[:, None] * (c - 1 - pos)[None, :]).astype(f)
    q_dec = np.ascontiguousarray(np.broadcast_to(q_dec[:, :, None], (RET_HEADS, c, LANES)))
    k_dec = np.ascontiguousarray(np.broadcast_to(k_dec[:, :, None], (RET_HEADS, c, LANES)))
    c_dec = np.ascontiguousarray(np.broadcast_to(
        np.exp(log_g * c).astype(f)[:, None, None], (RET_HEADS, LANES, LANES)))
    return cos_t, sin_t, decay, q_dec, k_dec, c_dec


def _ret_kernel(q_ref, k_ref, v_ref, g_ref, cos_ref, sin_ref, dm_ref, qd_ref, kd_ref, cd_ref,
                o_ref, state):
    c = RET_CHUNK

    @pl.when(pl.program_id(1) == 0)
    def _():
        state[...] = jnp.zeros_like(state)

    lane = lax.broadcasted_iota(jnp.int32, (c, LANES), 1)
    low_half = (lane & (RET_QK_DIM - 1)) < RET_QK_DIM // 2
    cos = cos_ref[...]
    sin = sin_ref[...]

    def rotary(x):
        swapped = jnp.where(low_half, pltpu.roll(x, LANES - RET_QK_DIM // 2, 1),
                            pltpu.roll(x, RET_QK_DIM // 2, 1))
        return x * cos + swapped * sin

    for p in range(RET_HEADS * RET_QK_DIM // LANES):
        rq = rotary(q_ref[:, p * LANES:(p + 1) * LANES].astype(F32))
        rk = rotary(k_ref[:, p * LANES:(p + 1) * LANES].astype(F32)) * RET_QK_DIM ** -0.5
        rkb = rk.astype(BF16)
        for j in range(2):
            h = 2 * p + j
            cols = slice(h * RET_V_DIM, (h + 1) * RET_V_DIM)
            in_head = (lane < RET_QK_DIM) if j == 0 else (lane >= RET_QK_DIM)
            qh = jnp.where(in_head, rq, 0.0)
            vh = v_ref[:, cols]
            s = lax.dot_general(qh.astype(BF16), rkb, (((1,), (1,)), ((), ())),
                                preferred_element_type=F32) * dm_ref[h]
            o = jnp.dot(s.astype(BF16), vh, preferred_element_type=F32)
            o = o + jnp.dot((qh * qd_ref[h]).astype(BF16), state[h].astype(BF16),
                            preferred_element_type=F32)
            kv_new = lax.dot_general((rk * kd_ref[h]).astype(BF16), vh, (((0,), (0,)), ((), ())),
                                     preferred_element_type=F32)
            state[h] = state[h] * cd_ref[h] + kv_new
            mu = jnp.mean(o, axis=-1, keepdims=True)
            d = o - mu
            y = d * lax.rsqrt(jnp.mean(d * d, axis=-1, keepdims=True) + NORM_EPS)
            g = g_ref[:, cols].astype(F32)
            o_ref[:, cols] = (g * jax.nn.sigmoid(g) * y).astype(o_ref.dtype)


def _retention(proj3):
    B, S, _ = proj3.shape
    c = RET_CHUNK
    qkw, vw = RET_HEADS * RET_QK_DIM, RET_HEADS * RET_V_DIM
    qb, kb, vb, gb = OFF_R_Q // qkw, OFF_R_K // qkw, OFF_R_V // vw, OFF_R_G // vw
    cos_t, sin_t, decay, q_dec, k_dec, c_dec = _ret_tables(S)
    const3 = lambda b, i: (0, 0, 0)
    return pl.pallas_call(
        _ret_kernel,
        grid=(B, S // c),
        in_specs=[
            pl.BlockSpec((None, c, qkw), lambda b, i: (b, i, qb)),
            pl.BlockSpec((None, c, qkw), lambda b, i: (b, i, kb)),
            pl.BlockSpec((None, c, vw), lambda b, i: (b, i, vb)),
            pl.BlockSpec((None, c, vw), lambda b, i: (b, i, gb)),
            pl.BlockSpec((c, LANES), lambda b, i: (i, 0)),
            pl.BlockSpec((c, LANES), lambda b, i: (i, 0)),
            pl.BlockSpec((RET_HEADS, c, c), const3),
            pl.BlockSpec((RET_HEADS, c, LANES), const3),
            pl.BlockSpec((RET_HEADS, c, LANES), const3),
            pl.BlockSpec((RET_HEADS, LANES, LANES), const3),
        ],
        out_specs=pl.BlockSpec((None, c, vw), lambda b, i: (b, i, 0)),
        out_shape=jax.ShapeDtypeStruct((B, S, vw), BF16),
        scratch_shapes=[pltpu.VMEM((RET_HEADS, LANES, LANES), F32)],
        compiler_params=_params(("parallel", "arbitrary")),
        name="ret",
    )(proj3, proj3, proj3, proj3, cos_t, sin_t, decay, q_dec, k_dec, c_dec)


def _memkv_kernel(m_ref, g_ref, w_ref, o_ref):
    h = _rms(m_ref[...], g_ref[...]).astype(BF16)
    o_ref[...] = jnp.dot(h, w_ref[...], preferred_element_type=F32).astype(o_ref.dtype)


def _memkv(mem, g, w_bf16):
    B, M, _ = mem.shape
    return pl.pallas_call(
        _memkv_kernel,
        grid=(B,),
        in_specs=[
            pl.BlockSpec((None, M, D_MODEL), lambda b: (b, 0, 0)),
            pl.BlockSpec((1, D_MODEL), lambda b: (0, 0)),
            pl.BlockSpec((D_MODEL, 2 * D_MODEL), lambda b: (0, 0)),
        ],
        out_specs=pl.BlockSpec((None, M, 2 * D_MODEL), lambda b: (b, 0, 0)),
        out_shape=jax.ShapeDtypeStruct((B, M, 2 * D_MODEL), BF16),
        compiler_params=_params(("parallel",)),
        name="memkv",
    )(mem, g, w_bf16)


MIX_CHUNKS = 2


def _lane_pack(cols, rows, dtype):
    lane = lax.broadcasted_iota(jnp.int32, (rows, LANES), 1)
    out = jnp.zeros((rows, LANES), dtype)
    for k, c in enumerate(cols):
        out = jnp.where(lane == k, c.astype(dtype), out)
    return out


def _mix_kernel(x_ref, ysb_ref, yret_ref, gsb_ref, gret_ref, kv_ref,
                wsbo_ref, wreto_ref, wmix_ref, wcq_ref, wco_ref,
                gx_ref, gm_ref, wr_ref, br_ref,
                x2_ref, h3_ref, idx_ref, gate_ref, rank_ref, cnt_ref, cnt_scr):
    tm = x_ref.shape[0]
    cr = tm // MIX_CHUNKS
    chunks = [slice(c * cr, (c + 1) * cr) for c in range(MIX_CHUNKS)]
    dot = functools.partial(jnp.dot, preferred_element_type=F32)

    @pl.when(jnp.logical_and(pl.program_id(0) == 0, pl.program_id(1) == 0))
    def _():
        cnt_scr[...] = jnp.zeros_like(cnt_scr)

    def sigmoid(g):
        return (0.5 * jnp.tanh(0.5 * g) + 0.5).astype(F32)

    a = [dot(ysb_ref[s, :], wsbo_ref[...]) for s in chunks]
    b = [dot(yret_ref[s, :], wreto_ref[...]) for s in chunks]
    merged = [sigmoid(gsb_ref[s, :]) * ac + sigmoid(gret_ref[s, :]) * bc
              for s, ac, bc in zip(chunks, a, b)]
    x1 = [x_ref[s, :] + dot(m.astype(BF16), wmix_ref[...]) for s, m in zip(chunks, merged)]

    h2 = [_rms(v, gx_ref[...]).astype(BF16) for v in x1]
    q = [(dot(v, wcq_ref[...]) * X_HEAD_DIM ** -0.5).astype(BF16) for v in h2]
    heads = [[] for _ in chunks]
    for hh in range(X_HEADS):
        lo = hh * X_HEAD_DIM
        kh = kv_ref[:, lo:lo + X_HEAD_DIM]
        vh = kv_ref[:, D_MODEL + lo:D_MODEL + lo + X_HEAD_DIM]
        s = [lax.dot_general(qc[:, lo:lo + X_HEAD_DIM], kh, (((1,), (1,)), ((), ())),
                             preferred_element_type=F32) for qc in q]
        p = [jnp.exp(sc - jnp.max(sc, axis=-1, keepdims=True)) for sc in s]
        p = [pc * (1.0 / jnp.sum(pc, axis=-1, keepdims=True)) for pc in p]
        for c, pc in enumerate(p):
            heads[c].append(dot(pc.astype(BF16), vh).astype(BF16))
    o = [jnp.concatenate(hc, axis=1) for hc in heads]
    x2 = [v + dot(oc, wco_ref[...]) for v, oc in zip(x1, o)]
    for s, v in zip(chunks, x2):
        x2_ref[s, :] = v

    h3 = [_rms(v, gm_ref[...]) for v in x2]
    for s, v in zip(chunks, h3):
        h3_ref[s, :] = v
    logits = []
    for v in h3:
        hi = v.astype(BF16)
        lo = (v - hi.astype(F32)).astype(BF16)
        r = dot(jnp.concatenate([hi, lo], axis=0), wr_ref[...])
        logits.append(r[:cr, :N_EXPERTS] + r[:cr, N_EXPERTS:] + r[cr:, :N_EXPERTS] + br_ref[...])

    e_iota = lax.broadcasted_iota(jnp.int32, (cr, N_EXPERTS), 1).astype(F32)
    r_i = lax.broadcasted_iota(jnp.int32, (cr, cr), 0)
    c_i = lax.broadcasted_iota(jnp.int32, (cr, cr), 1)
    before = jnp.where(c_i < r_i, 1.0, 0.0).astype(BF16)
    count = cnt_scr[...]
    for s, rem in zip(chunks, logits):
        vals, idxs = [], []
        for _ in range(TOP_K):
            m = jnp.max(rem, axis=-1, keepdims=True)
            ik = jnp.min(jnp.where(rem == m, e_iota, float(N_EXPERTS)), axis=-1, keepdims=True)
            vals.append(m)
            idxs.append(ik)
            rem = jnp.where(e_iota == ik, -jnp.inf, rem)
        ex = [jnp.exp(v - vals[0]) for v in vals]
        inv_den = 1.0 / (ex[0] + ex[1] + ex[2] + ex[3])
        gates = [e * inv_den for e in ex]

        chosen = [(e_iota == ik) for ik in idxs]
        member = jnp.zeros((cr, N_EXPERTS), F32)
        for ch in chosen:
            member = member + jnp.where(ch, 1.0, 0.0)
        prefix = dot(before, member.astype(BF16)) + count
        ranks = [jnp.sum(jnp.where(ch, prefix, 0.0), axis=-1, keepdims=True) for ch in chosen]
        count = count + jnp.sum(member, axis=0, keepdims=True)

        idx_ref[s, :] = _lane_pack(idxs, cr, jnp.int32)
        rank_ref[s, :] = _lane_pack(ranks, cr, jnp.int32)
        gate_ref[s, :] = _lane_pack(gates, cr, F32)
    cnt_scr[...] = count
    cnt_ref[...] = count


def _mix(x, proj3, y_sb, y_ret, kv, wsbo, wreto, wmix, wcq, wco, gx, gm, wr, br):
    B, S, _ = x.shape
    tm = min(512, S)
    M = kv.shape[1]
    gsb_b, gret_b = OFF_G_SB // D_MODEL, OFF_G_RET // D_MODEL
    tok = lambda b, i: (b, i, 0)
    const2 = lambda b, i: (0, 0)
    row_out = lambda w, dt: jax.ShapeDtypeStruct((B, S, w), dt)
    outs = pl.pallas_call(
        _mix_kernel,
        grid=(B, S // tm),
        in_specs=[
            pl.BlockSpec((None, tm, D_MODEL), tok),
            pl.BlockSpec((None, tm, SB_HEADS * SB_HEAD_DIM), tok),
            pl.BlockSpec((None, tm, D_MODEL), tok),
            pl.BlockSpec((None, tm, D_MODEL), lambda b, i: (b, i, gsb_b)),
            pl.BlockSpec((None, tm, D_MODEL), lambda b, i: (b, i, gret_b)),
            pl.BlockSpec((None, M, 2 * D_MODEL), lambda b, i: (b, 0, 0)),
            pl.BlockSpec((SB_HEADS * SB_HEAD_DIM, D_MODEL), const2),
            pl.BlockSpec((D_MODEL, D_MODEL), const2),
            pl.BlockSpec((D_MODEL, D_MODEL), const2),
            pl.BlockSpec((D_MODEL, D_MODEL), const2),
            pl.BlockSpec((D_MODEL, D_MODEL), const2),
            pl.BlockSpec((1, D_MODEL), const2),
            pl.BlockSpec((1, D_MODEL), const2),
            pl.BlockSpec((D_MODEL, 2 * N_EXPERTS), const2),
            pl.BlockSpec((1, N_EXPERTS), const2),
        ],
        out_specs=[
            pl.BlockSpec((None, tm, D_MODEL), tok),
            pl.BlockSpec((None, tm, D_MODEL), tok),
            pl.BlockSpec((None, tm, LANES), tok),
            pl.BlockSpec((None, tm, LANES), tok),
            pl.BlockSpec((None, tm, LANES), tok),
            pl.BlockSpec((1, N_EXPERTS), const2),
        ],
        out_shape=[
            row_out(D_MODEL, F32), row_out(D_MODEL, F32),
            row_out(LANES, jnp.int32), row_out(LANES, F32), row_out(LANES, jnp.int32),
            jax.ShapeDtypeStruct((1, N_EXPERTS), F32),
        ],
        scratch_shapes=[pltpu.VMEM((1, N_EXPERTS), F32)],
        compiler_params=_params(("arbitrary", "arbitrary")),
        name="mix",
    )(x, y_sb, y_ret, proj3, proj3, kv, wsbo, wreto, wmix, wcq, wco, gx, gm, wr, br)
    return outs


def _row_copy(src, s, dst, d, sem):
    return pltpu.make_async_copy(src.at[pl.ds(s, 1)], dst.at[pl.ds(d, 1)], sem)


DMA_UNROLL = 8


ZERO_ROWS = 256
SUBLANES = 8


def _dispatch_kernel(lo_ref, hi_ref, pos_ref, h_ref, xin_ref, zbuf, sem, zsem):
    tm = h_ref.shape[0]

    def zero_fill(act):
        def per_expert(e, carry):
            lo = lo_ref[e]
            hi = hi_ref[e]
            head_end = jnp.minimum((lo + SUBLANES - 1) // SUBLANES * SUBLANES, hi)

            def single(r, c):
                act(_row_copy(zbuf, 0, xin_ref, r, zsem))
                return c

            lax.fori_loop(lo, head_end, single, 0)

            def chunk(i, off):
                act(pltpu.make_async_copy(zbuf, xin_ref.at[pl.ds(pl.multiple_of(off, SUBLANES), ZERO_ROWS)], zsem))
                return off + ZERO_ROWS

            off = lax.fori_loop(0, (hi - head_end) // ZERO_ROWS, chunk, head_end)
            rest = hi - off
            size = ZERO_ROWS // 2
            while size >= SUBLANES:
                @pl.when((rest & size) != 0)
                def _(off=off, size=size):
                    act(pltpu.make_async_copy(
                        zbuf.at[pl.ds(0, size)],
                        xin_ref.at[pl.ds(pl.multiple_of(off, SUBLANES), size)], zsem))
                off = off + (rest & size)
                size //= 2
            return carry

        lax.fori_loop(0, N_EXPERTS, per_expert, 0)

    @pl.when(pl.program_id(0) == 0)
    def _():
        zbuf[...] = jnp.zeros_like(zbuf)
        zero_fill(lambda cp: cp.start())
        zero_fill(lambda cp: cp.wait())

    def issue(t, carry):
        for k in range(TOP_K):
            _row_copy(h_ref, t, xin_ref, pos_ref[0, 0, t * TOP_K + k], sem).start()
        return carry

    lax.fori_loop(0, tm, issue, 0, unroll=DMA_UNROLL)
    for _ in range(TOP_K):
        pltpu.make_async_copy(h_ref, xin_ref.at[pl.ds(0, tm)], sem).wait()


def _dispatch(h3, pos, pad_lo, pad_hi, n_rows):
    T = h3.shape[0]
    tm = min(256, T)
    pos3 = pos.reshape(T // tm, 1, tm * TOP_K)
    grid_spec = pltpu.PrefetchScalarGridSpec(
        num_scalar_prefetch=2,
        grid=(T // tm,),
        in_specs=[
            pl.BlockSpec((1, 1, tm * TOP_K), lambda i, lo, hi: (i, 0, 0), memory_space=pltpu.SMEM),
            pl.BlockSpec((tm, D_MODEL), lambda i, lo, hi: (i, 0)),
        ],
        out_specs=pl.BlockSpec(memory_space=pl.ANY),
        scratch_shapes=[pltpu.VMEM((ZERO_ROWS, D_MODEL), F32), pltpu.SemaphoreType.DMA(()),
                        pltpu.SemaphoreType.DMA(())],
    )
    return pl.pallas_call(
        _dispatch_kernel,
        grid_spec=grid_spec,
        out_shape=jax.ShapeDtypeStruct((n_rows, D_MODEL), F32),
        compiler_params=_params(("arbitrary",)),
        name="dispatch",
    )(pad_lo, pad_hi, pos3, h3)


def _ffn_kernel(be_ref, nreal_ref, x_ref, w1_ref, b1_ref, w2_ref, b2_ref, o_ref, w1b, w2b):
    b = pl.program_id(0)
    e = be_ref[b]
    prev = be_ref[jnp.maximum(b - 1, 0)]

    @pl.when(jnp.logical_or(b == 0, e != prev))
    def _():
        w1b[...] = w1_ref[...].astype(BF16)
        w2b[...] = w2_ref[...].astype(BF16)

    @pl.when(b < nreal_ref[0])
    def _():
        hc = jnp.dot(x_ref[...].astype(BF16), w1b[...], preferred_element_type=F32) + b1_ref[...]
        glu = jnp.minimum(hc[:, :D_FF], SWIGLU_LIMIT)
        lin = jnp.clip(hc[:, D_FF:], -SWIGLU_LIMIT, SWIGLU_LIMIT)
        act = glu * jax.nn.sigmoid(SWIGLU_ALPHA * glu) * (lin + 1.0)
        o_ref[...] = jnp.dot(act.astype(BF16), w2b[...], preferred_element_type=F32) + b2_ref[...]

    @pl.when(b >= nreal_ref[0])
    def _():
        o_ref[...] = jnp.zeros_like(o_ref)


def _ffn(xin, block_e, n_real, w1, b1, w2, b2):
    P = xin.shape[0]
    tb = FFN_BLOCK
    nb = P // tb
    grid_spec = pltpu.PrefetchScalarGridSpec(
        num_scalar_prefetch=2,
        grid=(nb,),
        in_specs=[
            pl.BlockSpec((tb, D_MODEL), lambda b, be, nr: (jnp.maximum(jnp.minimum(b, nr[0] - 1), 0), 0)),
            pl.BlockSpec((None, D_MODEL, 2 * D_FF), lambda b, be, nr: (be[b], 0, 0)),
            pl.BlockSpec((None, 1, 2 * D_FF), lambda b, be, nr: (be[b], 0, 0)),
            pl.BlockSpec((None, D_FF, D_MODEL), lambda b, be, nr: (be[b], 0, 0)),
            pl.BlockSpec((None, 1, D_MODEL), lambda b, be, nr: (be[b], 0, 0)),
        ],
        out_specs=pl.BlockSpec((tb, D_MODEL), lambda b, be, nr: (b, 0)),
        scratch_shapes=[pltpu.VMEM((D_MODEL, 2 * D_FF), BF16), pltpu.VMEM((D_FF, D_MODEL), BF16)],
    )
    return pl.pallas_call(
        _ffn_kernel,
        grid_spec=grid_spec,
        out_shape=jax.ShapeDtypeStruct((P, D_MODEL), F32),
        compiler_params=_params(("arbitrary",)),
        name="ffn",
    )(block_e, n_real, xin, w1, b1.reshape(N_EXPERTS, 1, 2 * D_FF), w2, b2.reshape(N_EXPERTS, 1, D_MODEL))


def _combine_kernel(pos_ref, nxt_ref, x2_ref, gate_ref, g_ref, y_ref, o_ref, buf, sem):
    tm = x2_ref.shape[0]
    i = pl.program_id(0)
    slot = i % 2

    def gather(p_ref, s):
        def issue(t, carry):
            for k in range(TOP_K):
                _row_copy(y_ref, p_ref[0, 0, t * TOP_K + k], buf.at[s, k], t, sem.at[s]).start()
            return carry
        lax.fori_loop(0, tm, issue, 0, unroll=DMA_UNROLL)

    @pl.when(i == 0)
    def _():
        gather(pos_ref, slot)

    @pl.when(i + 1 < pl.num_programs(0))
    def _():
        gather(nxt_ref, 1 - slot)

    for k in range(TOP_K):
        pltpu.make_async_copy(y_ref.at[pl.ds(0, tm)], buf.at[slot, k], sem.at[slot]).wait()

    acc = x2_ref[...]
    gate = gate_ref[...]
    for k in range(TOP_K):
        acc = acc + gate[:, k:k + 1] * buf[slot, k]
    o_ref[...] = _rms(acc, g_ref[...])


def _combine(x2, gate, pos, yrows, g_final):
    T = x2.shape[0]
    tm = min(256, T)
    n = T // tm
    pos3 = pos.reshape(n, 1, tm * TOP_K)
    return pl.pallas_call(
        _combine_kernel,
        grid=(n,),
        in_specs=[
            pl.BlockSpec((1, 1, tm * TOP_K), lambda i: (i, 0, 0), memory_space=pltpu.SMEM),
            pl.BlockSpec((1, 1, tm * TOP_K), lambda i: (jnp.minimum(i + 1, n - 1), 0, 0),
                         memory_space=pltpu.SMEM),
            pl.BlockSpec((tm, D_MODEL), lambda i: (i, 0)),
            pl.BlockSpec((tm, LANES), lambda i: (i, 0)),
            pl.BlockSpec((1, D_MODEL), lambda i: (0, 0)),
            pl.BlockSpec(memory_space=pl.ANY),
        ],
        out_specs=pl.BlockSpec((tm, D_MODEL), lambda i: (i, 0)),
        out_shape=jax.ShapeDtypeStruct((T, D_MODEL), F32),
        scratch_shapes=[pltpu.VMEM((2, TOP_K, tm, D_MODEL), F32), pltpu.SemaphoreType.DMA((2,))],
        compiler_params=_params(("arbitrary",)),
        name="combine",
    )(pos3, pos3, x2, gate, g_final, yrows)


def kernel(x, mem, g_mix, w_in, w_sb_o, w_ret_o, w_mix_out, g_xattn, g_mem, w_cq, w_ckv, w_co,
           g_moe, w_router, b_router, w_exp_in, b_exp_in, w_exp_out, b_exp_out, g_final):
    B, S, _ = x.shape
    T = B * S
    assert w_in.shape[0] == 1, "one layer"
    row = lambda v: v.reshape(1, -1)

    w_in_k = jnp.concatenate([w_in[0][:, o:o + w] for _, o, w in _IN_GROUPS], axis=1).astype(BF16)
    proj = _inproj(x.reshape(T, D_MODEL), row(g_mix[0]), w_in_k)
    proj3 = proj.reshape(B, S, IN_W)
    y_sb = _sb_attention(proj3)
    y_ret = _retention(proj3)
    kv = _memkv(mem, row(g_mem[0]), w_ckv[0].astype(BF16))
    wr_hi = w_router[0].astype(BF16)
    wr_lo = (w_router[0] - wr_hi.astype(F32)).astype(BF16)
    x2, h3, idx, gate, rank, cnt = _mix(
        x, proj3, y_sb, y_ret, kv,
        w_sb_o[0].astype(BF16), w_ret_o[0].astype(BF16), w_mix_out[0].astype(BF16),
        w_cq[0].astype(BF16), w_co[0].astype(BF16),
        row(g_xattn[0]), row(g_moe[0]), jnp.concatenate([wr_hi, wr_lo], axis=1), row(b_router[0]))

    tb = FFN_BLOCK
    n_blocks = (T * TOP_K + N_EXPERTS * (tb - 1) + tb - 1) // tb
    counts = cnt[0].astype(jnp.int32)
    padded = ((counts + tb - 1) // tb) * tb
    pad_end = jnp.cumsum(padded)
    pad_start = pad_end - padded
    idx4 = idx.reshape(T, LANES)[:, :TOP_K]
    start4 = jnp.sum(jnp.where(idx4[:, :, None] == jnp.arange(N_EXPERTS), pad_start, 0), axis=-1)
    pos = (start4 + rank.reshape(T, LANES)[:, :TOP_K]).astype(jnp.int32).reshape(T * TOP_K)
    n_real = (pad_end[-1:] // tb).astype(jnp.int32)
    block_start = jnp.arange(n_blocks, dtype=jnp.int32) * tb
    block_e = jnp.minimum(jnp.sum(pad_end[None, :] <= block_start[:, None], axis=1),
                          N_EXPERTS - 1).astype(jnp.int32)

    pad_lo = (pad_start + counts).astype(jnp.int32)
    pad_hi = jnp.where(jnp.arange(N_EXPERTS) == N_EXPERTS - 1, n_blocks * tb, pad_end).astype(jnp.int32)
    xin = _dispatch(h3.reshape(T, D_MODEL), pos, pad_lo, pad_hi, n_blocks * tb)
    yrows = _ffn(xin, block_e, n_real, w_exp_in[0], b_exp_in[0], w_exp_out[0], b_exp_out[0])
    out = _combine(x2.reshape(T, D_MODEL), gate.reshape(T, LANES), pos, yrows, row(g_final))
    return out.reshape(B, S, D_MODEL)
```

```python
import functools
import math

import numpy as np
import jax
import jax.numpy as jnp
from jax import lax
from jax.experimental import pallas as pl
from jax.experimental.pallas import tpu as pltpu

F32 = jnp.float32
BF16 = jnp.bfloat16

D_MODEL = 1024
SB_HEADS = 8
SB_HEAD_DIM = 64
RET_HEADS = 8
RET_QK_DIM = 64
RET_V_DIM = 128
ROPE_BASE = 10000.0
X_HEADS = 4
X_HEAD_DIM = 256
N_EXPERTS = 32
TOP_K = 4
D_FF = 1024
SWIGLU_LIMIT = 7.0
SWIGLU_ALPHA = 1.702
NORM_EPS = 1e-5

_IN_GROUPS = (("r_v", 2560, 1024), ("r_g", 3584, 1024), ("g_sb", 4608, 1024), ("g_ret", 5632, 1024),
              ("sb_q", 0, 512), ("sb_k", 512, 512), ("sb_v", 1024, 512),
              ("r_q", 1536, 512), ("r_k", 2048, 512))
OFF_R_V, OFF_R_G, OFF_G_SB, OFF_G_RET = 0, 1024, 2048, 3072
OFF_SB_Q, OFF_SB_K, OFF_SB_V, OFF_R_Q, OFF_R_K = 4096, 4608, 5120, 5632, 6144
IN_W = 6656

LANES = 128
VMEM_LIMIT = 56 * 1024 * 1024

SB_TILE = 128
SB_LOG_UNDERFLOW = -88.0
RET_CHUNK = 256
FFN_BLOCK = 256


def _rms(x, g):
    return x * lax.rsqrt(jnp.mean(x * x, axis=-1, keepdims=True) + NORM_EPS) * g


def _params(sem):
    return pltpu.CompilerParams(dimension_semantics=sem, vmem_limit_bytes=VMEM_LIMIT)


INPROJ_COLS = 512


def _inproj_kernel(x_ref, g_ref, w_ref, o_ref):
    h = _rms(x_ref[...], g_ref[...]).astype(BF16)
    for j in range(IN_W // INPROJ_COLS):
        cols = slice(j * INPROJ_COLS, (j + 1) * INPROJ_COLS)
        o_ref[:, cols] = jnp.dot(h, w_ref[:, cols], preferred_element_type=F32).astype(o_ref.dtype)


def _inproj(x2d, g, w_bf16):
    T = x2d.shape[0]
    tm = min(512, T)
    return pl.pallas_call(
        _inproj_kernel,
        grid=(T // tm,),
        in_specs=[
            pl.BlockSpec((tm, D_MODEL), lambda i: (i, 0)),
            pl.BlockSpec((1, D_MODEL), lambda i: (0, 0)),
            pl.BlockSpec((D_MODEL, IN_W), lambda i: (0, 0)),
        ],
        out_specs=pl.BlockSpec((tm, IN_W), lambda i: (i, 0)),
        out_shape=jax.ShapeDtypeStruct((T, IN_W), BF16),
        compiler_params=_params(("parallel",)),
        name="inproj",
    )(x2d, g, w_bf16)


SB_PAIRS = SB_HEADS * SB_HEAD_DIM // LANES


def _sb_kernel(q_ref, k_ref, v_ref, o_ref, q2_ref, carry_ref, acc_ref):
    n = SB_TILE
    qi = pl.program_id(1)
    lane = lax.broadcasted_iota(jnp.int32, (n, LANES), 1)
    first = lane < SB_HEAD_DIM
    row = lax.broadcasted_iota(jnp.int32, (2 * n, n), 0) & (n - 1)
    col = lax.broadcasted_iota(jnp.int32, (2 * n, n), 1)
    strict = col < row
    jj = lax.broadcasted_iota(jnp.int32, (2 * n, 2 * n), 0) & (n - 1)
    cc = lax.broadcasted_iota(jnp.int32, (2 * n, 2 * n), 1)
    suffix = jnp.where((cc >= n) | (jj > cc), 1.0, 0.0).astype(BF16)
    scale = SB_HEAD_DIM ** -0.5

    for p in range(SB_PAIRS):
        qs = (q_ref[:, p * LANES:(p + 1) * LANES].astype(F32) * scale).astype(BF16)
        zeros = jnp.zeros_like(qs)
        q2_ref[p * 2 * n:p * 2 * n + n, :] = jnp.where(first, qs, zeros)
        q2_ref[p * 2 * n + n:(p + 1) * 2 * n, :] = jnp.where(first, zeros, qs)

    def step(kt, diagonal):
        ks = pl.multiple_of(kt * n, n)
        zs, sps = [], []
        for p in range(SB_PAIRS):
            kk = k_ref[pl.ds(ks, n), p * LANES:(p + 1) * LANES]
            z = lax.dot_general(q2_ref[p * 2 * n:(p + 1) * 2 * n, :], kk, (((1,), (1,)), ((), ())),
                                preferred_element_type=F32)
            sp = jnp.maximum(z, 0.0) + jnp.log(1.0 + jnp.exp(-jnp.abs(z)))
            if diagonal:
                sp = jnp.where(strict, sp, 0.0)
            zs.append(z)
            sps.append(sp)
        z = jnp.concatenate(zs, axis=0)
        sp = jnp.concatenate(sps, axis=0)
        hi = sp.astype(BF16)
        lo = (sp - hi.astype(F32)).astype(BF16)
        cs = jnp.dot(jnp.concatenate([hi, lo], axis=1), suffix, preferred_element_type=F32)
        logw = (z - sp) - cs[:, :n]
        if not diagonal:
            logw = logw + carry_ref[...]
        w = jnp.exp(logw)
        if diagonal:
            w = jnp.where(jnp.concatenate([strict] * SB_PAIRS, axis=0), w, 0.0)
        wb = w.astype(BF16)
        for p in range(SB_PAIRS):
            rows = slice(p * 2 * n, (p + 1) * 2 * n)
            vv = v_ref[pl.ds(ks, n), p * LANES:(p + 1) * LANES]
            pv = jnp.dot(wb[rows], vv, preferred_element_type=F32)
            acc_ref[rows, :] = pv if diagonal else acc_ref[rows, :] + pv
        carry = -cs[:, n:] if diagonal else carry_ref[...] - cs[:, n:]
        carry_ref[...] = carry
        return jnp.max(carry)

    def cond(st):
        kt, cmax = st
        return jnp.logical_and(kt >= 0, cmax > SB_LOG_UNDERFLOW)

    def body(st):
        kt, _ = st
        return kt - 1, step(kt, False)

    lax.while_loop(cond, body, (qi - 1, step(qi, True)))

    for p in range(SB_PAIRS):
        o_ref[:, p * LANES:(p + 1) * LANES] = jnp.where(
            first, acc_ref[p * 2 * n:p * 2 * n + n, :], acc_ref[p * 2 * n + n:(p + 1) * 2 * n, :]
        ).astype(o_ref.dtype)


def _sb_attention(proj3):
    B, S, _ = proj3.shape
    n = SB_TILE
    w = SB_HEADS * SB_HEAD_DIM
    qb, kb, vb = OFF_SB_Q // w, OFF_SB_K // w, OFF_SB_V // w
    rows = SB_PAIRS * 2 * n
    return pl.pallas_call(
        _sb_kernel,
        grid=(B, S // n),
        in_specs=[
            pl.BlockSpec((None, n, w), lambda b, i: (b, i, qb)),
            pl.BlockSpec((None, S, w), lambda b, i: (b, 0, kb)),
            pl.BlockSpec((None, S, w), lambda b, i: (b, 0, vb)),
        ],
        out_specs=pl.BlockSpec((None, n, w), lambda b, i: (b, i, 0)),
        out_shape=jax.ShapeDtypeStruct((B, S, w), BF16),
        scratch_shapes=[pltpu.VMEM((rows, LANES), BF16), pltpu.VMEM((rows, LANES), F32),
                        pltpu.VMEM((rows, LANES), F32)],
        compiler_params=_params(("parallel", "arbitrary")),
        name="sb",
    )(proj3, proj3, proj3)


def _ret_tables(S):
    c = RET_CHUNK
    f = np.float32
    inv = f(ROPE_BASE) ** (-np.arange(0, RET_QK_DIM, 2, dtype=f) / f(RET_QK_DIM))
    ang = np.arange(S, dtype=f)[:, None] * inv[None, :]
    cos, sin = np.cos(ang), np.sin(ang)
    reps = LANES // RET_QK_DIM
    cos_t = np.tile(np.concatenate([cos, cos], axis=1), (1, reps))
    sin_t = np.tile(np.concatenate([-sin, sin], axis=1), (1, reps))
    log_g = np.log1p(-np.exp2(-5.0 - np.arange(RET_HEADS, dtype=np.float64)))
    pos = np.arange(c, dtype=np.float64)
    diff = pos[:, None] - pos[None, :]
    decay = np.where(diff >= 0, np.exp(log_g[:, None, None] * np.maximum(diff, 0.0)), 0.0).astype(f)
    q_dec = np.exp(log_g[:, None] * (pos + 1.0)[None, :]).astype(f)
    k_dec = np.exp(log_g[:, None] * (c - 1 - pos)[None, :]).astype(f)
    q_dec = np.ascontiguousarray(np.broadcast_to(q_dec[:, :, None], (RET_HEADS, c, LANES)))
    k_dec = np.ascontiguousarray(np.broadcast_to(k_dec[:, :, None], (RET_HEADS, c, LANES)))
    c_dec = np.ascontiguousarray(np.broadcast_to(
        np.exp(log_g * c).astype(f)[:, None, None], (RET_HEADS, LANES, LANES)))
    return cos_t, sin_t, decay, q_dec, k_dec, c_dec


def _ret_kernel(q_ref, k_ref, v_ref, g_ref, cos_ref, sin_ref, dm_ref, qd_ref, kd_ref, cd_ref,
                o_ref, state):
    c = RET_CHUNK

    @pl.when(pl.program_id(1) == 0)
    def _():
        state[...] = jnp.zeros_like(state)

    lane = lax.broadcasted_iota(jnp.int32, (c, LANES), 1)
    low_half = (lane & (RET_QK_DIM - 1)) < RET_QK_DIM // 2
    cos = cos_ref[...]
    sin = sin_ref[...]

    def rotary(x):
        swapped = jnp.where(low_half, pltpu.roll(x, LANES - RET_QK_DIM // 2, 1),
                            pltpu.roll(x, RET_QK_DIM // 2, 1))
        return x * cos + swapped * sin

    for p in range(RET_HEADS * RET_QK_DIM // LANES):
        rq = rotary(q_ref[:, p * LANES:(p + 1) * LANES].astype(F32))
        rk = rotary(k_ref[:, p * LANES:(p + 1) * LANES].astype(F32)) * RET_QK_DIM ** -0.5
        rkb = rk.astype(BF16)
        for j in range(2):
            h = 2 * p + j
            cols = slice(h * RET_V_DIM, (h + 1) * RET_V_DIM)
            in_head = (lane < RET_QK_DIM) if j == 0 else (lane >= RET_QK_DIM)
            qh = jnp.where(in_head, rq, 0.0)
            vh = v_ref[:, cols]
            s = lax.dot_general(qh.astype(BF16), rkb, (((1,), (1,)), ((), ())),
                                preferred_element_type=F32) * dm_ref[h]
            o = jnp.dot(s.astype(BF16), vh, preferred_element_type=F32)
            o = o + jnp.dot((qh * qd_ref[h]).astype(BF16), state[h].astype(BF16),
                            preferred_element_type=F32)
            kv_new = lax.dot_general((rk * kd_ref[h]).astype(BF16), vh, (((0,), (0,)), ((), ())),
                                     preferred_element_type=F32)
            state[h] = state[h] * cd_ref[h] + kv_new
            mu = jnp.mean(o, axis=-1, keepdims=True)
            d = o - mu
            y = d * lax.rsqrt(jnp.mean(d * d, axis=-1, keepdims=True) + NORM_EPS)
            g = g_ref[:, cols].astype(F32)
            o_ref[:, cols] = (g * jax.nn.sigmoid(g) * y).astype(o_ref.dtype)


def _retention(proj3):
    B, S, _ = proj3.shape
    c = RET_CHUNK
    qkw, vw = RET_HEADS * RET_QK_DIM, RET_HEADS * RET_V_DIM
    qb, kb, vb, gb = OFF_R_Q // qkw, OFF_R_K // qkw, OFF_R_V // vw, OFF_R_G // vw
    cos_t, sin_t, decay, q_dec, k_dec, c_dec = _ret_tables(S)
    const3 = lambda b, i: (0, 0, 0)
    return pl.pallas_call(
        _ret_kernel,
        grid=(B, S // c),
        in_specs=[
            pl.BlockSpec((None, c, qkw), lambda b, i: (b, i, qb)),
            pl.BlockSpec((None, c, qkw), lambda b, i: (b, i, kb)),
            pl.BlockSpec((None, c, vw), lambda b, i: (b, i, vb)),
            pl.BlockSpec((None, c, vw), lambda b, i: (b, i, gb)),
            pl.BlockSpec((c, LANES), lambda b, i: (i, 0)),
            pl.BlockSpec((c, LANES), lambda b, i: (i, 0)),
            pl.BlockSpec((RET_HEADS, c, c), const3),
            pl.BlockSpec((RET_HEADS, c, LANES), const3),
            pl.BlockSpec((RET_HEADS, c, LANES), const3),
            pl.BlockSpec((RET_HEADS, LANES, LANES), const3),
        ],
        out_specs=pl.BlockSpec((None, c, vw), lambda b, i: (b, i, 0)),
        out_shape=jax.ShapeDtypeStruct((B, S, vw), BF16),
        scratch_shapes=[pltpu.VMEM((RET_HEADS, LANES, LANES), F32)],
        compiler_params=_params(("parallel", "arbitrary")),
        name="ret",
    )(proj3, proj3, proj3, proj3, cos_t, sin_t, decay, q_dec, k_dec, c_dec)


def _memkv_kernel(m_ref, g_ref, w_ref, o_ref):
    h = _rms(m_ref[...], g_ref[...]).astype(BF16)
    o_ref[...] = jnp.dot(h, w_ref[...], preferred_element_type=F32).astype(o_ref.dtype)


def _memkv(mem, g, w_bf16):
    B, M, _ = mem.shape
    return pl.pallas_call(
        _memkv_kernel,
        grid=(B,),
        in_specs=[
            pl.BlockSpec((None, M, D_MODEL), lambda b: (b, 0, 0)),
            pl.BlockSpec((1, D_MODEL), lambda b: (0, 0)),
            pl.BlockSpec((D_MODEL, 2 * D_MODEL), lambda b: (0, 0)),
        ],
        out_specs=pl.BlockSpec((None, M, 2 * D_MODEL), lambda b: (b, 0, 0)),
        out_shape=jax.ShapeDtypeStruct((B, M, 2 * D_MODEL), BF16),
        compiler_params=_params(("parallel",)),
        name="memkv",
    )(mem, g, w_bf16)


MIX_CHUNKS = 2


def _lane_pack(cols, rows, dtype):
    lane = lax.broadcasted_iota(jnp.int32, (rows, LANES), 1)
    out = jnp.zeros((rows, LANES), dtype)
    for k, c in enumerate(cols):
        out = jnp.where(lane == k, c.astype(dtype), out)
    return out


def _mix_kernel(x_ref, ysb_ref, yret_ref, gsb_ref, gret_ref, kv_ref,
                wsbo_ref, wreto_ref, wmix_ref, wcq_ref, wco_ref,
                gx_ref, gm_ref, wr_ref, br_ref,
                x2_ref, h3_ref, idx_ref, gate_ref, rank_ref, cnt_ref, cnt_scr):
    tm = x_ref.shape[0]
    cr = tm // MIX_CHUNKS
    chunks = [slice(c * cr, (c + 1) * cr) for c in range(MIX_CHUNKS)]
    dot = functools.partial(jnp.dot, preferred_element_type=F32)

    @pl.when(jnp.logical_and(pl.program_id(0) == 0, pl.program_id(1) == 0))
    def _():
        cnt_scr[...] = jnp.zeros_like(cnt_scr)

    def sigmoid(g):
        return (0.5 * jnp.tanh(0.5 * g) + 0.5).astype(F32)

    a = [dot(ysb_ref[s, :], wsbo_ref[...]) for s in chunks]
    b = [dot(yret_ref[s, :], wreto_ref[...]) for s in chunks]
    merged = [sigmoid(gsb_ref[s, :]) * ac + sigmoid(gret_ref[s, :]) * bc
              for s, ac, bc in zip(chunks, a, b)]
    x1 = [x_ref[s, :] + dot(m.astype(BF16), wmix_ref[...]) for s, m in zip(chunks, merged)]

    h2 = [_rms(v, gx_ref[...]).astype(BF16) for v in x1]
    q = [(dot(v, wcq_ref[...]) * X_HEAD_DIM ** -0.5).astype(BF16) for v in h2]
    heads = [[] for _ in chunks]
    for hh in range(X_HEADS):
        lo = hh * X_HEAD_DIM
        kh = kv_ref[:, lo:lo + X_HEAD_DIM]
        vh = kv_ref[:, D_MODEL + lo:D_MODEL + lo + X_HEAD_DIM]
        s = [lax.dot_general(qc[:, lo:lo + X_HEAD_DIM], kh, (((1,), (1,)), ((), ())),
                             preferred_element_type=F32) for qc in q]
        p = [jnp.exp(sc - jnp.max(sc, axis=-1, keepdims=True)) for sc in s]
        p = [pc * (1.0 / jnp.sum(pc, axis=-1, keepdims=True)) for pc in p]
        for c, pc in enumerate(p):
            heads[c].append(dot(pc.astype(BF16), vh).astype(BF16))
    o = [jnp.concatenate(hc, axis=1) for hc in heads]
    x2 = [v + dot(oc, wco_ref[...]) for v, oc in zip(x1, o)]
    for s, v in zip(chunks, x2):
        x2_ref[s, :] = v

    h3 = [_rms(v, gm_ref[...]) for v in x2]
    for s, v in zip(chunks, h3):
        h3_ref[s, :] = v
    logits = []
    for v in h3:
        hi = v.astype(BF16)
        lo = (v - hi.astype(F32)).astype(BF16)
        r = dot(jnp.concatenate([hi, lo], axis=0), wr_ref[...])
        logits.append(r[:cr, :N_EXPERTS] + r[:cr, N_EXPERTS:] + r[cr:, :N_EXPERTS] + br_ref[...])

    e_iota = lax.broadcasted_iota(jnp.int32, (cr, N_EXPERTS), 1).astype(F32)
    r_i = lax.broadcasted_iota(jnp.int32, (cr, cr), 0)
    c_i = lax.broadcasted_iota(jnp.int32, (cr, cr), 1)
    before = jnp.where(c_i < r_i, 1.0, 0.0).astype(BF16)
    count = cnt_scr[...]
    for s, rem in zip(chunks, logits):
        vals, idxs = [], []
        for _ in range(TOP_K):
            m = jnp.max(rem, axis=-1, keepdims=True)
            ik = jnp.min(jnp.where(rem == m, e_iota, float(N_EXPERTS)), axis=-1, keepdims=True)
            vals.append(m)
            idxs.append(ik)
            rem = jnp.where(e_iota == ik, -jnp.inf, rem)
        ex = [jnp.exp(v - vals[0]) for v in vals]
        inv_den = 1.0 / (ex[0] + ex[1] + ex[2] + ex[3])
        gates = [e * inv_den for e in ex]

        chosen = [(e_iota == ik) for ik in idxs]
        member = jnp.zeros((cr, N_EXPERTS), F32)
        for ch in chosen:
            member = member + jnp.where(ch, 1.0, 0.0)
        prefix = dot(before, member.astype(BF16)) + count
        ranks = [jnp.sum(jnp.where(ch, prefix, 0.0), axis=-1, keepdims=True) for ch in chosen]
        count = count + jnp.sum(member, axis=0, keepdims=True)

        idx_ref[s, :] = _lane_pack(idxs, cr, jnp.int32)
        rank_ref[s, :] = _lane_pack(ranks, cr, jnp.int32)
        gate_ref[s, :] = _lane_pack(gates, cr, F32)
    cnt_scr[...] = count
    cnt_ref[...] = count


def _mix(x, proj3, y_sb, y_ret, kv, wsbo, wreto, wmix, wcq, wco, gx, gm, wr, br):
    B, S, _ = x.shape
    tm = min(512, S)
    M = kv.shape[1]
    gsb_b, gret_b = OFF_G_SB // D_MODEL, OFF_G_RET // D_MODEL
    tok = lambda b, i: (b, i, 0)
    const2 = lambda b, i: (0, 0)
    row_out = lambda w, dt: jax.ShapeDtypeStruct((B, S, w), dt)
    outs = pl.pallas_call(
        _mix_kernel,
        grid=(B, S // tm),
        in_specs=[
            pl.BlockSpec((None, tm, D_MODEL), tok),
            pl.BlockSpec((None, tm, SB_HEADS * SB_HEAD_DIM), tok),
            pl.BlockSpec((None, tm, D_MODEL), tok),
            pl.BlockSpec((None, tm, D_MODEL), lambda b, i: (b, i, gsb_b)),
            pl.BlockSpec((None, tm, D_MODEL), lambda b, i: (b, i, gret_b)),
            pl.BlockSpec((None, M, 2 * D_MODEL), lambda b, i: (b, 0, 0)),
            pl.BlockSpec((SB_HEADS * SB_HEAD_DIM, D_MODEL), const2),
            pl.BlockSpec((D_MODEL, D_MODEL), const2),
            pl.BlockSpec((D_MODEL, D_MODEL), const2),
            pl.BlockSpec((D_MODEL, D_MODEL), const2),
            pl.BlockSpec((D_MODEL, D_MODEL), const2),
            pl.BlockSpec((1, D_MODEL), const2),
            pl.BlockSpec((1, D_MODEL), const2),
            pl.BlockSpec((D_MODEL, 2 * N_EXPERTS), const2),
            pl.BlockSpec((1, N_EXPERTS), const2),
        ],
        out_specs=[
            pl.BlockSpec((None, tm, D_MODEL), tok),
            pl.BlockSpec((None, tm, D_MODEL), tok),
            pl.BlockSpec((None, tm, LANES), tok),
            pl.BlockSpec((None, tm, LANES), tok),
            pl.BlockSpec((None, tm, LANES), tok),
            pl.BlockSpec((1, N_EXPERTS), const2),
        ],
        out_shape=[
            row_out(D_MODEL, F32), row_out(D_MODEL, F32),
            row_out(LANES, jnp.int32), row_out(LANES, F32), row_out(LANES, jnp.int32),
            jax.ShapeDtypeStruct((1, N_EXPERTS), F32),
        ],
        scratch_shapes=[pltpu.VMEM((1, N_EXPERTS), F32)],
        compiler_params=_params(("arbitrary", "arbitrary")),
        name="mix",
    )(x, y_sb, y_ret, proj3, proj3, kv, wsbo, wreto, wmix, wcq, wco, gx, gm, wr, br)
    return outs


def _row_copy(src, s, dst, d, sem):
    return pltpu.make_async_copy(src.at[pl.ds(s, 1)], dst.at[pl.ds(d, 1)], sem)


DMA_UNROLL = 8


ZERO_ROWS = 256
SUBLANES = 8


def _dispatch_kernel(lo_ref, hi_ref, pos_ref, h_ref, xin_ref, zbuf, sem, zsem):
    tm = h_ref.shape[0]

    def zero_fill(act):
        def per_expert(e, carry):
            lo = lo_ref[e]
            hi = hi_ref[e]
            head_end = jnp.minimum((lo + SUBLANES - 1) // SUBLANES * SUBLANES, hi)

            def single(r, c):
                act(_row_copy(zbuf, 0, xin_ref, r, zsem))
                return c

            lax.fori_loop(lo, head_end, single, 0)

            def chunk(i, off):
                act(pltpu.make_async_copy(zbuf, xin_ref.at[pl.ds(pl.multiple_of(off, SUBLANES), ZERO_ROWS)], zsem))
                return off + ZERO_ROWS

            off = lax.fori_loop(0, (hi - head_end) // ZERO_ROWS, chunk, head_end)
            rest = hi - off
            size = ZERO_ROWS // 2
            while size >= SUBLANES:
                @pl.when((rest & size) != 0)
                def _(off=off, size=size):
                    act(pltpu.make_async_copy(
                        zbuf.at[pl.ds(0, size)],
                        xin_ref.at[pl.ds(pl.multiple_of(off, SUBLANES), size)], zsem))
                off = off + (rest & size)
                size //= 2
            return carry

        lax.fori_loop(0, N_EXPERTS, per_expert, 0)

    @pl.when(pl.program_id(0) == 0)
    def _():
        zbuf[...] = jnp.zeros_like(zbuf)
        zero_fill(lambda cp: cp.start())
        zero_fill(lambda cp: cp.wait())

    def issue(t, carry):
        for k in range(TOP_K):
            _row_copy(h_ref, t, xin_ref, pos_ref[0, 0, t * TOP_K + k], sem).start(priority=k % 2)
        return carry

    lax.fori_loop(0, tm, issue, 0, unroll=DMA_UNROLL)
    for _ in range(TOP_K):
        pltpu.make_async_copy(h_ref, xin_ref.at[pl.ds(0, tm)], sem).wait()


def _dispatch(h3, pos, pad_lo, pad_hi, n_rows):
    T = h3.shape[0]
    tm = min(256, T)
    pos3 = pos.reshape(T // tm, 1, tm * TOP_K)
    grid_spec = pltpu.PrefetchScalarGridSpec(
        num_scalar_prefetch=2,
        grid=(T // tm,),
        in_specs=[
            pl.BlockSpec((1, 1, tm * TOP_K), lambda i, lo, hi: (i, 0, 0), memory_space=pltpu.SMEM),
            pl.BlockSpec((tm, D_MODEL), lambda i, lo, hi: (i, 0)),
        ],
        out_specs=pl.BlockSpec(memory_space=pl.ANY),
        scratch_shapes=[pltpu.VMEM((ZERO_ROWS, D_MODEL), F32), pltpu.SemaphoreType.DMA(()),
                        pltpu.SemaphoreType.DMA(())],
    )
    return pl.pallas_call(
        _dispatch_kernel,
        grid_spec=grid_spec,
        out_shape=jax.ShapeDtypeStruct((n_rows, D_MODEL), F32),
        compiler_params=_params(("arbitrary",)),
        name="dispatch",
    )(pad_lo, pad_hi, pos3, h3)


def _ffn_kernel(first_ref, count_ref, nreal_ref, x_hbm, w1_ref, b1_ref, w2_ref, b2_ref, y_hbm,
                xbuf, ybuf, w1b, w2b, in_sem, out_sem):
    e = pl.program_id(0)
    tb = FFN_BLOCK
    n_real = nreal_ref[0]

    def rows(g):
        return pl.ds(pl.multiple_of(g * tb, tb), tb)

    def x_copy(g, slot):
        return pltpu.make_async_copy(x_hbm.at[rows(g)], xbuf.at[slot], in_sem.at[slot])

    def y_copy(g, slot):
        return pltpu.make_async_copy(ybuf.at[slot], y_hbm.at[rows(g)], out_sem.at[slot])

    @pl.when(jnp.logical_and(e == 0, n_real > 0))
    def _():
        x_copy(0, 0).start()

    @pl.when(count_ref[e] > 0)
    def _():
        w1b[...] = w1_ref[...].astype(BF16)
        w2b[...] = w2_ref[...].astype(BF16)

    def block(j, carry):
        g = first_ref[e] + j
        slot = g % 2
        x_copy(g, slot).wait()

        @pl.when(g + 1 < n_real)
        def _():
            x_copy(g + 1, 1 - slot).start()

        @pl.when(g >= 2)
        def _():
            y_copy(g - 2, slot).wait()

        hc = jnp.dot(xbuf[slot].astype(BF16), w1b[...], preferred_element_type=F32) + b1_ref[...]
        glu = jnp.minimum(hc[:, :D_FF], SWIGLU_LIMIT)
        lin = jnp.clip(hc[:, D_FF:], -SWIGLU_LIMIT, SWIGLU_LIMIT)
        act = glu * jax.nn.sigmoid(SWIGLU_ALPHA * glu) * (lin + 1.0)
        ybuf[slot] = jnp.dot(act.astype(BF16), w2b[...], preferred_element_type=F32) + b2_ref[...]
        y_copy(g, slot).start()
        return carry

    lax.fori_loop(0, count_ref[e], block, 0)

    @pl.when(e == N_EXPERTS - 1)
    def _():
        for back in (2, 1):
            @pl.when(n_real >= back)
            def _(back=back):
                y_copy(n_real - back, (n_real - back) % 2).wait()

        ybuf[0] = jnp.zeros((tb, D_MODEL), F32)

        def tail(g, carry):
            cp = y_copy(g, 0)
            cp.start()
            cp.wait()
            return carry

        lax.fori_loop(n_real, y_hbm.shape[0] // tb, tail, 0)


def _ffn(xin, first_block, block_count, n_real, w1, b1, w2, b2):
    P = xin.shape[0]
    tb = FFN_BLOCK
    expert = lambda e, first, count, nr: (e, 0, 0)
    grid_spec = pltpu.PrefetchScalarGridSpec(
        num_scalar_prefetch=3,
        grid=(N_EXPERTS,),
        in_specs=[
            pl.BlockSpec(memory_space=pl.ANY),
            pl.BlockSpec((None, D_MODEL, 2 * D_FF), expert),
            pl.BlockSpec((None, 1, 2 * D_FF), expert),
            pl.BlockSpec((None, D_FF, D_MODEL), expert),
            pl.BlockSpec((None, 1, D_MODEL), expert),
        ],
        out_specs=pl.BlockSpec(memory_space=pl.ANY),
        scratch_shapes=[pltpu.VMEM((2, tb, D_MODEL), F32), pltpu.VMEM((2, tb, D_MODEL), F32),
                        pltpu.VMEM((D_MODEL, 2 * D_FF), BF16), pltpu.VMEM((D_FF, D_MODEL), BF16),
                        pltpu.SemaphoreType.DMA((2,)), pltpu.SemaphoreType.DMA((2,))],
    )
    return pl.pallas_call(
        _ffn_kernel,
        grid_spec=grid_spec,
        out_shape=jax.ShapeDtypeStruct((P, D_MODEL), F32),
        compiler_params=_params(("arbitrary",)),
        name="ffn",
    )(first_block, block_count, n_real, xin, w1, b1.reshape(N_EXPERTS, 1, 2 * D_FF), w2,
      b2.reshape(N_EXPERTS, 1, D_MODEL))


def _combine_kernel(pos_ref, nxt_ref, x2_ref, gate_ref, g_ref, y_ref, o_ref, buf, sem):
    tm = x2_ref.shape[0]
    i = pl.program_id(0)
    slot = i % 2

    def gather(p_ref, s):
        def issue(t, carry):
            for k in range(TOP_K):
                _row_copy(y_ref, p_ref[0, 0, t * TOP_K + k], buf.at[s, k], t,
                          sem.at[s]).start(priority=k % 2)
            return carry
        lax.fori_loop(0, tm, issue, 0, unroll=DMA_UNROLL)

    @pl.when(i == 0)
    def _():
        gather(pos_ref, slot)

    @pl.when(i + 1 < pl.num_programs(0))
    def _():
        gather(nxt_ref, 1 - slot)

    for k in range(TOP_K):
        pltpu.make_async_copy(y_ref.at[pl.ds(0, tm)], buf.at[slot, k], sem.at[slot]).wait()

    acc = x2_ref[...]
    gate = gate_ref[...]
    for k in range(TOP_K):
        acc = acc + gate[:, k:k + 1] * buf[slot, k]
    o_ref[...] = _rms(acc, g_ref[...])


def _combine(x2, gate, pos, yrows, g_final):
    T = x2.shape[0]
    tm = min(256, T)
    n = T // tm
    pos3 = pos.reshape(n, 1, tm * TOP_K)
    return pl.pallas_call(
        _combine_kernel,
        grid=(n,),
        in_specs=[
            pl.BlockSpec((1, 1, tm * TOP_K), lambda i: (i, 0, 0), memory_space=pltpu.SMEM),
            pl.BlockSpec((1, 1, tm * TOP_K), lambda i: (jnp.minimum(i + 1, n - 1), 0, 0),
                         memory_space=pltpu.SMEM),
            pl.BlockSpec((tm, D_MODEL), lambda i: (i, 0)),
            pl.BlockSpec((tm, LANES), lambda i: (i, 0)),
            pl.BlockSpec((1, D_MODEL), lambda i: (0, 0)),
            pl.BlockSpec(memory_space=pl.ANY),
        ],
        out_specs=pl.BlockSpec((tm, D_MODEL), lambda i: (i, 0)),
        out_shape=jax.ShapeDtypeStruct((T, D_MODEL), F32),
        scratch_shapes=[pltpu.VMEM((2, TOP_K, tm, D_MODEL), F32), pltpu.SemaphoreType.DMA((2,))],
        compiler_params=_params(("arbitrary",)),
        name="combine",
    )(pos3, pos3, x2, gate, g_final, yrows)


def kernel(x, mem, g_mix, w_in, w_sb_o, w_ret_o, w_mix_out, g_xattn, g_mem, w_cq, w_ckv, w_co,
           g_moe, w_router, b_router, w_exp_in, b_exp_in, w_exp_out, b_exp_out, g_final):
    B, S, _ = x.shape
    T = B * S
    assert w_in.shape[0] == 1, "one layer"
    row = lambda v: v.reshape(1, -1)

    w_in_k = jnp.concatenate([w_in[0][:, o:o + w] for _, o, w in _IN_GROUPS], axis=1).astype(BF16)
    proj = _inproj(x.reshape(T, D_MODEL), row(g_mix[0]), w_in_k)
    proj3 = proj.reshape(B, S, IN_W)
    y_sb = _sb_attention(proj3)
    y_ret = _retention(proj3)
    kv = _memkv(mem, row(g_mem[0]), w_ckv[0].astype(BF16))
    wr_hi = w_router[0].astype(BF16)
    wr_lo = (w_router[0] - wr_hi.astype(F32)).astype(BF16)
    x2, h3, idx, gate, rank, cnt = _mix(
        x, proj3, y_sb, y_ret, kv,
        w_sb_o[0].astype(BF16), w_ret_o[0].astype(BF16), w_mix_out[0].astype(BF16),
        w_cq[0].astype(BF16), w_co[0].astype(BF16),
        row(g_xattn[0]), row(g_moe[0]), jnp.concatenate([wr_hi, wr_lo], axis=1), row(b_router[0]))

    tb = FFN_BLOCK
    n_blocks = (T * TOP_K + N_EXPERTS * (tb - 1) + tb - 1) // tb
    counts = cnt[0].astype(jnp.int32)
    padded = ((counts + tb - 1) // tb) * tb
    pad_end = jnp.cumsum(padded)
    pad_start = pad_end - padded
    idx4 = idx.reshape(T, LANES)[:, :TOP_K]
    start4 = jnp.sum(jnp.where(idx4[:, :, None] == jnp.arange(N_EXPERTS), pad_start, 0), axis=-1)
    pos = (start4 + rank.reshape(T, LANES)[:, :TOP_K]).astype(jnp.int32).reshape(T * TOP_K)
    n_real = (pad_end[-1:] // tb).astype(jnp.int32)
    first_block = (pad_start // tb).astype(jnp.int32)
    block_count = (padded // tb).astype(jnp.int32)

    pad_lo = (pad_start + counts).astype(jnp.int32)
    pad_hi = jnp.where(jnp.arange(N_EXPERTS) == N_EXPERTS - 1, n_blocks * tb, pad_end).astype(jnp.int32)
    xin = _dispatch(h3.reshape(T, D_MODEL), pos, pad_lo, pad_hi, n_blocks * tb)
    yrows = _ffn(xin, first_block, block_count, n_real,
                 w_exp_in[0], b_exp_in[0], w_exp_out[0], b_exp_out[0])
    out = _combine(x2.reshape(T, D_MODEL), gate.reshape(T, LANES), pos, yrows, row(g_final))
    return out.reshape(B, S, D_MODEL)
```

```python
import functools
import math

import numpy as np
import jax
import jax.numpy as jnp
from jax import lax
from jax.experimental import pallas as pl
from jax.experimental.pallas import tpu as pltpu

F32 = jnp.float32
BF16 = jnp.bfloat16

D_MODEL = 1024
SB_HEADS = 8
SB_HEAD_DIM = 64
RET_HEADS = 8
RET_QK_DIM = 64
RET_V_DIM = 128
ROPE_BASE = 10000.0
X_HEADS = 4
X_HEAD_DIM = 256
N_EXPERTS = 32
TOP_K = 4
D_FF = 1024
SWIGLU_LIMIT = 7.0
SWIGLU_ALPHA = 1.702
NORM_EPS = 1e-5

_IN_GROUPS = (("r_v", 2560, 1024), ("r_g", 3584, 1024), ("g_sb", 4608, 1024), ("g_ret", 5632, 1024),
              ("sb_q", 0, 512), ("sb_k", 512, 512), ("sb_v", 1024, 512),
              ("r_q", 1536, 512), ("r_k", 2048, 512))
OFF_R_V, OFF_R_G, OFF_G_SB, OFF_G_RET = 0, 1024, 2048, 3072
OFF_SB_Q, OFF_SB_K, OFF_SB_V, OFF_R_Q, OFF_R_K = 4096, 4608, 5120, 5632, 6144
IN_W = 6656

LANES = 128
VMEM_LIMIT = 56 * 1024 * 1024

SB_TILE = 128
SB_LOG_UNDERFLOW = -88.0
RET_CHUNK = 256
FFN_BLOCK = 256


def _rms(x, g):
    return x * lax.rsqrt(jnp.mean(x * x, axis=-1, keepdims=True) + NORM_EPS) * g


def _params(sem):
    return pltpu.CompilerParams(dimension_semantics=sem, vmem_limit_bytes=VMEM_LIMIT)


INPROJ_COLS = 512


def _inproj_kernel(x_ref, g_ref, w_ref, o_ref):
    h = _rms(x_ref[...], g_ref[...]).astype(BF16)
    for j in range(IN_W // INPROJ_COLS):
        cols = slice(j * INPROJ_COLS, (j + 1) * INPROJ_COLS)
        o_ref[:, cols] = jnp.dot(h, w_ref[:, cols], preferred_element_type=F32).astype(o_ref.dtype)


def _inproj(x2d, g, w_bf16):
    T = x2d.shape[0]
    tm = min(512, T)
    return pl.pallas_call(
        _inproj_kernel,
        grid=(T // tm,),
        in_specs=[
            pl.BlockSpec((tm, D_MODEL), lambda i: (i, 0)),
            pl.BlockSpec((1, D_MODEL), lambda i: (0, 0)),
            pl.BlockSpec((D_MODEL, IN_W), lambda i: (0, 0)),
        ],
        out_specs=pl.BlockSpec((tm, IN_W), lambda i: (i, 0)),
        out_shape=jax.ShapeDtypeStruct((T, IN_W), BF16),
        compiler_params=_params(("parallel",)),
        name="inproj",
    )(x2d, g, w_bf16)


SB_PAIRS = SB_HEADS * SB_HEAD_DIM // LANES


def _sb_kernel(q_ref, k_ref, v_ref, o_ref, q2_ref, *state_refs):
    carry_refs, acc_refs = state_refs[:SB_PAIRS], state_refs[SB_PAIRS:]
    n = SB_TILE
    qi = pl.program_id(1)
    lane = lax.broadcasted_iota(jnp.int32, (n, LANES), 1)
    first = lane < SB_HEAD_DIM
    row = lax.broadcasted_iota(jnp.int32, (2 * n, n), 0) & (n - 1)
    col = lax.broadcasted_iota(jnp.int32, (2 * n, n), 1)
    strict = col < row
    jj = lax.broadcasted_iota(jnp.int32, (2 * n, 2 * n), 0) & (n - 1)
    cc = lax.broadcasted_iota(jnp.int32, (2 * n, 2 * n), 1)
    suffix = jnp.where((cc >= n) | (jj >= cc), 1.0, 0.0).astype(BF16)
    scale = SB_HEAD_DIM ** -0.5

    for p in range(SB_PAIRS):
        qs = (q_ref[:, p * LANES:(p + 1) * LANES].astype(F32) * scale).astype(BF16)
        zeros = jnp.zeros_like(qs)
        q2_ref[p * 2 * n:p * 2 * n + n, :] = jnp.where(first, qs, zeros)
        q2_ref[p * 2 * n + n:(p + 1) * 2 * n, :] = jnp.where(first, zeros, qs)

    def step(kt, diagonal):
        ks = pl.multiple_of(kt * n, n)
        zs, hls, css, ws, cmaxs = {}, {}, {}, {}, []

        def logits(p):
            kk = k_ref[pl.ds(ks, n), p * LANES:(p + 1) * LANES]
            z = lax.dot_general(q2_ref[p * 2 * n:(p + 1) * 2 * n, :], kk, (((1,), (1,)), ((), ())),
                                preferred_element_type=F32)
            sp = jnp.maximum(z, 0.0) + jnp.log(1.0 + jnp.exp(-jnp.abs(z)))
            if diagonal:
                sp = jnp.where(strict, sp, 0.0)
            hi = sp.astype(BF16)
            lo = (sp - hi.astype(F32)).astype(BF16)
            zs[p] = z
            hls[p] = jnp.concatenate([hi, lo], axis=1)

        def sums(p):
            css[p] = jnp.dot(hls[p], suffix, preferred_element_type=F32)

        def weights(p):
            logw = zs[p] - css[p][:, :n]
            if not diagonal:
                logw = logw + carry_refs[p][...]
            w = jnp.exp(logw)
            if diagonal:
                w = jnp.where(strict, w, 0.0)
            ws[p] = w.astype(BF16)

        def values(p):
            vv = v_ref[pl.ds(ks, n), p * LANES:(p + 1) * LANES]
            pv = jnp.dot(ws[p], vv, preferred_element_type=F32)
            acc_refs[p][...] = pv if diagonal else acc_refs[p][...] + pv
            carry = -css[p][:, n:] if diagonal else carry_refs[p][...] - css[p][:, n:]
            carry_refs[p][...] = carry
            cmaxs.append(jnp.max(carry))

        for stage in (logits, sums, weights, values):
            for p in range(SB_PAIRS):
                stage(p)
        return functools.reduce(jnp.maximum, cmaxs)

    def cond(st):
        kt, cmax = st
        return jnp.logical_and(kt >= 0, cmax > SB_LOG_UNDERFLOW)

    def body(st):
        kt, _ = st
        return kt - 1, step(kt, False)

    lax.while_loop(cond, body, (qi - 1, step(qi, True)))

    for p in range(SB_PAIRS):
        o_ref[:, p * LANES:(p + 1) * LANES] = jnp.where(
            first, acc_refs[p][:n, :], acc_refs[p][n:, :]
        ).astype(o_ref.dtype)


def _sb_attention(proj3):
    B, S, _ = proj3.shape
    n = SB_TILE
    w = SB_HEADS * SB_HEAD_DIM
    qb, kb, vb = OFF_SB_Q // w, OFF_SB_K // w, OFF_SB_V // w
    rows = SB_PAIRS * 2 * n
    return pl.pallas_call(
        _sb_kernel,
        grid=(B, S // n),
        in_specs=[
            pl.BlockSpec((None, n, w), lambda b, i: (b, i, qb)),
            pl.BlockSpec((None, S, w), lambda b, i: (b, 0, kb)),
            pl.BlockSpec((None, S, w), lambda b, i: (b, 0, vb)),
        ],
        out_specs=pl.BlockSpec((None, n, w), lambda b, i: (b, i, 0)),
        out_shape=jax.ShapeDtypeStruct((B, S, w), BF16),
        scratch_shapes=([pltpu.VMEM((rows, LANES), BF16)]
                        + [pltpu.VMEM((2 * n, LANES), F32)] * (2 * SB_PAIRS)),
        compiler_params=_params(("parallel", "arbitrary")),
        name="sb",
    )(proj3, proj3, proj3)


def _ret_tables(S):
    c = RET_CHUNK
    f = np.float32
    inv = f(ROPE_BASE) ** (-np.arange(0, RET_QK_DIM, 2, dtype=f) / f(RET_QK_DIM))
    ang = np.arange(S, dtype=f)[:, None] * inv[None, :]
    cos, sin = np.cos(ang), np.sin(ang)
    reps = LANES // RET_QK_DIM
    cos_t = np.tile(np.concatenate([cos, cos], axis=1), (1, reps))
    sin_t = np.tile(np.concatenate([-sin, sin], axis=1), (1, reps))
    log_g = np.log1p(-np.exp2(-5.0 - np.arange(RET_HEADS, dtype=np.float64)))
    pos = np.arange(c, dtype=np.float64)
    diff = pos[:, None] - pos[None, :]
    decay = np.where(diff >= 0, np.exp(log_g[:, None, None] * np.maximum(diff, 0.0)), 0.0).astype(f)
    q_dec = np.exp(log_g[:, None] * (pos + 1.0)[None, :]).astype(f)
    k_dec = np.exp(log_g[:, None] * (c - 1 - pos)[None, :]).astype(f)
    q_dec = np.ascontiguousarray(np.broadcast_to(q_dec[:, :, None], (RET_HEADS, c, LANES)))
    k_dec = np.ascontiguousarray(np.broadcast_to(k_dec[:, :, None], (RET_HEADS, c, LANES)))
    c_dec = np.ascontiguousarray(np.broadcast_to(
        np.exp(log_g * c).astype(f)[:, None, None], (RET_HEADS, LANES, LANES)))
    return cos_t, sin_t, decay, q_dec, k_dec, c_dec


def _ret_kernel(q_ref, k_ref, v_ref, g_ref, cos_ref, sin_ref, dm_ref, qd_ref, kd_ref, cd_ref,
                o_ref, state):
    c = RET_CHUNK

    @pl.when(pl.program_id(1) == 0)
    def _():
        state[...] = jnp.zeros_like(state)

    lane = lax.broadcasted_iota(jnp.int32, (c, LANES), 1)
    low_half = (lane & (RET_QK_DIM - 1)) < RET_QK_DIM // 2
    cos = cos_ref[...]
    sin = sin_ref[...]

    def rotary(x):
        swapped = jnp.where(low_half, pltpu.roll(x, LANES - RET_QK_DIM // 2, 1),
                            pltpu.roll(x, RET_QK_DIM // 2, 1))
        return x * cos + swapped * sin

    heads = range(RET_HEADS)
    cols = [slice(h * RET_V_DIM, (h + 1) * RET_V_DIM) for h in heads]
    rq, rk, qh, sc, out = {}, {}, {}, {}, {}

    for p in range(RET_HEADS * RET_QK_DIM // LANES):
        rq[p] = rotary(q_ref[:, p * LANES:(p + 1) * LANES].astype(F32))
        rk[p] = rotary(k_ref[:, p * LANES:(p + 1) * LANES].astype(F32)) * RET_QK_DIM ** -0.5
    for h in heads:
        in_head = (lane < RET_QK_DIM) if h % 2 == 0 else (lane >= RET_QK_DIM)
        qh[h] = jnp.where(in_head, rq[h // 2], 0.0)
        sc[h] = lax.dot_general(qh[h].astype(BF16), rk[h // 2].astype(BF16), (((1,), (1,)), ((), ())),
                                preferred_element_type=F32) * dm_ref[h]
    for h in heads:
        vh = v_ref[:, cols[h]]
        o = jnp.dot(sc[h].astype(BF16), vh, preferred_element_type=F32)
        out[h] = o + jnp.dot((qh[h] * qd_ref[h]).astype(BF16), state[h].astype(BF16),
                             preferred_element_type=F32)
        kv_new = lax.dot_general((rk[h // 2] * kd_ref[h]).astype(BF16), vh, (((0,), (0,)), ((), ())),
                                 preferred_element_type=F32)
        state[h] = state[h] * cd_ref[h] + kv_new
    for h in heads:
        o = out[h]
        mu = jnp.mean(o, axis=-1, keepdims=True)
        d = o - mu
        y = d * lax.rsqrt(jnp.mean(d * d, axis=-1, keepdims=True) + NORM_EPS)
        g = g_ref[:, cols[h]].astype(F32)
        o_ref[:, cols[h]] = (g * jax.nn.sigmoid(g) * y).astype(o_ref.dtype)


def _retention(proj3):
    B, S, _ = proj3.shape
    c = RET_CHUNK
    qkw, vw = RET_HEADS * RET_QK_DIM, RET_HEADS * RET_V_DIM
    qb, kb, vb, gb = OFF_R_Q // qkw, OFF_R_K // qkw, OFF_R_V // vw, OFF_R_G // vw
    cos_t, sin_t, decay, q_dec, k_dec, c_dec = _ret_tables(S)
    const3 = lambda b, i: (0, 0, 0)
    return pl.pallas_call(
        _ret_kernel,
        grid=(B, S // c),
        in_specs=[
            pl.BlockSpec((None, c, qkw), lambda b, i: (b, i, qb)),
            pl.BlockSpec((None, c, qkw), lambda b, i: (b, i, kb)),
            pl.BlockSpec((None, c, vw), lambda b, i: (b, i, vb)),
            pl.BlockSpec((None, c, vw), lambda b, i: (b, i, gb)),
            pl.BlockSpec((c, LANES), lambda b, i: (i, 0)),
            pl.BlockSpec((c, LANES), lambda b, i: (i, 0)),
            pl.BlockSpec((RET_HEADS, c, c), const3),
            pl.BlockSpec((RET_HEADS, c, LANES), const3),
            pl.BlockSpec((RET_HEADS, c, LANES), const3),
            pl.BlockSpec((RET_HEADS, LANES, LANES), const3),
        ],
        out_specs=pl.BlockSpec((None, c, vw), lambda b, i: (b, i, 0)),
        out_shape=jax.ShapeDtypeStruct((B, S, vw), BF16),
        scratch_shapes=[pltpu.VMEM((RET_HEADS, LANES, LANES), F32)],
        compiler_params=_params(("parallel", "arbitrary")),
        name="ret",
    )(proj3, proj3, proj3, proj3, cos_t, sin_t, decay, q_dec, k_dec, c_dec)


def _memkv_kernel(m_ref, g_ref, w_ref, o_ref):
    h = _rms(m_ref[...], g_ref[...]).astype(BF16)
    o_ref[...] = jnp.dot(h, w_ref[...], preferred_element_type=F32).astype(o_ref.dtype)


def _memkv(mem, g, w_bf16):
    B, M, _ = mem.shape
    return pl.pallas_call(
        _memkv_kernel,
        grid=(B,),
        in_specs=[
            pl.BlockSpec((None, M, D_MODEL), lambda b: (b, 0, 0)),
            pl.BlockSpec((1, D_MODEL), lambda b: (0, 0)),
            pl.BlockSpec((D_MODEL, 2 * D_MODEL), lambda b: (0, 0)),
        ],
        out_specs=pl.BlockSpec((None, M, 2 * D_MODEL), lambda b: (b, 0, 0)),
        out_shape=jax.ShapeDtypeStruct((B, M, 2 * D_MODEL), BF16),
        compiler_params=_params(("parallel",)),
        name="memkv",
    )(mem, g, w_bf16)


MIX_CHUNKS = 2


def _lane_pack(cols, rows, dtype):
    lane = lax.broadcasted_iota(jnp.int32, (rows, LANES), 1)
    out = jnp.zeros((rows, LANES), dtype)
    for k, c in enumerate(cols):
        out = jnp.where(lane == k, c.astype(dtype), out)
    return out


def _mix_kernel(x_ref, ysb_ref, yret_ref, gsb_ref, gret_ref, kv_ref,
                wsbo_ref, wreto_ref, wmix_ref, wcq_ref, wco_ref,
                gx_ref, gm_ref, wr_ref, br_ref,
                x2_ref, h3_ref, idx_ref, gate_ref, rank_ref, cnt_ref, cnt_scr):
    tm = x_ref.shape[0]
    cr = tm // MIX_CHUNKS
    chunks = [slice(c * cr, (c + 1) * cr) for c in range(MIX_CHUNKS)]
    dot = functools.partial(jnp.dot, preferred_element_type=F32)

    @pl.when(jnp.logical_and(pl.program_id(0) == 0, pl.program_id(1) == 0))
    def _():
        cnt_scr[...] = jnp.zeros_like(cnt_scr)

    def sigmoid(g):
        return (0.5 * jnp.tanh(0.5 * g) + 0.5).astype(F32)

    a = [dot(ysb_ref[s, :], wsbo_ref[...]) for s in chunks]
    b = [dot(yret_ref[s, :], wreto_ref[...]) for s in chunks]
    merged = [sigmoid(gsb_ref[s, :]) * ac + sigmoid(gret_ref[s, :]) * bc
              for s, ac, bc in zip(chunks, a, b)]
    x1 = [x_ref[s, :] + dot(m.astype(BF16), wmix_ref[...]) for s, m in zip(chunks, merged)]

    h2 = [_rms(v, gx_ref[...]).astype(BF16) for v in x1]
    q = [(dot(v, wcq_ref[...]) * X_HEAD_DIM ** -0.5).astype(BF16) for v in h2]
    heads = [[] for _ in chunks]
    for hh in range(X_HEADS):
        lo = hh * X_HEAD_DIM
        kh = kv_ref[:, lo:lo + X_HEAD_DIM]
        vh = kv_ref[:, D_MODEL + lo:D_MODEL + lo + X_HEAD_DIM]
        s = [lax.dot_general(qc[:, lo:lo + X_HEAD_DIM], kh, (((1,), (1,)), ((), ())),
                             preferred_element_type=F32) for qc in q]
        p = [jnp.exp(sc - jnp.max(sc, axis=-1, keepdims=True)) for sc in s]
        p = [pc * (1.0 / jnp.sum(pc, axis=-1, keepdims=True)) for pc in p]
        for c, pc in enumerate(p):
            heads[c].append(dot(pc.astype(BF16), vh).astype(BF16))
    o = [jnp.concatenate(hc, axis=1) for hc in heads]
    x2 = [v + dot(oc, wco_ref[...]) for v, oc in zip(x1, o)]
    for s, v in zip(chunks, x2):
        x2_ref[s, :] = v

    h3 = [_rms(v, gm_ref[...]) for v in x2]
    for s, v in zip(chunks, h3):
        h3_ref[s, :] = v
    logits = []
    for v in h3:
        hi = v.astype(BF16)
        lo = (v - hi.astype(F32)).astype(BF16)
        r = dot(jnp.concatenate([hi, lo], axis=0), wr_ref[...])
        logits.append(r[:cr, :N_EXPERTS] + r[:cr, N_EXPERTS:] + r[cr:, :N_EXPERTS] + br_ref[...])

    e_iota = lax.broadcasted_iota(jnp.int32, (cr, N_EXPERTS), 1).astype(F32)
    r_i = lax.broadcasted_iota(jnp.int32, (cr, cr), 0)
    c_i = lax.broadcasted_iota(jnp.int32, (cr, cr), 1)
    before = jnp.where(c_i < r_i, 1.0, 0.0).astype(BF16)
    count = cnt_scr[...]
    for s, rem in zip(chunks, logits):
        vals, idxs = [], []
        for _ in range(TOP_K):
            m = jnp.max(rem, axis=-1, keepdims=True)
            ik = jnp.min(jnp.where(rem == m, e_iota, float(N_EXPERTS)), axis=-1, keepdims=True)
            vals.append(m)
            idxs.append(ik)
            rem = jnp.where(e_iota == ik, -jnp.inf, rem)
        ex = [jnp.exp(v - vals[0]) for v in vals]
        inv_den = 1.0 / (ex[0] + ex[1] + ex[2] + ex[3])
        gates = [e * inv_den for e in ex]

        chosen = [(e_iota == ik) for ik in idxs]
        member = jnp.zeros((cr, N_EXPERTS), F32)
        for ch in chosen:
            member = member + jnp.where(ch, 1.0, 0.0)
        prefix = dot(before, member.astype(BF16)) + count
        ranks = [jnp.sum(jnp.where(ch, prefix, 0.0), axis=-1, keepdims=True) for ch in chosen]
        count = count + jnp.sum(member, axis=0, keepdims=True)

        idx_ref[s, :] = _lane_pack(idxs, cr, jnp.int32)
        rank_ref[s, :] = _lane_pack(ranks, cr, jnp.int32)
        gate_ref[s, :] = _lane_pack(gates, cr, F32)
    cnt_scr[...] = count
    cnt_ref[...] = count


def _mix(x, proj3, y_sb, y_ret, kv, wsbo, wreto, wmix, wcq, wco, gx, gm, wr, br):
    B, S, _ = x.shape
    tm = min(512, S)
    M = kv.shape[1]
    gsb_b, gret_b = OFF_G_SB // D_MODEL, OFF_G_RET // D_MODEL
    tok = lambda b, i: (b, i, 0)
    const2 = lambda b, i: (0, 0)
    row_out = lambda w, dt: jax.ShapeDtypeStruct((B, S, w), dt)
    outs = pl.pallas_call(
        _mix_kernel,
        grid=(B, S // tm),
        in_specs=[
            pl.BlockSpec((None, tm, D_MODEL), tok),
            pl.BlockSpec((None, tm, SB_HEADS * SB_HEAD_DIM), tok),
            pl.BlockSpec((None, tm, D_MODEL), tok),
            pl.BlockSpec((None, tm, D_MODEL), lambda b, i: (b, i, gsb_b)),
            pl.BlockSpec((None, tm, D_MODEL), lambda b, i: (b, i, gret_b)),
            pl.BlockSpec((None, M, 2 * D_MODEL), lambda b, i: (b, 0, 0)),
            pl.BlockSpec((SB_HEADS * SB_HEAD_DIM, D_MODEL), const2),
            pl.BlockSpec((D_MODEL, D_MODEL), const2),
            pl.BlockSpec((D_MODEL, D_MODEL), const2),
            pl.BlockSpec((D_MODEL, D_MODEL), const2),
            pl.BlockSpec((D_MODEL, D_MODEL), const2),
            pl.BlockSpec((1, D_MODEL), const2),
            pl.BlockSpec((1, D_MODEL), const2),
            pl.BlockSpec((D_MODEL, 2 * N_EXPERTS), const2),
            pl.BlockSpec((1, N_EXPERTS), const2),
        ],
        out_specs=[
            pl.BlockSpec((None, tm, D_MODEL), tok),
            pl.BlockSpec((None, tm, D_MODEL), tok),
            pl.BlockSpec((None, tm, LANES), tok),
            pl.BlockSpec((None, tm, LANES), tok),
            pl.BlockSpec((None, tm, LANES), tok),
            pl.BlockSpec((1, N_EXPERTS), const2),
        ],
        out_shape=[
            row_out(D_MODEL, F32), row_out(D_MODEL, F32),
            row_out(LANES, jnp.int32), row_out(LANES, F32), row_out(LANES, jnp.int32),
            jax.ShapeDtypeStruct((1, N_EXPERTS), F32),
        ],
        scratch_shapes=[pltpu.VMEM((1, N_EXPERTS), F32)],
        compiler_params=_params(("arbitrary", "arbitrary")),
        name="mix",
    )(x, y_sb, y_ret, proj3, proj3, kv, wsbo, wreto, wmix, wcq, wco, gx, gm, wr, br)
    return outs


def _row_copy(src, s, dst, d, sem):
    return pltpu.make_async_copy(src.at[pl.ds(s, 1)], dst.at[pl.ds(d, 1)], sem)


DMA_UNROLL = 8


ZERO_ROWS = 256
SUBLANES = 8


def _dispatch_kernel(lo_ref, hi_ref, pos_ref, h_ref, xin_ref, zbuf, sem, zsem):
    tm = h_ref.shape[0]

    def zero_fill(act):
        def per_expert(e, carry):
            lo = lo_ref[e]
            hi = hi_ref[e]
            head_end = jnp.minimum((lo + SUBLANES - 1) // SUBLANES * SUBLANES, hi)

            def single(r, c):
                act(_row_copy(zbuf, 0, xin_ref, r, zsem))
                return c

            lax.fori_loop(lo, head_end, single, 0)

            def chunk(i, off):
                act(pltpu.make_async_copy(zbuf, xin_ref.at[pl.ds(pl.multiple_of(off, SUBLANES), ZERO_ROWS)], zsem))
                return off + ZERO_ROWS

            off = lax.fori_loop(0, (hi - head_end) // ZERO_ROWS, chunk, head_end)
            rest = hi - off
            size = ZERO_ROWS // 2
            while size >= SUBLANES:
                @pl.when((rest & size) != 0)
                def _(off=off, size=size):
                    act(pltpu.make_async_copy(
                        zbuf.at[pl.ds(0, size)],
                        xin_ref.at[pl.ds(pl.multiple_of(off, SUBLANES), size)], zsem))
                off = off + (rest & size)
                size //= 2
            return carry

        lax.fori_loop(0, N_EXPERTS, per_expert, 0)

    @pl.when(pl.program_id(0) == 0)
    def _():
        zbuf[...] = jnp.zeros_like(zbuf)
        zero_fill(lambda cp: cp.start())
        zero_fill(lambda cp: cp.wait())

    def issue(t, carry):
        for k in range(TOP_K):
            _row_copy(h_ref, t, xin_ref, pos_ref[0, 0, t * TOP_K + k], sem).start(priority=k % 2)
        return carry

    lax.fori_loop(0, tm, issue, 0, unroll=DMA_UNROLL)
    for _ in range(TOP_K):
        pltpu.make_async_copy(h_ref, xin_ref.at[pl.ds(0, tm)], sem).wait()


def _dispatch(h3, pos, pad_lo, pad_hi, n_rows):
    T = h3.shape[0]
    tm = min(256, T)
    pos3 = pos.reshape(T // tm, 1, tm * TOP_K)
    grid_spec = pltpu.PrefetchScalarGridSpec(
        num_scalar_prefetch=2,
        grid=(T // tm,),
        in_specs=[
            pl.BlockSpec((1, 1, tm * TOP_K), lambda i, lo, hi: (i, 0, 0), memory_space=pltpu.SMEM),
            pl.BlockSpec((tm, D_MODEL), lambda i, lo, hi: (i, 0)),
        ],
        out_specs=pl.BlockSpec(memory_space=pl.ANY),
        scratch_shapes=[pltpu.VMEM((ZERO_ROWS, D_MODEL), F32), pltpu.SemaphoreType.DMA(()),
                        pltpu.SemaphoreType.DMA(())],
    )
    return pl.pallas_call(
        _dispatch_kernel,
        grid_spec=grid_spec,
        out_shape=jax.ShapeDtypeStruct((n_rows, D_MODEL), F32),
        compiler_params=_params(("arbitrary",)),
        name="dispatch",
    )(pad_lo, pad_hi, pos3, h3)


def _ffn_kernel(first_ref, count_ref, nreal_ref, x_hbm, w1_ref, b1_ref, w2_ref, b2_ref, y_hbm,
                xbuf, ybuf, w1b, w2b, in_sem, out_sem):
    e = pl.program_id(0)
    tb = FFN_BLOCK
    n_real = nreal_ref[0]

    def rows(g):
        return pl.ds(pl.multiple_of(g * tb, tb), tb)

    def x_copy(g, slot):
        return pltpu.make_async_copy(x_hbm.at[rows(g)], xbuf.at[slot], in_sem.at[slot])

    def y_copy(g, slot):
        return pltpu.make_async_copy(ybuf.at[slot], y_hbm.at[rows(g)], out_sem.at[slot])

    @pl.when(jnp.logical_and(e == 0, n_real > 0))
    def _():
        x_copy(0, 0).start(priority=1)

    @pl.when(count_ref[e] > 0)
    def _():
        w1b[...] = w1_ref[...].astype(BF16)
        w2b[...] = w2_ref[...].astype(BF16)

    def block(j, carry):
        g = first_ref[e] + j
        slot = g % 2
        x_copy(g, slot).wait()

        @pl.when(g + 1 < n_real)
        def _():
            x_copy(g + 1, 1 - slot).start(priority=1)

        @pl.when(g >= 2)
        def _():
            y_copy(g - 2, slot).wait()

        hc = jnp.dot(xbuf[slot].astype(BF16), w1b[...], preferred_element_type=F32) + b1_ref[...]
        glu = jnp.minimum(hc[:, :D_FF], SWIGLU_LIMIT)
        lin = jnp.clip(hc[:, D_FF:], -SWIGLU_LIMIT, SWIGLU_LIMIT)
        act = glu * jax.nn.sigmoid(SWIGLU_ALPHA * glu) * (lin + 1.0)
        ybuf[slot] = jnp.dot(act.astype(BF16), w2b[...], preferred_element_type=F32) + b2_ref[...]
        y_copy(g, slot).start(priority=1)
        return carry

    lax.fori_loop(0, count_ref[e], block, 0)

    @pl.when(e == N_EXPERTS - 1)
    def _():
        for back in (2, 1):
            @pl.when(n_real >= back)
            def _(back=back):
                y_copy(n_real - back, (n_real - back) % 2).wait()

        ybuf[0] = jnp.zeros((tb, D_MODEL), F32)

        def tail(g, carry):
            cp = y_copy(g, 0)
            cp.start()
            cp.wait()
            return carry

        lax.fori_loop(n_real, y_hbm.shape[0] // tb, tail, 0)


def _ffn(xin, first_block, block_count, n_real, w1, b1, w2, b2):
    P = xin.shape[0]
    tb = FFN_BLOCK
    expert = lambda e, first, count, nr: (e, 0, 0)
    grid_spec = pltpu.PrefetchScalarGridSpec(
        num_scalar_prefetch=3,
        grid=(N_EXPERTS,),
        in_specs=[
            pl.BlockSpec(memory_space=pl.ANY),
            pl.BlockSpec((None, D_MODEL, 2 * D_FF), expert),
            pl.BlockSpec((None, 1, 2 * D_FF), expert),
            pl.BlockSpec((None, D_FF, D_MODEL), expert),
            pl.BlockSpec((None, 1, D_MODEL), expert),
        ],
        out_specs=pl.BlockSpec(memory_space=pl.ANY),
        scratch_shapes=[pltpu.VMEM((2, tb, D_MODEL), F32), pltpu.VMEM((2, tb, D_MODEL), F32),
                        pltpu.VMEM((D_MODEL, 2 * D_FF), BF16), pltpu.VMEM((D_FF, D_MODEL), BF16),
                        pltpu.SemaphoreType.DMA((2,)), pltpu.SemaphoreType.DMA((2,))],
    )
    return pl.pallas_call(
        _ffn_kernel,
        grid_spec=grid_spec,
        out_shape=jax.ShapeDtypeStruct((P, D_MODEL), F32),
        compiler_params=_params(("arbitrary",)),
        name="ffn",
    )(first_block, block_count, n_real, xin, w1, b1.reshape(N_EXPERTS, 1, 2 * D_FF), w2,
      b2.reshape(N_EXPERTS, 1, D_MODEL))


def _combine_kernel(pos_ref, nxt_ref, x2_ref, gate_ref, g_ref, y_ref, o_ref, buf, sem):
    tm = x2_ref.shape[0]
    i = pl.program_id(0)
    slot = i % 2

    def gather(p_ref, s):
        def issue(t, carry):
            for k in range(TOP_K):
                _row_copy(y_ref, p_ref[0, 0, t * TOP_K + k], buf.at[s, k], t,
                          sem.at[s]).start(priority=k % 2)
            return carry
        lax.fori_loop(0, tm, issue, 0, unroll=DMA_UNROLL)

    @pl.when(i == 0)
    def _():
        gather(pos_ref, slot)

    @pl.when(i + 1 < pl.num_programs(0))
    def _():
        gather(nxt_ref, 1 - slot)

    for k in range(TOP_K):
        pltpu.make_async_copy(y_ref.at[pl.ds(0, tm)], buf.at[slot, k], sem.at[slot]).wait()

    acc = x2_ref[...]
    gate = gate_ref[...]
    for k in range(TOP_K):
        acc = acc + gate[:, k:k + 1] * buf[slot, k]
    o_ref[...] = _rms(acc, g_ref[...])


def _combine(x2, gate, pos, yrows, g_final):
    T = x2.shape[0]
    tm = min(256, T)
    n = T // tm
    pos3 = pos.reshape(n, 1, tm * TOP_K)
    return pl.pallas_call(
        _combine_kernel,
        grid=(n,),
        in_specs=[
            pl.BlockSpec((1, 1, tm * TOP_K), lambda i: (i, 0, 0), memory_space=pltpu.SMEM),
            pl.BlockSpec((1, 1, tm * TOP_K), lambda i: (jnp.minimum(i + 1, n - 1), 0, 0),
                         memory_space=pltpu.SMEM),
            pl.BlockSpec((tm, D_MODEL), lambda i: (i, 0)),
            pl.BlockSpec((tm, LANES), lambda i: (i, 0)),
            pl.BlockSpec((1, D_MODEL), lambda i: (0, 0)),
            pl.BlockSpec(memory_space=pl.ANY),
        ],
        out_specs=pl.BlockSpec((tm, D_MODEL), lambda i: (i, 0)),
        out_shape=jax.ShapeDtypeStruct((T, D_MODEL), F32),
        scratch_shapes=[pltpu.VMEM((2, TOP_K, tm, D_MODEL), F32), pltpu.SemaphoreType.DMA((2,))],
        compiler_params=_params(("arbitrary",)),
        name="combine",
    )(pos3, pos3, x2, gate, g_final, yrows)


def kernel(x, mem, g_mix, w_in, w_sb_o, w_ret_o, w_mix_out, g_xattn, g_mem, w_cq, w_ckv, w_co,
           g_moe, w_router, b_router, w_exp_in, b_exp_in, w_exp_out, b_exp_out, g_final):
    B, S, _ = x.shape
    T = B * S
    assert w_in.shape[0] == 1, "one layer"
    row = lambda v: v.reshape(1, -1)

    w_in_k = jnp.concatenate([w_in[0][:, o:o + w] for _, o, w in _IN_GROUPS], axis=1).astype(BF16)
    proj = _inproj(x.reshape(T, D_MODEL), row(g_mix[0]), w_in_k)
    proj3 = proj.reshape(B, S, IN_W)
    y_sb = _sb_attention(proj3)
    y_ret = _retention(proj3)
    kv = _memkv(mem, row(g_mem[0]), w_ckv[0].astype(BF16))
    wr_hi = w_router[0].astype(BF16)
    wr_lo = (w_router[0] - wr_hi.astype(F32)).astype(BF16)
    x2, h3, idx, gate, rank, cnt = _mix(
        x, proj3, y_sb, y_ret, kv,
        w_sb_o[0].astype(BF16), w_ret_o[0].astype(BF16), w_mix_out[0].astype(BF16),
        w_cq[0].astype(BF16), w_co[0].astype(BF16),
        row(g_xattn[0]), row(g_moe[0]), jnp.concatenate([wr_hi, wr_lo], axis=1), row(b_router[0]))

    tb = FFN_BLOCK
    n_blocks = (T * TOP_K + N_EXPERTS * (tb - 1) + tb - 1) // tb
    counts = cnt[0].astype(jnp.int32)
    padded = ((counts + tb - 1) // tb) * tb
    pad_end = jnp.cumsum(padded)
    pad_start = pad_end - padded
    idx4 = idx.reshape(T, LANES)[:, :TOP_K]
    start4 = jnp.sum(jnp.where(idx4[:, :, None] == jnp.arange(N_EXPERTS), pad_start, 0), axis=-1)
    pos = (start4 + rank.reshape(T, LANES)[:, :TOP_K]).astype(jnp.int32).reshape(T * TOP_K)
    n_real = (pad_end[-1:] // tb).astype(jnp.int32)
    first_block = (pad_start // tb).astype(jnp.int32)
    block_count = (padded // tb).astype(jnp.int32)

    pad_lo = (pad_start + counts).astype(jnp.int32)
    pad_hi = jnp.where(jnp.arange(N_EXPERTS) == N_EXPERTS - 1, n_blocks * tb, pad_end).astype(jnp.int32)
    xin = _dispatch(h3.reshape(T, D_MODEL), pos, pad_lo, pad_hi, n_blocks * tb)
    yrows = _ffn(xin, first_block, block_count, n_real,
                 w_exp_in[0], b_exp_in[0], w_exp_out[0], b_exp_out[0])
    out = _combine(x2.reshape(T, D_MODEL), gate.reshape(T, LANES), pos, yrows, row(g_final))
    return out.reshape(B, S, D_MODEL)
```

```python
import functools
import math

import numpy as np
import jax
import jax.numpy as jnp
from jax import lax
from jax.experimental import pallas as pl
from jax.experimental.pallas import tpu as pltpu

F32 = jnp.float32
BF16 = jnp.bfloat16

D_MODEL = 1024
SB_HEADS = 8
SB_HEAD_DIM = 64
RET_HEADS = 8
RET_QK_DIM = 64
RET_V_DIM = 128
ROPE_BASE = 10000.0
X_HEADS = 4
X_HEAD_DIM = 256
N_EXPERTS = 32
TOP_K = 4
D_FF = 1024
SWIGLU_LIMIT = 7.0
SWIGLU_ALPHA = 1.702
NORM_EPS = 1e-5

_IN_GROUPS = (("r_v", 2560, 1024), ("r_g", 3584, 1024), ("g_sb", 4608, 1024), ("g_ret", 5632, 1024),
              ("sb_q", 0, 512), ("sb_k", 512, 512), ("sb_v", 1024, 512),
              ("r_q", 1536, 512), ("r_k", 2048, 512))
OFF_R_V, OFF_R_G, OFF_G_SB, OFF_G_RET = 0, 1024, 2048, 3072
OFF_SB_Q, OFF_SB_K, OFF_SB_V, OFF_R_Q, OFF_R_K = 4096, 4608, 5120, 5632, 6144
IN_W = 6656

LANES = 128
VMEM_LIMIT = 56 * 1024 * 1024

SB_TILE = 128
SB_LOG_UNDERFLOW = -88.0
RET_CHUNK = 256
FFN_BLOCK = 256
FFN_SLOTS = 4


def _rms(x, g):
    return x * lax.rsqrt(jnp.mean(x * x, axis=-1, keepdims=True) + NORM_EPS) * g


def _params(sem):
    return pltpu.CompilerParams(dimension_semantics=sem, vmem_limit_bytes=VMEM_LIMIT)


INPROJ_COLS = 512


def _inproj_kernel(x_ref, g_ref, w_ref, o_ref):
    h = _rms(x_ref[...], g_ref[...]).astype(BF16)
    for j in range(IN_W // INPROJ_COLS):
        cols = slice(j * INPROJ_COLS, (j + 1) * INPROJ_COLS)
        o_ref[:, cols] = jnp.dot(h, w_ref[:, cols], preferred_element_type=F32).astype(o_ref.dtype)


def _inproj(x2d, g, w_bf16):
    T = x2d.shape[0]
    tm = min(512, T)
    return pl.pallas_call(
        _inproj_kernel,
        grid=(T // tm,),
        in_specs=[
            pl.BlockSpec((tm, D_MODEL), lambda i: (i, 0)),
            pl.BlockSpec((1, D_MODEL), lambda i: (0, 0)),
            pl.BlockSpec((D_MODEL, IN_W), lambda i: (0, 0)),
        ],
        out_specs=pl.BlockSpec((tm, IN_W), lambda i: (i, 0)),
        out_shape=jax.ShapeDtypeStruct((T, IN_W), BF16),
        compiler_params=_params(("parallel",)),
        name="inproj",
    )(x2d, g, w_bf16)


SB_PAIRS = SB_HEADS * SB_HEAD_DIM // LANES


def _sb_kernel(q_ref, k_ref, v_ref, o_ref, q2_ref, *state_refs):
    carry_refs, acc_refs = state_refs[:SB_PAIRS], state_refs[SB_PAIRS:]
    n = SB_TILE
    qi = pl.program_id(1)
    lane = lax.broadcasted_iota(jnp.int32, (n, LANES), 1)
    first = lane < SB_HEAD_DIM
    row = lax.broadcasted_iota(jnp.int32, (2 * n, n), 0) & (n - 1)
    col = lax.broadcasted_iota(jnp.int32, (2 * n, n), 1)
    strict = col < row
    jj = lax.broadcasted_iota(jnp.int32, (2 * n, 2 * n), 0) & (n - 1)
    cc = lax.broadcasted_iota(jnp.int32, (2 * n, 2 * n), 1)
    suffix = jnp.where((cc >= n) | (jj >= cc), 1.0, 0.0).astype(BF16)
    scale = SB_HEAD_DIM ** -0.5

    for p in range(SB_PAIRS):
        qs = (q_ref[:, p * LANES:(p + 1) * LANES].astype(F32) * scale).astype(BF16)
        zeros = jnp.zeros_like(qs)
        q2_ref[p * 2 * n:p * 2 * n + n, :] = jnp.where(first, qs, zeros)
        q2_ref[p * 2 * n + n:(p + 1) * 2 * n, :] = jnp.where(first, zeros, qs)

    def step(kt, diagonal):
        ks = pl.multiple_of(kt * n, n)
        zs, hls, css, ws, cmaxs = {}, {}, {}, {}, []

        def logits(p):
            kk = k_ref[pl.ds(ks, n), p * LANES:(p + 1) * LANES]
            z = lax.dot_general(q2_ref[p * 2 * n:(p + 1) * 2 * n, :], kk, (((1,), (1,)), ((), ())),
                                preferred_element_type=F32)
            sp = jnp.maximum(z, 0.0) + jnp.log(1.0 + jnp.exp(-jnp.abs(z)))
            if diagonal:
                sp = jnp.where(strict, sp, 0.0)
            hi = sp.astype(BF16)
            lo = (sp - hi.astype(F32)).astype(BF16)
            zs[p] = z
            hls[p] = jnp.concatenate([hi, lo], axis=1)

        def sums(p):
            css[p] = jnp.dot(hls[p], suffix, preferred_element_type=F32)

        def weights(p):
            logw = zs[p] - css[p][:, :n]
            if not diagonal:
                logw = logw + carry_refs[p][...]
            w = jnp.exp(logw)
            if diagonal:
                w = jnp.where(strict, w, 0.0)
            ws[p] = w.astype(BF16)

        def values(p):
            vv = v_ref[pl.ds(ks, n), p * LANES:(p + 1) * LANES]
            pv = jnp.dot(ws[p], vv, preferred_element_type=F32)
            acc_refs[p][...] = pv if diagonal else acc_refs[p][...] + pv
            carry = -css[p][:, n:] if diagonal else carry_refs[p][...] - css[p][:, n:]
            carry_refs[p][...] = carry
            cmaxs.append(jnp.max(carry))

        for stage in (logits, sums, weights, values):
            for p in range(SB_PAIRS):
                stage(p)
        return functools.reduce(jnp.maximum, cmaxs)

    def cond(st):
        kt, cmax = st
        return jnp.logical_and(kt >= 0, cmax > SB_LOG_UNDERFLOW)

    def body(st):
        kt, _ = st
        return kt - 1, step(kt, False)

    lax.while_loop(cond, body, (qi - 1, step(qi, True)))

    for p in range(SB_PAIRS):
        o_ref[:, p * LANES:(p + 1) * LANES] = jnp.where(
            first, acc_refs[p][:n, :], acc_refs[p][n:, :]
        ).astype(o_ref.dtype)


def _sb_attention(proj3):
    B, S, _ = proj3.shape
    n = SB_TILE
    w = SB_HEADS * SB_HEAD_DIM
    qb, kb, vb = OFF_SB_Q // w, OFF_SB_K // w, OFF_SB_V // w
    rows = SB_PAIRS * 2 * n
    return pl.pallas_call(
        _sb_kernel,
        grid=(B, S // n),
        in_specs=[
            pl.BlockSpec((None, n, w), lambda b, i: (b, i, qb)),
            pl.BlockSpec((None, S, w), lambda b, i: (b, 0, kb)),
            pl.BlockSpec((None, S, w), lambda b, i: (b, 0, vb)),
        ],
        out_specs=pl.BlockSpec((None, n, w), lambda b, i: (b, i, 0)),
        out_shape=jax.ShapeDtypeStruct((B, S, w), BF16),
        scratch_shapes=([pltpu.VMEM((rows, LANES), BF16)]
                        + [pltpu.VMEM((2 * n, LANES), F32)] * (2 * SB_PAIRS)),
        compiler_params=_params(("parallel", "arbitrary")),
        name="sb",
    )(proj3, proj3, proj3)


def _ret_tables(S):
    c = RET_CHUNK
    f = np.float32
    inv = f(ROPE_BASE) ** (-np.arange(0, RET_QK_DIM, 2, dtype=f) / f(RET_QK_DIM))
    ang = np.arange(S, dtype=f)[:, None] * inv[None, :]
    cos, sin = np.cos(ang), np.sin(ang)
    reps = LANES // RET_QK_DIM
    cos_t = np.tile(np.concatenate([cos, cos], axis=1), (1, reps))
    sin_t = np.tile(np.concatenate([-sin, sin], axis=1), (1, reps))
    log_g = np.log1p(-np.exp2(-5.0 - np.arange(RET_HEADS, dtype=np.float64)))
    pos = np.arange(c, dtype=np.float64)
    diff = pos[:, None] - pos[None, :]
    decay = np.where(diff >= 0, np.exp(log_g[:, None, None] * np.maximum(diff, 0.0)), 0.0).astype(f)
    q_dec = np.exp(log_g[:, None] * (pos + 1.0)[None, :]).astype(f)
    k_dec = np.exp(log_g[:, None] * (c - 1 - pos)[None, :]).astype(f)
    q_dec = np.ascontiguousarray(np.broadcast_to(q_dec[:, :, None], (RET_HEADS, c, LANES)))
    k_dec = np.ascontiguousarray(np.broadcast_to(k_dec[:, :, None], (RET_HEADS, c, LANES)))
    c_dec = np.ascontiguousarray(np.broadcast_to(
        np.exp(log_g * c).astype(f)[:, None, None], (RET_HEADS, LANES, LANES)))
    return cos_t, sin_t, decay, q_dec, k_dec, c_dec


def _ret_kernel(q_ref, k_ref, v_ref, g_ref, cos_ref, sin_ref, dm_ref, qd_ref, kd_ref, cd_ref,
                o_ref, state):
    c = RET_CHUNK

    @pl.when(pl.program_id(1) == 0)
    def _():
        state[...] = jnp.zeros_like(state)

    lane = lax.broadcasted_iota(jnp.int32, (c, LANES), 1)
    low_half = (lane & (RET_QK_DIM - 1)) < RET_QK_DIM // 2
    cos = cos_ref[...]
    sin = sin_ref[...]

    def rotary(x):
        swapped = jnp.where(low_half, pltpu.roll(x, LANES - RET_QK_DIM // 2, 1),
                            pltpu.roll(x, RET_QK_DIM // 2, 1))
        return x * cos + swapped * sin

    heads = range(RET_HEADS)
    cols = [slice(h * RET_V_DIM, (h + 1) * RET_V_DIM) for h in heads]
    rq, rk, qh, sc, out = {}, {}, {}, {}, {}

    for p in range(RET_HEADS * RET_QK_DIM // LANES):
        rq[p] = rotary(q_ref[:, p * LANES:(p + 1) * LANES].astype(F32))
        rk[p] = rotary(k_ref[:, p * LANES:(p + 1) * LANES].astype(F32)) * RET_QK_DIM ** -0.5
    for h in heads:
        in_head = (lane < RET_QK_DIM) if h % 2 == 0 else (lane >= RET_QK_DIM)
        qh[h] = jnp.where(in_head, rq[h // 2], 0.0)
        sc[h] = lax.dot_general(qh[h].astype(BF16), rk[h // 2].astype(BF16), (((1,), (1,)), ((), ())),
                                preferred_element_type=F32) * dm_ref[h]
    for h in heads:
        vh = v_ref[:, cols[h]]
        o = jnp.dot(sc[h].astype(BF16), vh, preferred_element_type=F32)
        out[h] = o + jnp.dot((qh[h] * qd_ref[h]).astype(BF16), state[h].astype(BF16),
                             preferred_element_type=F32)
        kv_new = lax.dot_general((rk[h // 2] * kd_ref[h]).astype(BF16), vh, (((0,), (0,)), ((), ())),
                                 preferred_element_type=F32)
        state[h] = state[h] * cd_ref[h] + kv_new
    for h in heads:
        o = out[h]
        mu = jnp.mean(o, axis=-1, keepdims=True)
        d = o - mu
        y = d * lax.rsqrt(jnp.mean(d * d, axis=-1, keepdims=True) + NORM_EPS)
        g = g_ref[:, cols[h]].astype(F32)
        o_ref[:, cols[h]] = (g * jax.nn.sigmoid(g) * y).astype(o_ref.dtype)


def _retention(proj3):
    B, S, _ = proj3.shape
    c = RET_CHUNK
    qkw, vw = RET_HEADS * RET_QK_DIM, RET_HEADS * RET_V_DIM
    qb, kb, vb, gb = OFF_R_Q // qkw, OFF_R_K // qkw, OFF_R_V // vw, OFF_R_G // vw
    cos_t, sin_t, decay, q_dec, k_dec, c_dec = _ret_tables(S)
    const3 = lambda b, i: (0, 0, 0)
    return pl.pallas_call(
        _ret_kernel,
        grid=(B, S // c),
        in_specs=[
            pl.BlockSpec((None, c, qkw), lambda b, i: (b, i, qb)),
            pl.BlockSpec((None, c, qkw), lambda b, i: (b, i, kb)),
            pl.BlockSpec((None, c, vw), lambda b, i: (b, i, vb)),
            pl.BlockSpec((None, c, vw), lambda b, i: (b, i, gb)),
            pl.BlockSpec((c, LANES), lambda b, i: (i, 0)),
            pl.BlockSpec((c, LANES), lambda b, i: (i, 0)),
            pl.BlockSpec((RET_HEADS, c, c), const3),
            pl.BlockSpec((RET_HEADS, c, LANES), const3),
            pl.BlockSpec((RET_HEADS, c, LANES), const3),
            pl.BlockSpec((RET_HEADS, LANES, LANES), const3),
        ],
        out_specs=pl.BlockSpec((None, c, vw), lambda b, i: (b, i, 0)),
        out_shape=jax.ShapeDtypeStruct((B, S, vw), BF16),
        scratch_shapes=[pltpu.VMEM((RET_HEADS, LANES, LANES), F32)],
        compiler_params=_params(("parallel", "arbitrary")),
        name="ret",
    )(proj3, proj3, proj3, proj3, cos_t, sin_t, decay, q_dec, k_dec, c_dec)


def _memkv_kernel(m_ref, g_ref, w_ref, o_ref):
    h = _rms(m_ref[...], g_ref[...]).astype(BF16)
    o_ref[...] = jnp.dot(h, w_ref[...], preferred_element_type=F32).astype(o_ref.dtype)


def _memkv(mem, g, w_bf16):
    B, M, _ = mem.shape
    return pl.pallas_call(
        _memkv_kernel,
        grid=(B,),
        in_specs=[
            pl.BlockSpec((None, M, D_MODEL), lambda b: (b, 0, 0)),
            pl.BlockSpec((1, D_MODEL), lambda b: (0, 0)),
            pl.BlockSpec((D_MODEL, 2 * D_MODEL), lambda b: (0, 0)),
        ],
        out_specs=pl.BlockSpec((None, M, 2 * D_MODEL), lambda b: (b, 0, 0)),
        out_shape=jax.ShapeDtypeStruct((B, M, 2 * D_MODEL), BF16),
        compiler_params=_params(("parallel",)),
        name="memkv",
    )(mem, g, w_bf16)


MIX_CHUNKS = 2


def _lane_pack(cols, rows, dtype):
    lane = lax.broadcasted_iota(jnp.int32, (rows, LANES), 1)
    out = jnp.zeros((rows, LANES), dtype)
    for k, c in enumerate(cols):
        out = jnp.where(lane == k, c.astype(dtype), out)
    return out


def _mix_kernel(x_ref, ysb_ref, yret_ref, gsb_ref, gret_ref, kv_ref,
                wsbo_ref, wreto_ref, wmix_ref, wcq_ref, wco_ref,
                gx_ref, gm_ref, wr_ref, br_ref,
                x2_ref, h3_ref, idx_ref, gate_ref, rank_ref, cnt_ref, cnt_scr):
    tm = x_ref.shape[0]
    cr = tm // MIX_CHUNKS
    chunks = [slice(c * cr, (c + 1) * cr) for c in range(MIX_CHUNKS)]
    dot = functools.partial(jnp.dot, preferred_element_type=F32)

    @pl.when(jnp.logical_and(pl.program_id(0) == 0, pl.program_id(1) == 0))
    def _():
        cnt_scr[...] = jnp.zeros_like(cnt_scr)

    def sigmoid(g):
        return (0.5 * jnp.tanh(0.5 * g) + 0.5).astype(F32)

    a = [dot(ysb_ref[s, :], wsbo_ref[...]) for s in chunks]
    b = [dot(yret_ref[s, :], wreto_ref[...]) for s in chunks]
    merged = [sigmoid(gsb_ref[s, :]) * ac + sigmoid(gret_ref[s, :]) * bc
              for s, ac, bc in zip(chunks, a, b)]
    x1 = [x_ref[s, :] + dot(m.astype(BF16), wmix_ref[...]) for s, m in zip(chunks, merged)]

    h2 = [_rms(v, gx_ref[...]).astype(BF16) for v in x1]
    q = [(dot(v, wcq_ref[...]) * X_HEAD_DIM ** -0.5).astype(BF16) for v in h2]
    heads = [[] for _ in chunks]
    for hh in range(X_HEADS):
        lo = hh * X_HEAD_DIM
        kh = kv_ref[:, lo:lo + X_HEAD_DIM]
        vh = kv_ref[:, D_MODEL + lo:D_MODEL + lo + X_HEAD_DIM]
        s = [lax.dot_general(qc[:, lo:lo + X_HEAD_DIM], kh, (((1,), (1,)), ((), ())),
                             preferred_element_type=F32) for qc in q]
        p = [jnp.exp(sc - jnp.max(sc, axis=-1, keepdims=True)) for sc in s]
        p = [pc * (1.0 / jnp.sum(pc, axis=-1, keepdims=True)) for pc in p]
        for c, pc in enumerate(p):
            heads[c].append(dot(pc.astype(BF16), vh).astype(BF16))
    o = [jnp.concatenate(hc, axis=1) for hc in heads]
    x2 = [v + dot(oc, wco_ref[...]) for v, oc in zip(x1, o)]
    for s, v in zip(chunks, x2):
        x2_ref[s, :] = v

    h3 = [_rms(v, gm_ref[...]) for v in x2]
    for ci, v in enumerate(h3):
        for c in range(ROW_TILES):
            h3_ref[pl.ds(ci * cr * ROW_TILES + c, cr, stride=ROW_TILES), :] = v[:, c * LANES:(c + 1) * LANES]
    logits = []
    for v in h3:
        hi = v.astype(BF16)
        lo = (v - hi.astype(F32)).astype(BF16)
        r = dot(jnp.concatenate([hi, lo], axis=0), wr_ref[...])
        logits.append(r[:cr, :N_EXPERTS] + r[:cr, N_EXPERTS:] + r[cr:, :N_EXPERTS] + br_ref[...])

    e_iota = lax.broadcasted_iota(jnp.int32, (cr, N_EXPERTS), 1).astype(F32)
    r_i = lax.broadcasted_iota(jnp.int32, (cr, cr), 0)
    c_i = lax.broadcasted_iota(jnp.int32, (cr, cr), 1)
    before = jnp.where(c_i < r_i, 1.0, 0.0).astype(BF16)
    count = cnt_scr[...]
    for s, rem in zip(chunks, logits):
        vals, idxs = [], []
        for _ in range(TOP_K):
            m = jnp.max(rem, axis=-1, keepdims=True)
            ik = jnp.min(jnp.where(rem == m, e_iota, float(N_EXPERTS)), axis=-1, keepdims=True)
            vals.append(m)
            idxs.append(ik)
            rem = jnp.where(e_iota == ik, -jnp.inf, rem)
        ex = [jnp.exp(v - vals[0]) for v in vals]
        inv_den = 1.0 / (ex[0] + ex[1] + ex[2] + ex[3])
        gates = [e * inv_den for e in ex]

        chosen = [(e_iota == ik) for ik in idxs]
        member = jnp.zeros((cr, N_EXPERTS), F32)
        for ch in chosen:
            member = member + jnp.where(ch, 1.0, 0.0)
        prefix = dot(before, member.astype(BF16)) + count
        ranks = [jnp.sum(jnp.where(ch, prefix, 0.0), axis=-1, keepdims=True) for ch in chosen]
        count = count + jnp.sum(member, axis=0, keepdims=True)

        idx_ref[s, :] = _lane_pack(idxs, cr, jnp.int32)
        rank_ref[s, :] = _lane_pack(ranks, cr, jnp.int32)
        gate_ref[s, :] = _lane_pack(gates, cr, F32)
    cnt_scr[...] = count
    cnt_ref[...] = count


def _mix(x, proj3, y_sb, y_ret, kv, wsbo, wreto, wmix, wcq, wco, gx, gm, wr, br):
    B, S, _ = x.shape
    tm = min(512, S)
    M = kv.shape[1]
    gsb_b, gret_b = OFF_G_SB // D_MODEL, OFF_G_RET // D_MODEL
    tok = lambda b, i: (b, i, 0)
    const2 = lambda b, i: (0, 0)
    row_out = lambda w, dt: jax.ShapeDtypeStruct((B, S, w), dt)
    outs = pl.pallas_call(
        _mix_kernel,
        grid=(B, S // tm),
        in_specs=[
            pl.BlockSpec((None, tm, D_MODEL), tok),
            pl.BlockSpec((None, tm, SB_HEADS * SB_HEAD_DIM), tok),
            pl.BlockSpec((None, tm, D_MODEL), tok),
            pl.BlockSpec((None, tm, D_MODEL), lambda b, i: (b, i, gsb_b)),
            pl.BlockSpec((None, tm, D_MODEL), lambda b, i: (b, i, gret_b)),
            pl.BlockSpec((None, M, 2 * D_MODEL), lambda b, i: (b, 0, 0)),
            pl.BlockSpec((SB_HEADS * SB_HEAD_DIM, D_MODEL), const2),
            pl.BlockSpec((D_MODEL, D_MODEL), const2),
            pl.BlockSpec((D_MODEL, D_MODEL), const2),
            pl.BlockSpec((D_MODEL, D_MODEL), const2),
            pl.BlockSpec((D_MODEL, D_MODEL), const2),
            pl.BlockSpec((1, D_MODEL), const2),
            pl.BlockSpec((1, D_MODEL), const2),
            pl.BlockSpec((D_MODEL, 2 * N_EXPERTS), const2),
            pl.BlockSpec((1, N_EXPERTS), const2),
        ],
        out_specs=[
            pl.BlockSpec((None, tm, D_MODEL), tok),
            pl.BlockSpec((None, tm * ROW_TILES, LANES), tok),
            pl.BlockSpec((None, tm, LANES), tok),
            pl.BlockSpec((None, tm, LANES), tok),
            pl.BlockSpec((None, tm, LANES), tok),
            pl.BlockSpec((1, N_EXPERTS), const2),
        ],
        out_shape=[
            row_out(D_MODEL, F32), jax.ShapeDtypeStruct((B, S * ROW_TILES, LANES), F32),
            row_out(LANES, jnp.int32), row_out(LANES, F32), row_out(LANES, jnp.int32),
            jax.ShapeDtypeStruct((1, N_EXPERTS), F32),
        ],
        scratch_shapes=[pltpu.VMEM((1, N_EXPERTS), F32)],
        compiler_params=_params(("arbitrary", "arbitrary")),
        name="mix",
    )(x, y_sb, y_ret, proj3, proj3, kv, wsbo, wreto, wmix, wcq, wco, gx, gm, wr, br)
    return outs


def _row_copy(src, s, dst, d, sem):
    return pltpu.make_async_copy(src.at[pl.ds(s, 1)], dst.at[pl.ds(d, 1)], sem)


ROW_TILES = D_MODEL // LANES


def _token_tile_copy(src, s, dst, d, sem):
    rows = lambda i: pl.ds(pl.multiple_of(i * ROW_TILES, ROW_TILES), ROW_TILES)
    return pltpu.make_async_copy(src.at[rows(s)], dst.at[rows(d)], sem)


DMA_UNROLL = 16


ZERO_ROWS = 256
SUBLANES = 8


def _dispatch_kernel(lo_ref, hi_ref, pos_ref, h_ref, xin_ref, zbuf, sem, zsem):
    tm = h_ref.shape[0] // ROW_TILES

    def tokens(start, count):
        return pl.ds(pl.multiple_of(start * ROW_TILES, ROW_TILES), count * ROW_TILES)

    def zero_fill(act):
        def per_expert(e, carry):
            lo = lo_ref[e]
            hi = hi_ref[e]

            def chunk(i, off):
                act(pltpu.make_async_copy(zbuf, xin_ref.at[tokens(off, ZERO_ROWS)], zsem))
                return off + ZERO_ROWS

            off = lax.fori_loop(0, (hi - lo) // ZERO_ROWS, chunk, lo)
            rest = hi - off
            size = ZERO_ROWS // 2
            while size >= 1:
                @pl.when((rest & size) != 0)
                def _(off=off, size=size):
                    act(pltpu.make_async_copy(zbuf.at[tokens(0, size)], xin_ref.at[tokens(off, size)],
                                              zsem))
                off = off + (rest & size)
                size //= 2
            return carry

        lax.fori_loop(0, N_EXPERTS, per_expert, 0)

    @pl.when(pl.program_id(0) == 0)
    def _():
        zbuf[...] = jnp.zeros_like(zbuf)
        zero_fill(lambda cp: cp.start())
        zero_fill(lambda cp: cp.wait())

    def issue(t, carry):
        for k in range(TOP_K):
            _token_tile_copy(h_ref, t, xin_ref, pos_ref[0, 0, t * TOP_K + k],
                             sem).start(priority=k % 2)
        return carry

    lax.fori_loop(0, tm, issue, 0, unroll=DMA_UNROLL)
    for _ in range(TOP_K):
        pltpu.make_async_copy(h_ref, xin_ref.at[tokens(0, tm)], sem).wait()


def _dispatch(h3_tm, pos, pad_lo, pad_hi, n_rows):
    T = h3_tm.shape[0] // ROW_TILES
    tm = min(1024, T)
    pos3 = pos.reshape(T // tm, 1, tm * TOP_K)
    grid_spec = pltpu.PrefetchScalarGridSpec(
        num_scalar_prefetch=2,
        grid=(T // tm,),
        in_specs=[
            pl.BlockSpec((1, 1, tm * TOP_K), lambda i, lo, hi: (i, 0, 0), memory_space=pltpu.SMEM),
            pl.BlockSpec((tm * ROW_TILES, LANES), lambda i, lo, hi: (i, 0)),
        ],
        out_specs=pl.BlockSpec(memory_space=pl.ANY),
        scratch_shapes=[pltpu.VMEM((ZERO_ROWS * ROW_TILES, LANES), F32), pltpu.SemaphoreType.DMA(()),
                        pltpu.SemaphoreType.DMA(())],
    )
    return pl.pallas_call(
        _dispatch_kernel,
        grid_spec=grid_spec,
        out_shape=jax.ShapeDtypeStruct((n_rows * ROW_TILES, LANES), F32),
        compiler_params=_params(("arbitrary",)),
        name="dispatch",
    )(pad_lo, pad_hi, pos3, h3_tm)


def _ffn_kernel(first_ref, count_ref, nreal_ref, x_hbm, w1_ref, b1_ref, w2_ref, b2_ref, y_hbm,
                xbuf, ybuf, w1b, w2b, in_sem, out_sem):
    e = pl.program_id(0)
    tb = FFN_BLOCK
    ahead = FFN_SLOTS - 1
    n_real = nreal_ref[0]

    span = tb * ROW_TILES

    def rows(g):
        return pl.ds(pl.multiple_of(g * span, span), span)

    def x_copy(g):
        slot = g % FFN_SLOTS
        return pltpu.make_async_copy(x_hbm.at[rows(g)], xbuf.at[slot], in_sem.at[slot])

    def y_copy(g):
        slot = g % FFN_SLOTS
        return pltpu.make_async_copy(ybuf.at[slot], y_hbm.at[rows(g)], out_sem.at[slot])

    @pl.when(e == 0)
    def _():
        for g in range(ahead):
            @pl.when(g < n_real)
            def _(g=g):
                x_copy(g).start(priority=1)

    @pl.when(count_ref[e] > 0)
    def _():
        w1b[...] = w1_ref[...].astype(BF16)
        w2b[...] = w2_ref[...].astype(BF16)

    def block(j, carry):
        g = first_ref[e] + j
        slot = g % FFN_SLOTS
        x_copy(g).wait()

        @pl.when(g + ahead < n_real)
        def _():
            x_copy(g + ahead).start(priority=1)

        @pl.when(g >= FFN_SLOTS)
        def _():
            y_copy(g - FFN_SLOTS).wait()

        x = jnp.concatenate([xbuf[slot, pl.ds(c, tb, stride=ROW_TILES), :].astype(BF16)
                             for c in range(ROW_TILES)], axis=1)
        hc = jnp.dot(x, w1b[...], preferred_element_type=F32) + b1_ref[...]
        glu = jnp.minimum(hc[:, :D_FF], SWIGLU_LIMIT)
        lin = jnp.clip(hc[:, D_FF:], -SWIGLU_LIMIT, SWIGLU_LIMIT)
        act = glu * jax.nn.sigmoid(SWIGLU_ALPHA * glu) * (lin + 1.0)
        y = jnp.dot(act.astype(BF16), w2b[...], preferred_element_type=F32) + b2_ref[...]
        for c in range(ROW_TILES):
            ybuf[slot, pl.ds(c, tb, stride=ROW_TILES), :] = y[:, c * LANES:(c + 1) * LANES]
        y_copy(g).start(priority=1)
        return carry

    lax.fori_loop(0, count_ref[e], block, 0)

    @pl.when(e == N_EXPERTS - 1)
    def _():
        for back in range(FFN_SLOTS, 0, -1):
            @pl.when(n_real >= back)
            def _(back=back):
                y_copy(n_real - back).wait()

        for s in range(FFN_SLOTS):
            ybuf[s] = jnp.zeros((tb * ROW_TILES, LANES), F32)

        def tail(g, carry):
            cp = y_copy(g)
            cp.start()
            cp.wait()
            return carry

        lax.fori_loop(n_real, y_hbm.shape[0] // (tb * ROW_TILES), tail, 0)


def _ffn(xin, first_block, block_count, n_real, w1, b1, w2, b2):
    P = xin.shape[0]
    tb = FFN_BLOCK
    expert = lambda e, first, count, nr: (e, 0, 0)
    grid_spec = pltpu.PrefetchScalarGridSpec(
        num_scalar_prefetch=3,
        grid=(N_EXPERTS,),
        in_specs=[
            pl.BlockSpec(memory_space=pl.ANY),
            pl.BlockSpec((None, D_MODEL, 2 * D_FF), expert),
            pl.BlockSpec((None, 1, 2 * D_FF), expert),
            pl.BlockSpec((None, D_FF, D_MODEL), expert),
            pl.BlockSpec((None, 1, D_MODEL), expert),
        ],
        out_specs=pl.BlockSpec(memory_space=pl.ANY),
        scratch_shapes=[pltpu.VMEM((FFN_SLOTS, tb * ROW_TILES, LANES), F32),
                        pltpu.VMEM((FFN_SLOTS, tb * ROW_TILES, LANES), F32),
                        pltpu.VMEM((D_MODEL, 2 * D_FF), BF16), pltpu.VMEM((D_FF, D_MODEL), BF16),
                        pltpu.SemaphoreType.DMA((FFN_SLOTS,)),
                        pltpu.SemaphoreType.DMA((FFN_SLOTS,))],
    )
    return pl.pallas_call(
        _ffn_kernel,
        grid_spec=grid_spec,
        out_shape=jax.ShapeDtypeStruct(xin.shape, F32),
        compiler_params=_params(("arbitrary",)),
        name="ffn",
    )(first_block, block_count, n_real, xin, w1, b1.reshape(N_EXPERTS, 1, 2 * D_FF), w2,
      b2.reshape(N_EXPERTS, 1, D_MODEL))


def _combine_kernel(pos_ref, nxt_ref, x2_ref, gate_ref, g_ref, y_ref, o_ref, buf, sem):
    tm = x2_ref.shape[0]
    i = pl.program_id(0)
    slot = i % 2

    def gather(p_ref, s):
        def issue(t, carry):
            for k in range(TOP_K):
                _token_tile_copy(y_ref, p_ref[0, 0, t * TOP_K + k], buf.at[s, k], t,
                                 sem.at[s]).start(priority=k % 2)
            return carry
        lax.fori_loop(0, tm, issue, 0, unroll=DMA_UNROLL)

    @pl.when(i == 0)
    def _():
        gather(pos_ref, slot)

    @pl.when(i + 1 < pl.num_programs(0))
    def _():
        gather(nxt_ref, 1 - slot)

    for k in range(TOP_K):
        pltpu.make_async_copy(y_ref.at[pl.ds(0, tm * ROW_TILES)], buf.at[slot, k], sem.at[slot]).wait()

    gate = gate_ref[...]
    gates = [gate[:, k:k + 1] for k in range(TOP_K)]
    acc, ssq = [], jnp.zeros((tm, 1), F32)
    for c in range(ROW_TILES):
        a = x2_ref[:, c * LANES:(c + 1) * LANES]
        for k in range(TOP_K):
            a = a + gates[k] * buf[slot, k, pl.ds(c, tm, stride=ROW_TILES), :]
        acc.append(a)
        ssq = ssq + jnp.sum(a * a, axis=-1, keepdims=True)
    inv = lax.rsqrt(ssq * (1.0 / D_MODEL) + NORM_EPS)
    for c in range(ROW_TILES):
        o_ref[:, c * LANES:(c + 1) * LANES] = acc[c] * inv * g_ref[:, c * LANES:(c + 1) * LANES]


def _combine(x2, gate, pos, yrows, g_final):
    T = x2.shape[0]
    tm = min(256, T)
    n = T // tm
    pos3 = pos.reshape(n, 1, tm * TOP_K)
    return pl.pallas_call(
        _combine_kernel,
        grid=(n,),
        in_specs=[
            pl.BlockSpec((1, 1, tm * TOP_K), lambda i: (i, 0, 0), memory_space=pltpu.SMEM),
            pl.BlockSpec((1, 1, tm * TOP_K), lambda i: (jnp.minimum(i + 1, n - 1), 0, 0),
                         memory_space=pltpu.SMEM),
            pl.BlockSpec((tm, D_MODEL), lambda i: (i, 0)),
            pl.BlockSpec((tm, LANES), lambda i: (i, 0)),
            pl.BlockSpec((1, D_MODEL), lambda i: (0, 0)),
            pl.BlockSpec(memory_space=pl.ANY),
        ],
        out_specs=pl.BlockSpec((tm, D_MODEL), lambda i: (i, 0)),
        out_shape=jax.ShapeDtypeStruct((T, D_MODEL), F32),
        scratch_shapes=[pltpu.VMEM((2, TOP_K, tm * ROW_TILES, LANES), F32),
                        pltpu.SemaphoreType.DMA((2,))],
        compiler_params=_params(("arbitrary",)),
        name="combine",
    )(pos3, pos3, x2, gate, g_final, yrows)


def kernel(x, mem, g_mix, w_in, w_sb_o, w_ret_o, w_mix_out, g_xattn, g_mem, w_cq, w_ckv, w_co,
           g_moe, w_router, b_router, w_exp_in, b_exp_in, w_exp_out, b_exp_out, g_final):
    B, S, _ = x.shape
    T = B * S
    assert w_in.shape[0] == 1, "one layer"
    row = lambda v: v.reshape(1, -1)

    w_in_k = jnp.concatenate([w_in[0][:, o:o + w] for _, o, w in _IN_GROUPS], axis=1).astype(BF16)
    proj = _inproj(x.reshape(T, D_MODEL), row(g_mix[0]), w_in_k)
    proj3 = proj.reshape(B, S, IN_W)
    y_sb = _sb_attention(proj3)
    y_ret = _retention(proj3)
    kv = _memkv(mem, row(g_mem[0]), w_ckv[0].astype(BF16))
    wr_hi = w_router[0].astype(BF16)
    wr_lo = (w_router[0] - wr_hi.astype(F32)).astype(BF16)
    x2, h3, idx, gate, rank, cnt = _mix(
        x, proj3, y_sb, y_ret, kv,
        w_sb_o[0].astype(BF16), w_ret_o[0].astype(BF16), w_mix_out[0].astype(BF16),
        w_cq[0].astype(BF16), w_co[0].astype(BF16),
        row(g_xattn[0]), row(g_moe[0]), jnp.concatenate([wr_hi, wr_lo], axis=1), row(b_router[0]))

    tb = FFN_BLOCK
    n_blocks = (T * TOP_K + N_EXPERTS * (tb - 1) + tb - 1) // tb
    counts = cnt[0].astype(jnp.int32)
    padded = ((counts + tb - 1) // tb) * tb
    pad_end = jnp.cumsum(padded)
    pad_start = pad_end - padded
    idx4 = idx.reshape(T, LANES)[:, :TOP_K]
    start4 = jnp.sum(jnp.where(idx4[:, :, None] == jnp.arange(N_EXPERTS), pad_start, 0), axis=-1)
    pos = (start4 + rank.reshape(T, LANES)[:, :TOP_K]).astype(jnp.int32).reshape(T * TOP_K)
    n_real = (pad_end[-1:] // tb).astype(jnp.int32)
    first_block = (pad_start // tb).astype(jnp.int32)
    block_count = (padded // tb).astype(jnp.int32)

    pad_lo = (pad_start + counts).astype(jnp.int32)
    pad_hi = jnp.where(jnp.arange(N_EXPERTS) == N_EXPERTS - 1, n_blocks * tb, pad_end).astype(jnp.int32)
    xin = _dispatch(h3.reshape(T * ROW_TILES, LANES), pos, pad_lo, pad_hi, n_blocks * tb)
    yrows = _ffn(xin, first_block, block_count, n_real,
                 w_exp_in[0], b_exp_in[0], w_exp_out[0], b_exp_out[0])
    out = _combine(x2.reshape(T, D_MODEL), gate.reshape(T, LANES), pos, yrows, row(g_final))
    return out.reshape(B, S, D_MODEL)
```

```python
import functools

import numpy as np
import jax
import jax.numpy as jnp
from jax import lax
from jax.experimental import pallas as pl
from jax.experimental.pallas import tpu as pltpu

F32 = jnp.float32
BF16 = jnp.bfloat16

D_MODEL = 1024
SB_HEADS = 8
SB_HEAD_DIM = 64
RET_HEADS = 8
RET_QK_DIM = 64
RET_V_DIM = 128
ROPE_BASE = 10000.0
X_HEADS = 4
X_HEAD_DIM = 256
N_EXPERTS = 32
TOP_K = 4
D_FF = 1024
SWIGLU_LIMIT = 7.0
SWIGLU_ALPHA = 1.702
NORM_EPS = 1e-5

_IN_GROUPS = (("r_v", 2560, 1024), ("r_g", 3584, 1024), ("g_sb", 4608, 1024), ("g_ret", 5632, 1024),
              ("sb_q", 0, 512), ("sb_k", 512, 512), ("sb_v", 1024, 512),
              ("r_q", 1536, 512), ("r_k", 2048, 512))
OFF_R_V, OFF_R_G, OFF_G_SB, OFF_G_RET = 0, 1024, 2048, 3072
OFF_SB_Q, OFF_SB_K, OFF_SB_V, OFF_R_Q, OFF_R_K = 4096, 4608, 5120, 5632, 6144
IN_W = 6656

LANES = 128
VMEM_LIMIT = 56 * 1024 * 1024

SB_TILE = 128
SB_LOG_UNDERFLOW = -88.0
RET_CHUNK = 256
FFN_BLOCK = 256
FFN_SLOTS = 4

def _rms(x, g):
    return x * lax.rsqrt(jnp.mean(x * x, axis=-1, keepdims=True) + NORM_EPS) * g


def _params(sem):
    return pltpu.CompilerParams(dimension_semantics=sem, vmem_limit_bytes=VMEM_LIMIT)


INPROJ_COLS = 512


def _inproj_kernel(x_ref, g_ref, w_ref, o_ref):
    h = _rms(x_ref[...], g_ref[...]).astype(BF16)
    for j in range(IN_W // INPROJ_COLS):
        cols = slice(j * INPROJ_COLS, (j + 1) * INPROJ_COLS)
        o_ref[:, cols] = jnp.dot(h, w_ref[:, cols], preferred_element_type=F32).astype(o_ref.dtype)


def _inproj(x2d, g, w_bf16):
    T = x2d.shape[0]
    tm = min(512, T)
    return pl.pallas_call(
        _inproj_kernel,
        grid=(T // tm,),
        in_specs=[
            pl.BlockSpec((tm, D_MODEL), lambda i: (i, 0)),
            pl.BlockSpec((1, D_MODEL), lambda i: (0, 0)),
            pl.BlockSpec((D_MODEL, IN_W), lambda i: (0, 0)),
        ],
        out_specs=pl.BlockSpec((tm, IN_W), lambda i: (i, 0)),
        out_shape=jax.ShapeDtypeStruct((T, IN_W), BF16),
        compiler_params=_params(("parallel",)),
        name="inproj",
    )(x2d, g, w_bf16)


SB_PAIRS = SB_HEADS * SB_HEAD_DIM // LANES


def _sb_kernel(q_ref, k_ref, v_ref, o_ref, q2_ref, *state_refs):
    carry_refs, acc_refs = state_refs[:SB_PAIRS], state_refs[SB_PAIRS:]
    n = SB_TILE
    qi = pl.program_id(1)
    lane = lax.broadcasted_iota(jnp.int32, (n, LANES), 1)
    first = lane < SB_HEAD_DIM
    row = lax.broadcasted_iota(jnp.int32, (2 * n, n), 0) & (n - 1)
    col = lax.broadcasted_iota(jnp.int32, (2 * n, n), 1)
    strict = col < row
    jj = lax.broadcasted_iota(jnp.int32, (2 * n, 2 * n), 0) & (n - 1)
    cc = lax.broadcasted_iota(jnp.int32, (2 * n, 2 * n), 1)
    suffix = jnp.where((cc >= n) | (jj >= cc), 1.0, 0.0).astype(BF16)
    scale = SB_HEAD_DIM ** -0.5

    for p in range(SB_PAIRS):
        qs = (q_ref[:, p * LANES:(p + 1) * LANES].astype(F32) * scale).astype(BF16)
        zeros = jnp.zeros_like(qs)
        q2_ref[p * 2 * n:p * 2 * n + n, :] = jnp.where(first, qs, zeros)
        q2_ref[p * 2 * n + n:(p + 1) * 2 * n, :] = jnp.where(first, zeros, qs)

    def step(kt, diagonal):
        ks = pl.multiple_of(kt * n, n)
        zs, hls, css, ws, cmaxs = {}, {}, {}, {}, []

        def logits(p):
            kk = k_ref[pl.ds(ks, n), p * LANES:(p + 1) * LANES]
            z = lax.dot_general(q2_ref[p * 2 * n:(p + 1) * 2 * n, :], kk, (((1,), (1,)), ((), ())),
                                preferred_element_type=F32)
            sp = jnp.maximum(z, 0.0) + jnp.log(1.0 + jnp.exp(-jnp.abs(z)))
            if diagonal:
                sp = jnp.where(strict, sp, 0.0)
            hi = sp.astype(BF16)
            lo = (sp - hi.astype(F32)).astype(BF16)
            zs[p] = z
            hls[p] = jnp.concatenate([hi, lo], axis=1)

        def sums(p):
            css[p] = jnp.dot(hls[p], suffix, preferred_element_type=F32)

        def weights(p):
            logw = zs[p] - css[p][:, :n]
            if not diagonal:
                logw = logw + carry_refs[p][...]
            w = jnp.exp(logw)
            if diagonal:
                w = jnp.where(strict, w, 0.0)
            ws[p] = w.astype(BF16)

        def values(p):
            vv = v_ref[pl.ds(ks, n), p * LANES:(p + 1) * LANES]
            pv = jnp.dot(ws[p], vv, preferred_element_type=F32)
            acc_refs[p][...] = pv if diagonal else acc_refs[p][...] + pv
            carry = -css[p][:, n:] if diagonal else carry_refs[p][...] - css[p][:, n:]
            carry_refs[p][...] = carry
            cmaxs.append(jnp.max(carry))

        for stage in (logits, sums, weights, values):
            for p in range(SB_PAIRS):
                stage(p)
        return functools.reduce(jnp.maximum, cmaxs)

    def cond(st):
        kt, cmax = st
        return jnp.logical_and(kt >= 0, cmax > SB_LOG_UNDERFLOW)

    def body(st):
        kt, _ = st
        return kt - 1, step(kt, False)

    lax.while_loop(cond, body, (qi - 1, step(qi, True)))

    for p in range(SB_PAIRS):
        o_ref[:, p * LANES:(p + 1) * LANES] = jnp.where(
            first, acc_refs[p][:n, :], acc_refs[p][n:, :]
        ).astype(o_ref.dtype)


def _sb_attention(proj3):
    B, S, _ = proj3.shape
    n = SB_TILE
    w = SB_HEADS * SB_HEAD_DIM
    qb, kb, vb = OFF_SB_Q // w, OFF_SB_K // w, OFF_SB_V // w
    rows = SB_PAIRS * 2 * n
    return pl.pallas_call(
        _sb_kernel,
        grid=(B, S // n),
        in_specs=[
            pl.BlockSpec((None, n, w), lambda b, i: (b, i, qb)),
            pl.BlockSpec((None, S, w), lambda b, i: (b, 0, kb)),
            pl.BlockSpec((None, S, w), lambda b, i: (b, 0, vb)),
        ],
        out_specs=pl.BlockSpec((None, n, w), lambda b, i: (b, i, 0)),
        out_shape=jax.ShapeDtypeStruct((B, S, w), BF16),
        scratch_shapes=([pltpu.VMEM((rows, LANES), BF16)]
                        + [pltpu.VMEM((2 * n, LANES), F32)] * (2 * SB_PAIRS)),
        compiler_params=_params(("parallel", "arbitrary")),
        name="sb",
    )(proj3, proj3, proj3)


def _ret_tables(S):
    c = RET_CHUNK
    f = np.float32
    inv = f(ROPE_BASE) ** (-np.arange(0, RET_QK_DIM, 2, dtype=f) / f(RET_QK_DIM))
    ang = np.arange(S, dtype=f)[:, None] * inv[None, :]
    cos, sin = np.cos(ang), np.sin(ang)
    reps = LANES // RET_QK_DIM
    cos_t = np.tile(np.concatenate([cos, cos], axis=1), (1, reps))
    sin_t = np.tile(np.concatenate([-sin, sin], axis=1), (1, reps))
    log_g = np.log1p(-np.exp2(-5.0 - np.arange(RET_HEADS, dtype=np.float64)))
    pos = np.arange(c, dtype=np.float64)
    diff = pos[:, None] - pos[None, :]
    decay = np.where(diff >= 0, np.exp(log_g[:, None, None] * np.maximum(diff, 0.0)), 0.0).astype(f)
    q_dec = np.exp(log_g[:, None] * (pos + 1.0)[None, :]).astype(f)
    k_dec = np.exp(log_g[:, None] * (c - 1 - pos)[None, :]).astype(f)
    q_dec = np.ascontiguousarray(np.broadcast_to(q_dec[:, :, None], (RET_HEADS, c, LANES)))
    k_dec = np.ascontiguousarray(np.broadcast_to(k_dec[:, :, None], (RET_HEADS, c, LANES)))
    c_dec = np.ascontiguousarray(np.broadcast_to(
        np.exp(log_g * c).astype(f)[:, None, None], (RET_HEADS, LANES, LANES)))
    return cos_t, sin_t, decay, q_dec, k_dec, c_dec


def _ret_kernel(q_ref, k_ref, v_ref, g_ref, cos_ref, sin_ref, dm_ref, qd_ref, kd_ref, cd_ref,
                o_ref, state):
    c = RET_CHUNK

    @pl.when(pl.program_id(1) == 0)
    def _():
        state[...] = jnp.zeros_like(state)

    lane = lax.broadcasted_iota(jnp.int32, (c, LANES), 1)
    low_half = (lane & (RET_QK_DIM - 1)) < RET_QK_DIM // 2
    cos = cos_ref[...]
    sin = sin_ref[...]

    def rotary(x):
        swapped = jnp.where(low_half, pltpu.roll(x, LANES - RET_QK_DIM // 2, 1),
                            pltpu.roll(x, RET_QK_DIM // 2, 1))
        return x * cos + swapped * sin

    heads = range(RET_HEADS)
    cols = [slice(h * RET_V_DIM, (h + 1) * RET_V_DIM) for h in heads]
    rq, rk, qh, sc, out = {}, {}, {}, {}, {}

    for p in range(RET_HEADS * RET_QK_DIM // LANES):
        rq[p] = rotary(q_ref[:, p * LANES:(p + 1) * LANES].astype(F32))
        rk[p] = rotary(k_ref[:, p * LANES:(p + 1) * LANES].astype(F32)) * RET_QK_DIM ** -0.5
    for h in heads:
        in_head = (lane < RET_QK_DIM) if h % 2 == 0 else (lane >= RET_QK_DIM)
        qh[h] = jnp.where(in_head, rq[h // 2], 0.0)
        sc[h] = lax.dot_general(qh[h].astype(BF16), rk[h // 2].astype(BF16), (((1,), (1,)), ((), ())),
                                preferred_element_type=F32) * dm_ref[h]
    for h in heads:
        vh = v_ref[:, cols[h]]
        o = jnp.dot(sc[h].astype(BF16), vh, preferred_element_type=F32)
        out[h] = o + jnp.dot((qh[h] * qd_ref[h]).astype(BF16), state[h].astype(BF16),
                             preferred_element_type=F32)
        kv_new = lax.dot_general((rk[h // 2] * kd_ref[h]).astype(BF16), vh, (((0,), (0,)), ((), ())),
                                 preferred_element_type=F32)
        state[h] = state[h] * cd_ref[h] + kv_new
    for h in heads:
        o = out[h]
        mu = jnp.mean(o, axis=-1, keepdims=True)
        d = o - mu
        y = d * lax.rsqrt(jnp.mean(d * d, axis=-1, keepdims=True) + NORM_EPS)
        g = g_ref[:, cols[h]].astype(F32)
        o_ref[:, cols[h]] = (g * jax.nn.sigmoid(g) * y).astype(o_ref.dtype)


def _retention(proj3):
    B, S, _ = proj3.shape
    c = RET_CHUNK
    qkw, vw = RET_HEADS * RET_QK_DIM, RET_HEADS * RET_V_DIM
    qb, kb, vb, gb = OFF_R_Q // qkw, OFF_R_K // qkw, OFF_R_V // vw, OFF_R_G // vw
    cos_t, sin_t, decay, q_dec, k_dec, c_dec = _ret_tables(S)
    const3 = lambda b, i: (0, 0, 0)
    return pl.pallas_call(
        _ret_kernel,
        grid=(B, S // c),
        in_specs=[
            pl.BlockSpec((None, c, qkw), lambda b, i: (b, i, qb)),
            pl.BlockSpec((None, c, qkw), lambda b, i: (b, i, kb)),
            pl.BlockSpec((None, c, vw), lambda b, i: (b, i, vb)),
            pl.BlockSpec((None, c, vw), lambda b, i: (b, i, gb)),
            pl.BlockSpec((c, LANES), lambda b, i: (i, 0)),
            pl.BlockSpec((c, LANES), lambda b, i: (i, 0)),
            pl.BlockSpec((RET_HEADS, c, c), const3),
            pl.BlockSpec((RET_HEADS, c, LANES), const3),
            pl.BlockSpec((RET_HEADS, c, LANES), const3),
            pl.BlockSpec((RET_HEADS, LANES, LANES), const3),
        ],
        out_specs=pl.BlockSpec((None, c, vw), lambda b, i: (b, i, 0)),
        out_shape=jax.ShapeDtypeStruct((B, S, vw), BF16),
        scratch_shapes=[pltpu.VMEM((RET_HEADS, LANES, LANES), F32)],
        compiler_params=_params(("parallel", "arbitrary")),
        name="ret",
    )(proj3, proj3, proj3, proj3, cos_t, sin_t, decay, q_dec, k_dec, c_dec)


def _memkv_kernel(m_ref, g_ref, w_ref, o_ref):
    h = _rms(m_ref[...], g_ref[...]).astype(BF16)
    o_ref[...] = jnp.dot(h, w_ref[...], preferred_element_type=F32).astype(o_ref.dtype)


def _memkv(mem, g, w_bf16):
    B, M, _ = mem.shape
    return pl.pallas_call(
        _memkv_kernel,
        grid=(B,),
        in_specs=[
            pl.BlockSpec((None, M, D_MODEL), lambda b: (b, 0, 0)),
            pl.BlockSpec((1, D_MODEL), lambda b: (0, 0)),
            pl.BlockSpec((D_MODEL, 2 * D_MODEL), lambda b: (0, 0)),
        ],
        out_specs=pl.BlockSpec((None, M, 2 * D_MODEL), lambda b: (b, 0, 0)),
        out_shape=jax.ShapeDtypeStruct((B, M, 2 * D_MODEL), BF16),
        compiler_params=_params(("parallel",)),
        name="memkv",
    )(mem, g, w_bf16)


MIX_CHUNKS = 2


def _lane_pack(cols, rows, dtype):
    lane = lax.broadcasted_iota(jnp.int32, (rows, LANES), 1)
    out = jnp.zeros((rows, LANES), dtype)
    for k, c in enumerate(cols):
        out = jnp.where(lane == k, c.astype(dtype), out)
    return out


def _mix_kernel(x_ref, ysb_ref, yret_ref, gsb_ref, gret_ref, kv_ref,
                wsbo_ref, wreto_ref, wmix_ref, wcq_ref, wco_ref,
                gx_ref, gm_ref, wr_ref, br_ref,
                x2_ref, h3_ref, idx_ref, gate_ref, rank_ref, cnt_ref, cnt_scr):
    tm = x_ref.shape[0]
    cr = tm // MIX_CHUNKS
    chunks = [slice(c * cr, (c + 1) * cr) for c in range(MIX_CHUNKS)]
    dot = functools.partial(jnp.dot, preferred_element_type=F32)

    @pl.when(jnp.logical_and(pl.program_id(0) == 0, pl.program_id(1) == 0))
    def _():
        cnt_scr[...] = jnp.zeros_like(cnt_scr)

    def sigmoid(g):
        return (0.5 * jnp.tanh(0.5 * g) + 0.5).astype(F32)

    a = [dot(ysb_ref[s, :], wsbo_ref[...]) for s in chunks]
    b = [dot(yret_ref[s, :], wreto_ref[...]) for s in chunks]
    merged = [sigmoid(gsb_ref[s, :]) * ac + sigmoid(gret_ref[s, :]) * bc
              for s, ac, bc in zip(chunks, a, b)]
    x1 = [x_ref[s, :] + dot(m.astype(BF16), wmix_ref[...]) for s, m in zip(chunks, merged)]

    h2 = [_rms(v, gx_ref[...]).astype(BF16) for v in x1]
    q = [(dot(v, wcq_ref[...]) * X_HEAD_DIM ** -0.5).astype(BF16) for v in h2]
    heads = [[] for _ in chunks]
    for hh in range(X_HEADS):
        lo = hh * X_HEAD_DIM
        kh = kv_ref[:, lo:lo + X_HEAD_DIM]
        vh = kv_ref[:, D_MODEL + lo:D_MODEL + lo + X_HEAD_DIM]
        s = [lax.dot_general(qc[:, lo:lo + X_HEAD_DIM], kh, (((1,), (1,)), ((), ())),
                             preferred_element_type=F32) for qc in q]
        p = [jnp.exp(sc - jnp.max(sc, axis=-1, keepdims=True)) for sc in s]
        p = [pc * (1.0 / jnp.sum(pc, axis=-1, keepdims=True)) for pc in p]
        for c, pc in enumerate(p):
            heads[c].append(dot(pc.astype(BF16), vh).astype(BF16))
    o = [jnp.concatenate(hc, axis=1) for hc in heads]
    x2 = [v + dot(oc, wco_ref[...]) for v, oc in zip(x1, o)]
    for s, v in zip(chunks, x2):
        x2_ref[s, :] = v

    h3 = [_rms(v, gm_ref[...]) for v in x2]
    for ci, v in enumerate(h3):
        for c in range(ROW_TILES):
            h3_ref[pl.ds(ci * cr * ROW_TILES + c, cr, stride=ROW_TILES), :] = v[:, c * LANES:(c + 1) * LANES]
    logits = []
    for v in h3:
        hi = v.astype(BF16)
        lo = (v - hi.astype(F32)).astype(BF16)
        r = dot(jnp.concatenate([hi, lo], axis=0), wr_ref[...])
        logits.append(r[:cr, :N_EXPERTS] + r[:cr, N_EXPERTS:] + r[cr:, :N_EXPERTS] + br_ref[...])

    e_iota = lax.broadcasted_iota(jnp.int32, (cr, N_EXPERTS), 1).astype(F32)
    r_i = lax.broadcasted_iota(jnp.int32, (cr, cr), 0)
    c_i = lax.broadcasted_iota(jnp.int32, (cr, cr), 1)
    before = jnp.where(c_i < r_i, 1.0, 0.0).astype(BF16)
    count = cnt_scr[...]
    for s, rem in zip(chunks, logits):
        vals, idxs = [], []
        for _ in range(TOP_K):
            m = jnp.max(rem, axis=-1, keepdims=True)
            ik = jnp.min(jnp.where(rem == m, e_iota, float(N_EXPERTS)), axis=-1, keepdims=True)
            vals.append(m)
            idxs.append(ik)
            rem = jnp.where(e_iota == ik, -jnp.inf, rem)
        ex = [jnp.exp(v - vals[0]) for v in vals]
        inv_den = 1.0 / (ex[0] + ex[1] + ex[2] + ex[3])
        gates = [e * inv_den for e in ex]

        chosen = [(e_iota == ik) for ik in idxs]
        member = jnp.zeros((cr, N_EXPERTS), F32)
        for ch in chosen:
            member = member + jnp.where(ch, 1.0, 0.0)
        prefix = dot(before, member.astype(BF16)) + count
        ranks = [jnp.sum(jnp.where(ch, prefix, 0.0), axis=-1, keepdims=True) for ch in chosen]
        count = count + jnp.sum(member, axis=0, keepdims=True)

        idx_ref[s, :] = _lane_pack(idxs, cr, jnp.int32)
        rank_ref[s, :] = _lane_pack(ranks, cr, jnp.int32)
        gate_ref[s, :] = _lane_pack(gates, cr, F32)
    cnt_scr[...] = count
    cnt_ref[...] = count


def _mix(x, proj3, y_sb, y_ret, kv, wsbo, wreto, wmix, wcq, wco, gx, gm, wr, br):
    B, S, _ = x.shape
    tm = min(512, S)
    M = kv.shape[1]
    gsb_b, gret_b = OFF_G_SB // D_MODEL, OFF_G_RET // D_MODEL
    tok = lambda b, i: (b, i, 0)
    const2 = lambda b, i: (0, 0)
    row_out = lambda w, dt: jax.ShapeDtypeStruct((B, S, w), dt)
    outs = pl.pallas_call(
        _mix_kernel,
        grid=(B, S // tm),
        in_specs=[
            pl.BlockSpec((None, tm, D_MODEL), tok),
            pl.BlockSpec((None, tm, SB_HEADS * SB_HEAD_DIM), tok),
            pl.BlockSpec((None, tm, D_MODEL), tok),
            pl.BlockSpec((None, tm, D_MODEL), lambda b, i: (b, i, gsb_b)),
            pl.BlockSpec((None, tm, D_MODEL), lambda b, i: (b, i, gret_b)),
            pl.BlockSpec((None, M, 2 * D_MODEL), lambda b, i: (b, 0, 0)),
            pl.BlockSpec((SB_HEADS * SB_HEAD_DIM, D_MODEL), const2),
            pl.BlockSpec((D_MODEL, D_MODEL), const2),
            pl.BlockSpec((D_MODEL, D_MODEL), const2),
            pl.BlockSpec((D_MODEL, D_MODEL), const2),
            pl.BlockSpec((D_MODEL, D_MODEL), const2),
            pl.BlockSpec((1, D_MODEL), const2),
            pl.BlockSpec((1, D_MODEL), const2),
            pl.BlockSpec((D_MODEL, 2 * N_EXPERTS), const2),
            pl.BlockSpec((1, N_EXPERTS), const2),
        ],
        out_specs=[
            pl.BlockSpec((None, tm, D_MODEL), tok),
            pl.BlockSpec((None, tm * ROW_TILES, LANES), tok),
            pl.BlockSpec((None, tm, LANES), tok),
            pl.BlockSpec((None, tm, LANES), tok),
            pl.BlockSpec((None, tm, LANES), tok),
            pl.BlockSpec((1, N_EXPERTS), const2),
        ],
        out_shape=[
            row_out(D_MODEL, F32), jax.ShapeDtypeStruct((B, S * ROW_TILES, LANES), F32),
            row_out(LANES, jnp.int32), row_out(LANES, F32), row_out(LANES, jnp.int32),
            jax.ShapeDtypeStruct((1, N_EXPERTS), F32),
        ],
        scratch_shapes=[pltpu.VMEM((1, N_EXPERTS), F32)],
        compiler_params=_params(("arbitrary", "arbitrary")),
        name="mix",
    )(x, y_sb, y_ret, proj3, proj3, kv, wsbo, wreto, wmix, wcq, wco, gx, gm, wr, br)
    return outs


ROW_TILES = D_MODEL // LANES


def _token_tile_copy(src, s, dst, d, sem):
    rows = lambda i: pl.ds(pl.multiple_of(i * ROW_TILES, ROW_TILES), ROW_TILES)
    return pltpu.make_async_copy(src.at[rows(s)], dst.at[rows(d)], sem)


DMA_UNROLL = 16


ZERO_ROWS = 256


def _dispatch_kernel(lo_ref, hi_ref, pos_ref, h_ref, xin_ref, zbuf, sem, zsem):
    tm = h_ref.shape[0] // ROW_TILES

    def tokens(start, count):
        return pl.ds(pl.multiple_of(start * ROW_TILES, ROW_TILES), count * ROW_TILES)

    def zero_fill(act):
        def per_expert(e, carry):
            lo = lo_ref[e]
            hi = hi_ref[e]

            def chunk(i, off):
                act(pltpu.make_async_copy(zbuf, xin_ref.at[tokens(off, ZERO_ROWS)], zsem))
                return off + ZERO_ROWS

            off = lax.fori_loop(0, (hi - lo) // ZERO_ROWS, chunk, lo)
            rest = hi - off
            size = ZERO_ROWS // 2
            while size >= 1:
                @pl.when((rest & size) != 0)
                def _(off=off, size=size):
                    act(pltpu.make_async_copy(zbuf.at[tokens(0, size)], xin_ref.at[tokens(off, size)],
                                              zsem))
                off = off + (rest & size)
                size //= 2
            return carry

        lax.fori_loop(0, N_EXPERTS, per_expert, 0)

    @pl.when(pl.program_id(0) == 0)
    def _():
        zbuf[...] = jnp.zeros_like(zbuf)
        zero_fill(lambda cp: cp.start())
        zero_fill(lambda cp: cp.wait())

    parts, _, per_part = pos_ref.shape
    for part in range(parts):
        def issue(t, carry, part=part):
            for k in range(TOP_K):
                _token_tile_copy(h_ref, part * (per_part // TOP_K) + t, xin_ref,
                                 pos_ref[part, 0, t * TOP_K + k], sem).start(priority=k % 2)
            return carry

        lax.fori_loop(0, per_part // TOP_K, issue, 0, unroll=DMA_UNROLL)
    for _ in range(TOP_K):
        pltpu.make_async_copy(h_ref, xin_ref.at[tokens(0, tm)], sem).wait()


def _dispatch(h3_tm, pos3, pad_lo, pad_hi, n_rows):
    T = h3_tm.shape[0] // ROW_TILES
    per_tile = pos3.shape[2] // TOP_K
    tm = min(max(1024, per_tile), T)
    parts = tm // per_tile
    grid_spec = pltpu.PrefetchScalarGridSpec(
        num_scalar_prefetch=2,
        grid=(T // tm,),
        in_specs=[
            pl.BlockSpec((parts, 1, per_tile * TOP_K), lambda i, lo, hi: (i, 0, 0),
                         memory_space=pltpu.SMEM),
            pl.BlockSpec((tm * ROW_TILES, LANES), lambda i, lo, hi: (i, 0)),
        ],
        out_specs=pl.BlockSpec(memory_space=pl.ANY),
        scratch_shapes=[pltpu.VMEM((ZERO_ROWS * ROW_TILES, LANES), F32), pltpu.SemaphoreType.DMA(()),
                        pltpu.SemaphoreType.DMA(())],
    )
    return pl.pallas_call(
        _dispatch_kernel,
        grid_spec=grid_spec,
        out_shape=jax.ShapeDtypeStruct((n_rows * ROW_TILES, LANES), F32),
        compiler_params=_params(("arbitrary",)),
        name="dispatch",
    )(pad_lo, pad_hi, pos3, h3_tm)


def _ffn_kernel(first_ref, count_ref, nreal_ref, x_hbm, w1_ref, b1_ref, w2_ref, b2_ref, y_hbm,
                xbuf, ybuf, w1b, w2b, in_sem, out_sem):
    e = pl.program_id(0)
    tb = FFN_BLOCK
    ahead = FFN_SLOTS - 1
    n_real = nreal_ref[0]

    span = tb * ROW_TILES

    def rows(g):
        return pl.ds(pl.multiple_of(g * span, span), span)

    def x_copy(g):
        slot = g % FFN_SLOTS
        return pltpu.make_async_copy(x_hbm.at[rows(g)], xbuf.at[slot], in_sem.at[slot])

    def y_copy(g):
        slot = g % FFN_SLOTS
        return pltpu.make_async_copy(ybuf.at[slot], y_hbm.at[rows(g)], out_sem.at[slot])

    @pl.when(e == 0)
    def _():
        for g in range(ahead):
            @pl.when(g < n_real)
            def _(g=g):
                x_copy(g).start(priority=1)

    @pl.when(count_ref[e] > 0)
    def _():
        w1b[...] = w1_ref[...].astype(BF16)
        w2b[...] = w2_ref[...].astype(BF16)

    def block(j, carry):
        g = first_ref[e] + j
        slot = g % FFN_SLOTS
        x_copy(g).wait()

        @pl.when(g + ahead < n_real)
        def _():
            x_copy(g + ahead).start(priority=1)

        @pl.when(g >= FFN_SLOTS)
        def _():
            y_copy(g - FFN_SLOTS).wait()

        x = jnp.concatenate([xbuf[slot, pl.ds(c, tb, stride=ROW_TILES), :].astype(BF16)
                             for c in range(ROW_TILES)], axis=1)
        hc = jnp.dot(x, w1b[...], preferred_element_type=F32) + b1_ref[...]
        glu = jnp.minimum(hc[:, :D_FF], SWIGLU_LIMIT)
        lin = jnp.clip(hc[:, D_FF:], -SWIGLU_LIMIT, SWIGLU_LIMIT)
        act = glu * jax.nn.sigmoid(SWIGLU_ALPHA * glu) * (lin + 1.0)
        y = jnp.dot(act.astype(BF16), w2b[...], preferred_element_type=F32) + b2_ref[...]
        for c in range(ROW_TILES):
            ybuf[slot, pl.ds(c, tb, stride=ROW_TILES), :] = y[:, c * LANES:(c + 1) * LANES]
        y_copy(g).start(priority=1)
        return carry

    lax.fori_loop(0, count_ref[e], block, 0)

    @pl.when(e == N_EXPERTS - 1)
    def _():
        for back in range(FFN_SLOTS, 0, -1):
            @pl.when(n_real >= back)
            def _(back=back):
                y_copy(n_real - back).wait()

        for s in range(FFN_SLOTS):
            ybuf[s] = jnp.zeros((tb * ROW_TILES, LANES), F32)

        def tail(g, carry):
            cp = y_copy(g)
            cp.start()
            cp.wait()
            return carry

        lax.fori_loop(n_real, y_hbm.shape[0] // (tb * ROW_TILES), tail, 0)


def _ffn(xin, first_block, block_count, n_real, w1, b1, w2, b2):
    tb = FFN_BLOCK
    expert = lambda e, first, count, nr: (e, 0, 0)
    grid_spec = pltpu.PrefetchScalarGridSpec(
        num_scalar_prefetch=3,
        grid=(N_EXPERTS,),
        in_specs=[
            pl.BlockSpec(memory_space=pl.ANY),
            pl.BlockSpec((None, D_MODEL, 2 * D_FF), expert),
            pl.BlockSpec((None, 1, 2 * D_FF), expert),
            pl.BlockSpec((None, D_FF, D_MODEL), expert),
            pl.BlockSpec((None, 1, D_MODEL), expert),
        ],
        out_specs=pl.BlockSpec(memory_space=pl.ANY),
        scratch_shapes=[pltpu.VMEM((FFN_SLOTS, tb * ROW_TILES, LANES), F32),
                        pltpu.VMEM((FFN_SLOTS, tb * ROW_TILES, LANES), F32),
                        pltpu.VMEM((D_MODEL, 2 * D_FF), BF16), pltpu.VMEM((D_FF, D_MODEL), BF16),
                        pltpu.SemaphoreType.DMA((FFN_SLOTS,)),
                        pltpu.SemaphoreType.DMA((FFN_SLOTS,))],
    )
    return pl.pallas_call(
        _ffn_kernel,
        grid_spec=grid_spec,
        out_shape=jax.ShapeDtypeStruct(xin.shape, F32),
        compiler_params=_params(("arbitrary",)),
        name="ffn",
    )(first_block, block_count, n_real, xin, w1, b1.reshape(N_EXPERTS, 1, 2 * D_FF), w2,
      b2.reshape(N_EXPERTS, 1, D_MODEL))


COMBINE_PARTS = 4
COMBINE_AHEAD = 2


def _combine_kernel(pos_ref, nxt_ref, x2_ref, gate_ref, g_ref, y_ref, o_ref, *scratch):
    bufs, sem = scratch[:COMBINE_PARTS], scratch[COMBINE_PARTS]
    tm = x2_ref.shape[0]
    sub = tm // COMBINE_PARTS
    i = pl.program_id(0)
    last = i + 1 == pl.num_programs(0)

    def request(j):
        p_ref, part = (pos_ref, j) if j < COMBINE_PARTS else (nxt_ref, j - COMBINE_PARTS)
        for t in range(sub):
            for k in range(TOP_K):
                _token_tile_copy(y_ref, p_ref[0, 0, (part * sub + t) * TOP_K + k], bufs[part].at[k], t,
                                 sem.at[part]).start(priority=k % 2)

    def wait(j):
        for k in range(TOP_K):
            pltpu.make_async_copy(y_ref.at[pl.ds(0, sub * ROW_TILES)], bufs[j].at[k], sem.at[j]).wait()

    def reduce(j):
        rows = slice(j * sub, (j + 1) * sub)
        gate = gate_ref[rows, :]
        gates = [gate[:, k:k + 1] for k in range(TOP_K)]
        acc, ssq = [], jnp.zeros((sub, 1), F32)
        for c in range(ROW_TILES):
            a = x2_ref[rows, c * LANES:(c + 1) * LANES]
            for k in range(TOP_K):
                a = a + gates[k] * bufs[j][k, pl.ds(c, sub, stride=ROW_TILES), :]
            acc.append(a)
            ssq = ssq + jnp.sum(a * a, axis=-1, keepdims=True)
        inv = lax.rsqrt(ssq * (1.0 / D_MODEL) + NORM_EPS)
        for c in range(ROW_TILES):
            o_ref[rows, c * LANES:(c + 1) * LANES] = acc[c] * inv * g_ref[:, c * LANES:(c + 1) * LANES]

    @pl.when(i == 0)
    def _():
        for j in range(COMBINE_AHEAD):
            request(j)

    for j in range(COMBINE_PARTS):
        wait(j)
        if j + COMBINE_AHEAD < COMBINE_PARTS:
            request(j + COMBINE_AHEAD)
            reduce(j)
        else:
            @pl.when(jnp.logical_not(last))
            def _(j=j):
                request(j + COMBINE_AHEAD)
                reduce(j)

            @pl.when(last)
            def _(j=j):
                reduce(j)


COMBINE_TILE = 512


def _combine(x2, gate, pos3, yrows, g_final):
    T = x2.shape[0]
    n, _, per_tile = pos3.shape
    tm = per_tile // TOP_K
    return pl.pallas_call(
        _combine_kernel,
        grid=(n,),
        in_specs=[
            pl.BlockSpec((1, 1, tm * TOP_K), lambda i: (i, 0, 0), memory_space=pltpu.SMEM),
            pl.BlockSpec((1, 1, tm * TOP_K), lambda i: (jnp.minimum(i + 1, n - 1), 0, 0),
                         memory_space=pltpu.SMEM),
            pl.BlockSpec((tm, D_MODEL), lambda i: (i, 0)),
            pl.BlockSpec((tm, LANES), lambda i: (i, 0)),
            pl.BlockSpec((1, D_MODEL), lambda i: (0, 0)),
            pl.BlockSpec(memory_space=pl.ANY),
        ],
        out_specs=pl.BlockSpec((tm, D_MODEL), lambda i: (i, 0)),
        out_shape=jax.ShapeDtypeStruct((T, D_MODEL), F32),
        scratch_shapes=([pltpu.VMEM((TOP_K, tm // COMBINE_PARTS * ROW_TILES, LANES), F32)] * COMBINE_PARTS
                        + [pltpu.SemaphoreType.DMA((COMBINE_PARTS,))]),
        compiler_params=_params(("arbitrary",)),
        name="combine",
    )(pos3, pos3, x2, gate, g_final, yrows)


def kernel(x, mem, g_mix, w_in, w_sb_o, w_ret_o, w_mix_out, g_xattn, g_mem, w_cq, w_ckv, w_co,
           g_moe, w_router, b_router, w_exp_in, b_exp_in, w_exp_out, b_exp_out, g_final):
    B, S, _ = x.shape
    T = B * S
    assert w_in.shape[0] == 1, "one layer"
    row = lambda v: v.reshape(1, -1)

    w_in_k = jnp.concatenate([w_in[0][:, o:o + w] for _, o, w in _IN_GROUPS], axis=1).astype(BF16)
    proj = _inproj(x.reshape(T, D_MODEL), row(g_mix[0]), w_in_k)
    proj3 = proj.reshape(B, S, IN_W)
    y_sb = _sb_attention(proj3)
    y_ret = _retention(proj3)
    kv = _memkv(mem, row(g_mem[0]), w_ckv[0].astype(BF16))
    wr_hi = w_router[0].astype(BF16)
    wr_lo = (w_router[0] - wr_hi.astype(F32)).astype(BF16)
    x2, h3, idx, gate, rank, cnt = _mix(
        x, proj3, y_sb, y_ret, kv,
        w_sb_o[0].astype(BF16), w_ret_o[0].astype(BF16), w_mix_out[0].astype(BF16),
        w_cq[0].astype(BF16), w_co[0].astype(BF16),
        row(g_xattn[0]), row(g_moe[0]), jnp.concatenate([wr_hi, wr_lo], axis=1), row(b_router[0]))

    tb = FFN_BLOCK
    n_blocks = (T * TOP_K + N_EXPERTS * (tb - 1) + tb - 1) // tb
    counts = cnt[0].astype(jnp.int32)
    padded = ((counts + tb - 1) // tb) * tb
    pad_end = jnp.cumsum(padded)
    pad_start = pad_end - padded
    idx4 = idx.reshape(T, LANES)[:, :TOP_K]
    start4 = jnp.sum(jnp.where(idx4[:, :, None] == jnp.arange(N_EXPERTS), pad_start, 0), axis=-1)
    pos = (start4 + rank.reshape(T, LANES)[:, :TOP_K]).astype(jnp.int32).reshape(T * TOP_K)
    n_real = (pad_end[-1:] // tb).astype(jnp.int32)
    first_block = (pad_start // tb).astype(jnp.int32)
    block_count = (padded // tb).astype(jnp.int32)

    pad_lo = (pad_start + counts).astype(jnp.int32)
    pad_hi = jnp.where(jnp.arange(N_EXPERTS) == N_EXPERTS - 1, n_blocks * tb, pad_end).astype(jnp.int32)
    tile = min(COMBINE_TILE, T)
    pos3 = pos.reshape(T // tile, 1, tile * TOP_K)
    xin = _dispatch(h3.reshape(T * ROW_TILES, LANES), pos3, pad_lo, pad_hi, n_blocks * tb)
    yrows = _ffn(xin, first_block, block_count, n_real,
                 w_exp_in[0], b_exp_in[0], w_exp_out[0], b_exp_out[0])
    out = _combine(x2.reshape(T, D_MODEL), gate.reshape(T, LANES), pos3, yrows, row(g_final))
    return out.reshape(B, S, D_MODEL)
```

```python
import functools

import numpy as np
import jax
import jax.numpy as jnp
from jax import lax
from jax.experimental import pallas as pl
from jax.experimental.pallas import tpu as pltpu

F32 = jnp.float32
BF16 = jnp.bfloat16

D_MODEL = 1024
SB_HEADS = 8
SB_HEAD_DIM = 64
RET_HEADS = 8
RET_QK_DIM = 64
RET_V_DIM = 128
ROPE_BASE = 10000.0
X_HEADS = 4
X_HEAD_DIM = 256
N_EXPERTS = 32
TOP_K = 4
D_FF = 1024
SWIGLU_LIMIT = 7.0
SWIGLU_ALPHA = 1.702
NORM_EPS = 1e-5

_IN_GROUPS = (("r_v", 2560, 1024), ("r_g", 3584, 1024), ("g_sb", 4608, 1024), ("g_ret", 5632, 1024),
              ("sb_q", 0, 512), ("sb_k", 512, 512), ("sb_v", 1024, 512),
              ("r_q", 1536, 512), ("r_k", 2048, 512))
OFF_R_V, OFF_R_G, OFF_G_SB, OFF_G_RET = 0, 1024, 2048, 3072
OFF_SB_Q, OFF_SB_K, OFF_SB_V, OFF_R_Q, OFF_R_K = 4096, 4608, 5120, 5632, 6144
IN_W = 6656

LANES = 128
VMEM_LIMIT = 56 * 1024 * 1024

SB_TILE = 128
SB_LOG_UNDERFLOW = -88.0
RET_CHUNK = 256
FFN_BLOCK = 256
FFN_GROUP = 2
FFN_SLOTS = 6

def _rms(x, g):
    return x * lax.rsqrt(jnp.mean(x * x, axis=-1, keepdims=True) + NORM_EPS) * g


def _params(sem):
    return pltpu.CompilerParams(dimension_semantics=sem, vmem_limit_bytes=VMEM_LIMIT)


INPROJ_COLS = 512


def _inproj_kernel(x_ref, g_ref, w_ref, o_ref):
    h = _rms(x_ref[...], g_ref[...]).astype(BF16)
    for j in range(IN_W // INPROJ_COLS):
        cols = slice(j * INPROJ_COLS, (j + 1) * INPROJ_COLS)
        o_ref[:, cols] = jnp.dot(h, w_ref[:, cols], preferred_element_type=F32).astype(o_ref.dtype)


def _inproj(x2d, g, w_bf16):
    T = x2d.shape[0]
    tm = min(512, T)
    return pl.pallas_call(
        _inproj_kernel,
        grid=(T // tm,),
        in_specs=[
            pl.BlockSpec((tm, D_MODEL), lambda i: (i, 0)),
            pl.BlockSpec((1, D_MODEL), lambda i: (0, 0)),
            pl.BlockSpec((D_MODEL, IN_W), lambda i: (0, 0)),
        ],
        out_specs=pl.BlockSpec((tm, IN_W), lambda i: (i, 0)),
        out_shape=jax.ShapeDtypeStruct((T, IN_W), BF16),
        compiler_params=_params(("parallel",)),
        name="inproj",
    )(x2d, g, w_bf16)


SB_PAIRS = SB_HEADS * SB_HEAD_DIM // LANES


def _sb_kernel(q_ref, k_ref, v_ref, o_ref, q2_ref, *state_refs):
    carry_refs, acc_refs = state_refs[:SB_PAIRS], state_refs[SB_PAIRS:]
    n = SB_TILE
    qi = pl.program_id(1)
    lane = lax.broadcasted_iota(jnp.int32, (n, LANES), 1)
    first = lane < SB_HEAD_DIM
    row = lax.broadcasted_iota(jnp.int32, (2 * n, n), 0) & (n - 1)
    col = lax.broadcasted_iota(jnp.int32, (2 * n, n), 1)
    strict = col < row
    jj = lax.broadcasted_iota(jnp.int32, (2 * n, 2 * n), 0) & (n - 1)
    cc = lax.broadcasted_iota(jnp.int32, (2 * n, 2 * n), 1)
    suffix = jnp.where((cc >= n) | (jj >= cc), 1.0, 0.0).astype(BF16)
    scale = SB_HEAD_DIM ** -0.5

    for p in range(SB_PAIRS):
        qs = (q_ref[:, p * LANES:(p + 1) * LANES].astype(F32) * scale).astype(BF16)
        zeros = jnp.zeros_like(qs)
        q2_ref[p * 2 * n:p * 2 * n + n, :] = jnp.where(first, qs, zeros)
        q2_ref[p * 2 * n + n:(p + 1) * 2 * n, :] = jnp.where(first, zeros, qs)

    def step(kt, diagonal):
        ks = pl.multiple_of(kt * n, n)
        zs, hls, css, ws, cmaxs = {}, {}, {}, {}, []

        def logits(p):
            kk = k_ref[pl.ds(ks, n), p * LANES:(p + 1) * LANES]
            z = lax.dot_general(q2_ref[p * 2 * n:(p + 1) * 2 * n, :], kk, (((1,), (1,)), ((), ())),
                                preferred_element_type=F32)
            sp = jnp.maximum(z, 0.0) + jnp.log(1.0 + jnp.exp(-jnp.abs(z)))
            if diagonal:
                sp = jnp.where(strict, sp, 0.0)
            hi = sp.astype(BF16)
            lo = (sp - hi.astype(F32)).astype(BF16)
            zs[p] = z
            hls[p] = jnp.concatenate([hi, lo], axis=1)

        def sums(p):
            css[p] = jnp.dot(hls[p], suffix, preferred_element_type=F32)

        def weights(p):
            logw = zs[p] - css[p][:, :n]
            if not diagonal:
                logw = logw + carry_refs[p][...]
            w = jnp.exp(logw)
            if diagonal:
                w = jnp.where(strict, w, 0.0)
            ws[p] = w.astype(BF16)

        def values(p):
            vv = v_ref[pl.ds(ks, n), p * LANES:(p + 1) * LANES]
            pv = jnp.dot(ws[p], vv, preferred_element_type=F32)
            acc_refs[p][...] = pv if diagonal else acc_refs[p][...] + pv
            carry = -css[p][:, n:] if diagonal else carry_refs[p][...] - css[p][:, n:]
            carry_refs[p][...] = carry
            cmaxs.append(jnp.max(carry))

        for stage in (logits, sums, weights, values):
            for p in range(SB_PAIRS):
                stage(p)
        return functools.reduce(jnp.maximum, cmaxs)

    def cond(st):
        kt, cmax = st
        return jnp.logical_and(kt >= 0, cmax > SB_LOG_UNDERFLOW)

    def body(st):
        kt, _ = st
        return kt - 1, step(kt, False)

    lax.while_loop(cond, body, (qi - 1, step(qi, True)))

    for p in range(SB_PAIRS):
        o_ref[:, p * LANES:(p + 1) * LANES] = jnp.where(
            first, acc_refs[p][:n, :], acc_refs[p][n:, :]
        ).astype(o_ref.dtype)


def _sb_attention(proj3):
    B, S, _ = proj3.shape
    n = SB_TILE
    w = SB_HEADS * SB_HEAD_DIM
    qb, kb, vb = OFF_SB_Q // w, OFF_SB_K // w, OFF_SB_V // w
    rows = SB_PAIRS * 2 * n
    return pl.pallas_call(
        _sb_kernel,
        grid=(B, S // n),
        in_specs=[
            pl.BlockSpec((None, n, w), lambda b, i: (b, i, qb)),
            pl.BlockSpec((None, S, w), lambda b, i: (b, 0, kb)),
            pl.BlockSpec((None, S, w), lambda b, i: (b, 0, vb)),
        ],
        out_specs=pl.BlockSpec((None, n, w), lambda b, i: (b, i, 0)),
        out_shape=jax.ShapeDtypeStruct((B, S, w), BF16),
        scratch_shapes=([pltpu.VMEM((rows, LANES), BF16)]
                        + [pltpu.VMEM((2 * n, LANES), F32)] * (2 * SB_PAIRS)),
        compiler_params=_params(("parallel", "arbitrary")),
        name="sb",
    )(proj3, proj3, proj3)


def _ret_tables(S):
    c = RET_CHUNK
    f = np.float32
    inv = f(ROPE_BASE) ** (-np.arange(0, RET_QK_DIM, 2, dtype=f) / f(RET_QK_DIM))
    ang = np.arange(S, dtype=f)[:, None] * inv[None, :]
    cos, sin = np.cos(ang), np.sin(ang)
    reps = LANES // RET_QK_DIM
    cos_t = np.tile(np.concatenate([cos, cos], axis=1), (1, reps))
    sin_t = np.tile(np.concatenate([-sin, sin], axis=1), (1, reps))
    log_g = np.log1p(-np.exp2(-5.0 - np.arange(RET_HEADS, dtype=np.float64)))
    pos = np.arange(c, dtype=np.float64)
    diff = pos[:, None] - pos[None, :]
    decay = np.where(diff >= 0, np.exp(log_g[:, None, None] * np.maximum(diff, 0.0)), 0.0).astype(f)
    q_dec = np.exp(log_g[:, None] * (pos + 1.0)[None, :]).astype(f)
    k_dec = np.exp(log_g[:, None] * (c - 1 - pos)[None, :]).astype(f)
    q_dec = np.ascontiguousarray(np.broadcast_to(q_dec[:, :, None], (RET_HEADS, c, LANES)))
    k_dec = np.ascontiguousarray(np.broadcast_to(k_dec[:, :, None], (RET_HEADS, c, LANES)))
    c_dec = np.ascontiguousarray(np.broadcast_to(
        np.exp(log_g * c).astype(f)[:, None, None], (RET_HEADS, LANES, LANES)))
    return cos_t, sin_t, decay, q_dec, k_dec, c_dec


def _ret_kernel(q_ref, k_ref, v_ref, g_ref, cos_ref, sin_ref, dm_ref, qd_ref, kd_ref, cd_ref,
                o_ref, state):
    c = RET_CHUNK

    @pl.when(pl.program_id(1) == 0)
    def _():
        state[...] = jnp.zeros_like(state)

    lane = lax.broadcasted_iota(jnp.int32, (c, LANES), 1)
    low_half = (lane & (RET_QK_DIM - 1)) < RET_QK_DIM // 2
    cos = cos_ref[...]
    sin = sin_ref[...]

    def rotary(x):
        swapped = jnp.where(low_half, pltpu.roll(x, LANES - RET_QK_DIM // 2, 1),
                            pltpu.roll(x, RET_QK_DIM // 2, 1))
        return x * cos + swapped * sin

    heads = range(RET_HEADS)
    cols = [slice(h * RET_V_DIM, (h + 1) * RET_V_DIM) for h in heads]
    rq, rk, qh, sc, out = {}, {}, {}, {}, {}

    for p in range(RET_HEADS * RET_QK_DIM // LANES):
        rq[p] = rotary(q_ref[:, p * LANES:(p + 1) * LANES].astype(F32))
        rk[p] = rotary(k_ref[:, p * LANES:(p + 1) * LANES].astype(F32)) * RET_QK_DIM ** -0.5
    for h in heads:
        in_head = (lane < RET_QK_DIM) if h % 2 == 0 else (lane >= RET_QK_DIM)
        qh[h] = jnp.where(in_head, rq[h // 2], 0.0)
        sc[h] = lax.dot_general(qh[h].astype(BF16), rk[h // 2].astype(BF16), (((1,), (1,)), ((), ())),
                                preferred_element_type=F32) * dm_ref[h]
    for h in heads:
        vh = v_ref[:, cols[h]]
        o = jnp.dot(sc[h].astype(BF16), vh, preferred_element_type=F32)
        out[h] = o + jnp.dot((qh[h] * qd_ref[h]).astype(BF16), state[h].astype(BF16),
                             preferred_element_type=F32)
        kv_new = lax.dot_general((rk[h // 2] * kd_ref[h]).astype(BF16), vh, (((0,), (0,)), ((), ())),
                                 preferred_element_type=F32)
        state[h] = state[h] * cd_ref[h] + kv_new
    for h in heads:
        o = out[h]
        mu = jnp.mean(o, axis=-1, keepdims=True)
        d = o - mu
        y = d * lax.rsqrt(jnp.mean(d * d, axis=-1, keepdims=True) + NORM_EPS)
        g = g_ref[:, cols[h]].astype(F32)
        o_ref[:, cols[h]] = (g * jax.nn.sigmoid(g) * y).astype(o_ref.dtype)


def _retention(proj3):
    B, S, _ = proj3.shape
    c = RET_CHUNK
    qkw, vw = RET_HEADS * RET_QK_DIM, RET_HEADS * RET_V_DIM
    qb, kb, vb, gb = OFF_R_Q // qkw, OFF_R_K // qkw, OFF_R_V // vw, OFF_R_G // vw
    cos_t, sin_t, decay, q_dec, k_dec, c_dec = _ret_tables(S)
    const3 = lambda b, i: (0, 0, 0)
    return pl.pallas_call(
        _ret_kernel,
        grid=(B, S // c),
        in_specs=[
            pl.BlockSpec((None, c, qkw), lambda b, i: (b, i, qb)),
            pl.BlockSpec((None, c, qkw), lambda b, i: (b, i, kb)),
            pl.BlockSpec((None, c, vw), lambda b, i: (b, i, vb)),
            pl.BlockSpec((None, c, vw), lambda b, i: (b, i, gb)),
            pl.BlockSpec((c, LANES), lambda b, i: (i, 0)),
            pl.BlockSpec((c, LANES), lambda b, i: (i, 0)),
            pl.BlockSpec((RET_HEADS, c, c), const3),
            pl.BlockSpec((RET_HEADS, c, LANES), const3),
            pl.BlockSpec((RET_HEADS, c, LANES), const3),
            pl.BlockSpec((RET_HEADS, LANES, LANES), const3),
        ],
        out_specs=pl.BlockSpec((None, c, vw), lambda b, i: (b, i, 0)),
        out_shape=jax.ShapeDtypeStruct((B, S, vw), BF16),
        scratch_shapes=[pltpu.VMEM((RET_HEADS, LANES, LANES), F32)],
        compiler_params=_params(("parallel", "arbitrary")),
        name="ret",
    )(proj3, proj3, proj3, proj3, cos_t, sin_t, decay, q_dec, k_dec, c_dec)


def _memkv_kernel(m_ref, g_ref, w_ref, o_ref):
    h = _rms(m_ref[...], g_ref[...]).astype(BF16)
    o_ref[...] = jnp.dot(h, w_ref[...], preferred_element_type=F32).astype(o_ref.dtype)


def _memkv(mem, g, w_bf16):
    B, M, _ = mem.shape
    return pl.pallas_call(
        _memkv_kernel,
        grid=(B,),
        in_specs=[
            pl.BlockSpec((None, M, D_MODEL), lambda b: (b, 0, 0)),
            pl.BlockSpec((1, D_MODEL), lambda b: (0, 0)),
            pl.BlockSpec((D_MODEL, 2 * D_MODEL), lambda b: (0, 0)),
        ],
        out_specs=pl.BlockSpec((None, M, 2 * D_MODEL), lambda b: (b, 0, 0)),
        out_shape=jax.ShapeDtypeStruct((B, M, 2 * D_MODEL), BF16),
        compiler_params=_params(("parallel",)),
        name="memkv",
    )(mem, g, w_bf16)


MIX_CHUNKS = 2


def _lane_pack(cols, rows, dtype):
    lane = lax.broadcasted_iota(jnp.int32, (rows, LANES), 1)
    out = jnp.zeros((rows, LANES), dtype)
    for k, c in enumerate(cols):
        out = jnp.where(lane == k, c.astype(dtype), out)
    return out


def _mix_kernel(x_ref, ysb_ref, yret_ref, gsb_ref, gret_ref, kv_ref,
                wsbo_ref, wreto_ref, wmix_ref, wcq_ref, wco_ref,
                gx_ref, gm_ref, wr_ref, br_ref,
                x2_ref, h3_ref, idx_ref, gate_ref, rank_ref, cnt_ref, cnt_scr):
    tm = x_ref.shape[0]
    cr = tm // MIX_CHUNKS
    chunks = [slice(c * cr, (c + 1) * cr) for c in range(MIX_CHUNKS)]
    dot = functools.partial(jnp.dot, preferred_element_type=F32)

    @pl.when(jnp.logical_and(pl.program_id(0) == 0, pl.program_id(1) == 0))
    def _():
        cnt_scr[...] = jnp.zeros_like(cnt_scr)

    def sigmoid(g):
        return (0.5 * jnp.tanh(0.5 * g) + 0.5).astype(F32)

    a = [dot(ysb_ref[s, :], wsbo_ref[...]) for s in chunks]
    b = [dot(yret_ref[s, :], wreto_ref[...]) for s in chunks]
    merged = [sigmoid(gsb_ref[s, :]) * ac + sigmoid(gret_ref[s, :]) * bc
              for s, ac, bc in zip(chunks, a, b)]
    x1 = [x_ref[s, :] + dot(m.astype(BF16), wmix_ref[...]) for s, m in zip(chunks, merged)]

    h2 = [_rms(v, gx_ref[...]).astype(BF16) for v in x1]
    q = [(dot(v, wcq_ref[...]) * X_HEAD_DIM ** -0.5).astype(BF16) for v in h2]
    heads = [[] for _ in chunks]
    for hh in range(X_HEADS):
        lo = hh * X_HEAD_DIM
        kh = kv_ref[:, lo:lo + X_HEAD_DIM]
        vh = kv_ref[:, D_MODEL + lo:D_MODEL + lo + X_HEAD_DIM]
        s = [lax.dot_general(qc[:, lo:lo + X_HEAD_DIM], kh, (((1,), (1,)), ((), ())),
                             preferred_element_type=F32) for qc in q]
        p = [jnp.exp(sc - jnp.max(sc, axis=-1, keepdims=True)) for sc in s]
        p = [pc * (1.0 / jnp.sum(pc, axis=-1, keepdims=True)) for pc in p]
        for c, pc in enumerate(p):
            heads[c].append(dot(pc.astype(BF16), vh).astype(BF16))
    o = [jnp.concatenate(hc, axis=1) for hc in heads]
    x2 = [v + dot(oc, wco_ref[...]) for v, oc in zip(x1, o)]
    for s, v in zip(chunks, x2):
        x2_ref[s, :] = v

    h3 = [_rms(v, gm_ref[...]) for v in x2]
    for ci, v in enumerate(h3):
        for c in range(ROW_TILES):
            h3_ref[pl.ds(ci * cr * ROW_TILES + c, cr, stride=ROW_TILES), :] = v[:, c * LANES:(c + 1) * LANES]
    logits = []
    for v in h3:
        hi = v.astype(BF16)
        lo = (v - hi.astype(F32)).astype(BF16)
        r = dot(jnp.concatenate([hi, lo], axis=0), wr_ref[...])
        logits.append(r[:cr, :N_EXPERTS] + r[:cr, N_EXPERTS:] + r[cr:, :N_EXPERTS] + br_ref[...])

    e_iota = lax.broadcasted_iota(jnp.int32, (cr, N_EXPERTS), 1).astype(F32)
    r_i = lax.broadcasted_iota(jnp.int32, (cr, cr), 0)
    c_i = lax.broadcasted_iota(jnp.int32, (cr, cr), 1)
    before = jnp.where(c_i < r_i, 1.0, 0.0).astype(BF16)
    count = cnt_scr[...]
    for s, rem in zip(chunks, logits):
        vals, idxs = [], []
        for _ in range(TOP_K):
            m = jnp.max(rem, axis=-1, keepdims=True)
            ik = jnp.min(jnp.where(rem == m, e_iota, float(N_EXPERTS)), axis=-1, keepdims=True)
            vals.append(m)
            idxs.append(ik)
            rem = jnp.where(e_iota == ik, -jnp.inf, rem)
        ex = [jnp.exp(v - vals[0]) for v in vals]
        inv_den = 1.0 / (ex[0] + ex[1] + ex[2] + ex[3])
        gates = [e * inv_den for e in ex]

        chosen = [(e_iota == ik) for ik in idxs]
        member = jnp.zeros((cr, N_EXPERTS), F32)
        for ch in chosen:
            member = member + jnp.where(ch, 1.0, 0.0)
        prefix = dot(before, member.astype(BF16)) + count
        ranks = [jnp.sum(jnp.where(ch, prefix, 0.0), axis=-1, keepdims=True) for ch in chosen]
        count = count + jnp.sum(member, axis=0, keepdims=True)

        idx_ref[s, :] = _lane_pack(idxs, cr, jnp.int32)
        rank_ref[s, :] = _lane_pack(ranks, cr, jnp.int32)
        gate_ref[s, :] = _lane_pack(gates, cr, F32)
    cnt_scr[...] = count
    cnt_ref[...] = count


def _mix(x, proj3, y_sb, y_ret, kv, wsbo, wreto, wmix, wcq, wco, gx, gm, wr, br):
    B, S, _ = x.shape
    tm = min(512, S)
    M = kv.shape[1]
    gsb_b, gret_b = OFF_G_SB // D_MODEL, OFF_G_RET // D_MODEL
    tok = lambda b, i: (b, i, 0)
    const2 = lambda b, i: (0, 0)
    row_out = lambda w, dt: jax.ShapeDtypeStruct((B, S, w), dt)
    outs = pl.pallas_call(
        _mix_kernel,
        grid=(B, S // tm),
        in_specs=[
            pl.BlockSpec((None, tm, D_MODEL), tok),
            pl.BlockSpec((None, tm, SB_HEADS * SB_HEAD_DIM), tok),
            pl.BlockSpec((None, tm, D_MODEL), tok),
            pl.BlockSpec((None, tm, D_MODEL), lambda b, i: (b, i, gsb_b)),
            pl.BlockSpec((None, tm, D_MODEL), lambda b, i: (b, i, gret_b)),
            pl.BlockSpec((None, M, 2 * D_MODEL), lambda b, i: (b, 0, 0)),
            pl.BlockSpec((SB_HEADS * SB_HEAD_DIM, D_MODEL), const2),
            pl.BlockSpec((D_MODEL, D_MODEL), const2),
            pl.BlockSpec((D_MODEL, D_MODEL), const2),
            pl.BlockSpec((D_MODEL, D_MODEL), const2),
            pl.BlockSpec((D_MODEL, D_MODEL), const2),
            pl.BlockSpec((1, D_MODEL), const2),
            pl.BlockSpec((1, D_MODEL), const2),
            pl.BlockSpec((D_MODEL, 2 * N_EXPERTS), const2),
            pl.BlockSpec((1, N_EXPERTS), const2),
        ],
        out_specs=[
            pl.BlockSpec((None, tm, D_MODEL), tok),
            pl.BlockSpec((None, tm * ROW_TILES, LANES), tok),
            pl.BlockSpec((None, tm, LANES), tok),
            pl.BlockSpec((None, tm, LANES), tok),
            pl.BlockSpec((None, tm, LANES), tok),
            pl.BlockSpec((1, N_EXPERTS), const2),
        ],
        out_shape=[
            row_out(D_MODEL, F32), jax.ShapeDtypeStruct((B, S * ROW_TILES, LANES), F32),
            row_out(LANES, jnp.int32), row_out(LANES, F32), row_out(LANES, jnp.int32),
            jax.ShapeDtypeStruct((1, N_EXPERTS), F32),
        ],
        scratch_shapes=[pltpu.VMEM((1, N_EXPERTS), F32)],
        compiler_params=_params(("arbitrary", "arbitrary")),
        name="mix",
    )(x, y_sb, y_ret, proj3, proj3, kv, wsbo, wreto, wmix, wcq, wco, gx, gm, wr, br)
    return outs


ROW_TILES = D_MODEL // LANES


def _token_tile_copy(src, s, dst, d, sem):
    rows = lambda i: pl.ds(pl.multiple_of(i * ROW_TILES, ROW_TILES), ROW_TILES)
    return pltpu.make_async_copy(src.at[rows(s)], dst.at[rows(d)], sem)


DMA_UNROLL = 16


ZERO_ROWS = 256


def _dispatch_kernel(lo_ref, hi_ref, pos_ref, h_ref, xin_ref, zbuf, sem, zsem):
    tm = h_ref.shape[0] // ROW_TILES

    def tokens(start, count):
        return pl.ds(pl.multiple_of(start * ROW_TILES, ROW_TILES), count * ROW_TILES)

    def zero_fill(act):
        def per_expert(e, carry):
            lo = lo_ref[e]
            hi = hi_ref[e]

            def chunk(i, off):
                act(pltpu.make_async_copy(zbuf, xin_ref.at[tokens(off, ZERO_ROWS)], zsem))
                return off + ZERO_ROWS

            off = lax.fori_loop(0, (hi - lo) // ZERO_ROWS, chunk, lo)
            rest = hi - off
            size = ZERO_ROWS // 2
            while size >= 1:
                @pl.when((rest & size) != 0)
                def _(off=off, size=size):
                    act(pltpu.make_async_copy(zbuf.at[tokens(0, size)], xin_ref.at[tokens(off, size)],
                                              zsem))
                off = off + (rest & size)
                size //= 2
            return carry

        lax.fori_loop(0, N_EXPERTS, per_expert, 0)

    @pl.when(pl.program_id(0) == 0)
    def _():
        zbuf[...] = jnp.zeros_like(zbuf)
        zero_fill(lambda cp: cp.start())
        zero_fill(lambda cp: cp.wait())

    parts, _, per_part = pos_ref.shape
    for part in range(parts):
        def issue(t, carry, part=part):
            for k in range(TOP_K):
                _token_tile_copy(h_ref, part * (per_part // TOP_K) + t, xin_ref,
                                 pos_ref[part, 0, t * TOP_K + k], sem).start(priority=k % 2)
            return carry

        lax.fori_loop(0, per_part // TOP_K, issue, 0, unroll=DMA_UNROLL)
    for _ in range(TOP_K):
        pltpu.make_async_copy(h_ref, xin_ref.at[tokens(0, tm)], sem).wait()


def _dispatch(h3_tm, pos3, pad_lo, pad_hi, n_rows):
    T = h3_tm.shape[0] // ROW_TILES
    per_tile = pos3.shape[2] // TOP_K
    tm = min(max(1024, per_tile), T)
    parts = tm // per_tile
    grid_spec = pltpu.PrefetchScalarGridSpec(
        num_scalar_prefetch=2,
        grid=(T // tm,),
        in_specs=[
            pl.BlockSpec((parts, 1, per_tile * TOP_K), lambda i, lo, hi: (i, 0, 0),
                         memory_space=pltpu.SMEM),
            pl.BlockSpec((tm * ROW_TILES, LANES), lambda i, lo, hi: (i, 0)),
        ],
        out_specs=pl.BlockSpec(memory_space=pl.ANY),
        scratch_shapes=[pltpu.VMEM((ZERO_ROWS * ROW_TILES, LANES), F32), pltpu.SemaphoreType.DMA(()),
                        pltpu.SemaphoreType.DMA(())],
    )
    return pl.pallas_call(
        _dispatch_kernel,
        grid_spec=grid_spec,
        out_shape=jax.ShapeDtypeStruct((n_rows * ROW_TILES, LANES), F32),
        compiler_params=_params(("arbitrary",)),
        name="dispatch",
    )(pad_lo, pad_hi, pos3, h3_tm)


def _ffn_kernel(first_ref, count_ref, nreal_ref, x_hbm, w1_ref, b1_ref, w2_ref, b2_ref, y_hbm,
                xbuf, ybuf, w1b, w2b, in_sem, out_sem):
    e = pl.program_id(0)
    tb = FFN_BLOCK
    ahead = FFN_SLOTS - FFN_GROUP
    n_real = nreal_ref[0]

    span = tb * ROW_TILES

    def rows(g):
        return pl.ds(pl.multiple_of(g * span, span), span)

    def x_copy(g):
        slot = g % FFN_SLOTS
        return pltpu.make_async_copy(x_hbm.at[rows(g)], xbuf.at[slot], in_sem.at[slot])

    def y_copy(g):
        slot = g % FFN_SLOTS
        return pltpu.make_async_copy(ybuf.at[slot], y_hbm.at[rows(g)], out_sem.at[slot])

    @pl.when(e == 0)
    def _():
        for g in range(ahead):
            @pl.when(g < n_real)
            def _(g=g):
                x_copy(g).start(priority=1)

    @pl.when(count_ref[e] > 0)
    def _():
        w1b[...] = w1_ref[...].astype(BF16)
        w2b[...] = w2_ref[...].astype(BF16)

    def blocks(g0, m):
        group = [g0 + i for i in range(m)]
        for g in group:
            x_copy(g).wait()
        for g in group:
            @pl.when(g + ahead < n_real)
            def _(g=g):
                x_copy(g + ahead).start(priority=1)

            @pl.when(g >= FFN_SLOTS)
            def _(g=g):
                y_copy(g - FFN_SLOTS).wait()

        x = jnp.concatenate(
            [jnp.concatenate([xbuf[g % FFN_SLOTS, pl.ds(c, tb, stride=ROW_TILES), :].astype(BF16)
                              for c in range(ROW_TILES)], axis=1) for g in group], axis=0)
        hc = jnp.dot(x, w1b[...], preferred_element_type=F32) + b1_ref[...]
        glu = jnp.minimum(hc[:, :D_FF], SWIGLU_LIMIT)
        lin = jnp.clip(hc[:, D_FF:], -SWIGLU_LIMIT, SWIGLU_LIMIT)
        act = glu * jax.nn.sigmoid(SWIGLU_ALPHA * glu) * (lin + 1.0)
        y = jnp.dot(act.astype(BF16), w2b[...], preferred_element_type=F32) + b2_ref[...]
        for i, g in enumerate(group):
            for c in range(ROW_TILES):
                ybuf[g % FFN_SLOTS, pl.ds(c, tb, stride=ROW_TILES), :] = (
                    y[i * tb:(i + 1) * tb, c * LANES:(c + 1) * LANES])
            y_copy(g).start(priority=1)

    first, count = first_ref[e], count_ref[e]

    def pair(j, carry):
        blocks(first + FFN_GROUP * j, FFN_GROUP)
        return carry

    lax.fori_loop(0, count // FFN_GROUP, pair, 0)

    def single(j, carry):
        blocks(first + j, 1)
        return carry

    lax.fori_loop(count // FFN_GROUP * FFN_GROUP, count, single, 0)

    @pl.when(e == N_EXPERTS - 1)
    def _():
        for back in range(FFN_SLOTS, 0, -1):
            @pl.when(n_real >= back)
            def _(back=back):
                y_copy(n_real - back).wait()

        for s in range(FFN_SLOTS):
            ybuf[s] = jnp.zeros((tb * ROW_TILES, LANES), F32)

        def tail(g, carry):
            cp = y_copy(g)
            cp.start()
            cp.wait()
            return carry

        lax.fori_loop(n_real, y_hbm.shape[0] // (tb * ROW_TILES), tail, 0)


def _ffn(xin, first_block, block_count, n_real, w1, b1, w2, b2):
    tb = FFN_BLOCK
    expert = lambda e, first, count, nr: (e, 0, 0)
    grid_spec = pltpu.PrefetchScalarGridSpec(
        num_scalar_prefetch=3,
        grid=(N_EXPERTS,),
        in_specs=[
            pl.BlockSpec(memory_space=pl.ANY),
            pl.BlockSpec((None, D_MODEL, 2 * D_FF), expert),
            pl.BlockSpec((None, 1, 2 * D_FF), expert),
            pl.BlockSpec((None, D_FF, D_MODEL), expert),
            pl.BlockSpec((None, 1, D_MODEL), expert),
        ],
        out_specs=pl.BlockSpec(memory_space=pl.ANY),
        scratch_shapes=[pltpu.VMEM((FFN_SLOTS, tb * ROW_TILES, LANES), F32),
                        pltpu.VMEM((FFN_SLOTS, tb * ROW_TILES, LANES), F32),
                        pltpu.VMEM((D_MODEL, 2 * D_FF), BF16), pltpu.VMEM((D_FF, D_MODEL), BF16),
                        pltpu.SemaphoreType.DMA((FFN_SLOTS,)),
                        pltpu.SemaphoreType.DMA((FFN_SLOTS,))],
    )
    return pl.pallas_call(
        _ffn_kernel,
        grid_spec=grid_spec,
        out_shape=jax.ShapeDtypeStruct(xin.shape, F32),
        compiler_params=_params(("arbitrary",)),
        name="ffn",
    )(first_block, block_count, n_real, xin, w1, b1.reshape(N_EXPERTS, 1, 2 * D_FF), w2,
      b2.reshape(N_EXPERTS, 1, D_MODEL))


COMBINE_PARTS = 4
COMBINE_AHEAD = 2


def _combine_kernel(pos_ref, nxt_ref, x2_ref, gate_ref, g_ref, y_ref, o_ref, *scratch):
    bufs, sem = scratch[:COMBINE_PARTS], scratch[COMBINE_PARTS]
    tm = x2_ref.shape[0]
    sub = tm // COMBINE_PARTS
    i = pl.program_id(0)
    last = i + 1 == pl.num_programs(0)

    def request(j):
        p_ref, part = (pos_ref, j) if j < COMBINE_PARTS else (nxt_ref, j - COMBINE_PARTS)
        for t in range(sub):
            for k in range(TOP_K):
                _token_tile_copy(y_ref, p_ref[0, 0, (part * sub + t) * TOP_K + k], bufs[part].at[k], t,
                                 sem.at[part]).start(priority=k % 2)

    def wait(j):
        for k in range(TOP_K):
            pltpu.make_async_copy(y_ref.at[pl.ds(0, sub * ROW_TILES)], bufs[j].at[k], sem.at[j]).wait()

    def reduce(j):
        rows = slice(j * sub, (j + 1) * sub)
        gate = gate_ref[rows, :]
        gates = [gate[:, k:k + 1] for k in range(TOP_K)]
        acc, ssq = [], jnp.zeros((sub, 1), F32)
        for c in range(ROW_TILES):
            a = x2_ref[rows, c * LANES:(c + 1) * LANES]
            for k in range(TOP_K):
                a = a + gates[k] * bufs[j][k, pl.ds(c, sub, stride=ROW_TILES), :]
            acc.append(a)
            ssq = ssq + jnp.sum(a * a, axis=-1, keepdims=True)
        inv = lax.rsqrt(ssq * (1.0 / D_MODEL) + NORM_EPS)
        for c in range(ROW_TILES):
            o_ref[rows, c * LANES:(c + 1) * LANES] = acc[c] * inv * g_ref[:, c * LANES:(c + 1) * LANES]

    @pl.when(i == 0)
    def _():
        for j in range(COMBINE_AHEAD):
            request(j)

    for j in range(COMBINE_PARTS):
        wait(j)
        if j + COMBINE_AHEAD < COMBINE_PARTS:
            request(j + COMBINE_AHEAD)
            reduce(j)
        else:
            @pl.when(jnp.logical_not(last))
            def _(j=j):
                request(j + COMBINE_AHEAD)
                reduce(j)

            @pl.when(last)
            def _(j=j):
                reduce(j)


COMBINE_TILE = 512


def _combine(x2, gate, pos3, yrows, g_final):
    T = x2.shape[0]
    n, _, per_tile = pos3.shape
    tm = per_tile // TOP_K
    return pl.pallas_call(
        _combine_kernel,
        grid=(n,),
        in_specs=[
            pl.BlockSpec((1, 1, tm * TOP_K), lambda i: (i, 0, 0), memory_space=pltpu.SMEM),
            pl.BlockSpec((1, 1, tm * TOP_K), lambda i: (jnp.minimum(i + 1, n - 1), 0, 0),
                         memory_space=pltpu.SMEM),
            pl.BlockSpec((tm, D_MODEL), lambda i: (i, 0)),
            pl.BlockSpec((tm, LANES), lambda i: (i, 0)),
            pl.BlockSpec((1, D_MODEL), lambda i: (0, 0)),
            pl.BlockSpec(memory_space=pl.ANY),
        ],
        out_specs=pl.BlockSpec((tm, D_MODEL), lambda i: (i, 0)),
        out_shape=jax.ShapeDtypeStruct((T, D_MODEL), F32),
        scratch_shapes=([pltpu.VMEM((TOP_K, tm // COMBINE_PARTS * ROW_TILES, LANES), F32)] * COMBINE_PARTS
                        + [pltpu.SemaphoreType.DMA((COMBINE_PARTS,))]),
        compiler_params=_params(("arbitrary",)),
        name="combine",
    )(pos3, pos3, x2, gate, g_final, yrows)


def kernel(x, mem, g_mix, w_in, w_sb_o, w_ret_o, w_mix_out, g_xattn, g_mem, w_cq, w_ckv, w_co,
           g_moe, w_router, b_router, w_exp_in, b_exp_in, w_exp_out, b_exp_out, g_final):
    B, S, _ = x.shape
    T = B * S
    assert w_in.shape[0] == 1, "one layer"
    row = lambda v: v.reshape(1, -1)

    w_in_k = jnp.concatenate([w_in[0][:, o:o + w] for _, o, w in _IN_GROUPS], axis=1).astype(BF16)
    proj = _inproj(x.reshape(T, D_MODEL), row(g_mix[0]), w_in_k)
    proj3 = proj.reshape(B, S, IN_W)
    y_sb = _sb_attention(proj3)
    y_ret = _retention(proj3)
    kv = _memkv(mem, row(g_mem[0]), w_ckv[0].astype(BF16))
    wr_hi = w_router[0].astype(BF16)
    wr_lo = (w_router[0] - wr_hi.astype(F32)).astype(BF16)
    x2, h3, idx, gate, rank, cnt = _mix(
        x, proj3, y_sb, y_ret, kv,
        w_sb_o[0].astype(BF16), w_ret_o[0].astype(BF16), w_mix_out[0].astype(BF16),
        w_cq[0].astype(BF16), w_co[0].astype(BF16),
        row(g_xattn[0]), row(g_moe[0]), jnp.concatenate([wr_hi, wr_lo], axis=1), row(b_router[0]))

    tb = FFN_BLOCK
    n_blocks = (T * TOP_K + N_EXPERTS * (tb - 1) + tb - 1) // tb
    counts = cnt[0].astype(jnp.int32)
    padded = ((counts + tb - 1) // tb) * tb
    pad_end = jnp.cumsum(padded)
    pad_start = pad_end - padded
    idx4 = idx.reshape(T, LANES)[:, :TOP_K]
    start4 = jnp.sum(jnp.where(idx4[:, :, None] == jnp.arange(N_EXPERTS), pad_start, 0), axis=-1)
    pos = (start4 + rank.reshape(T, LANES)[:, :TOP_K]).astype(jnp.int32).reshape(T * TOP_K)
    n_real = (pad_end[-1:] // tb).astype(jnp.int32)
    first_block = (pad_start // tb).astype(jnp.int32)
    block_count = (padded // tb).astype(jnp.int32)

    pad_lo = (pad_start + counts).astype(jnp.int32)
    pad_hi = jnp.where(jnp.arange(N_EXPERTS) == N_EXPERTS - 1, n_blocks * tb, pad_end).astype(jnp.int32)
    tile = min(COMBINE_TILE, T)
    pos3 = pos.reshape(T // tile, 1, tile * TOP_K)
    xin = _dispatch(h3.reshape(T * ROW_TILES, LANES), pos3, pad_lo, pad_hi, n_blocks * tb)
    yrows = _ffn(xin, first_block, block_count, n_real,
                 w_exp_in[0], b_exp_in[0], w_exp_out[0], b_exp_out[0])
    out = _combine(x2.reshape(T, D_MODEL), gate.reshape(T, LANES), pos3, yrows, row(g_final))
    return out.reshape(B, S, D_MODEL)
```

```python
import functools

import numpy as np
import jax
import jax.numpy as jnp
from jax import lax
from jax.experimental import pallas as pl
from jax.experimental.pallas import tpu as pltpu

F32 = jnp.float32
BF16 = jnp.bfloat16

D_MODEL = 1024
SB_HEADS = 8
SB_HEAD_DIM = 64
RET_HEADS = 8
RET_QK_DIM = 64
RET_V_DIM = 128
ROPE_BASE = 10000.0
X_HEADS = 4
X_HEAD_DIM = 256
N_EXPERTS = 32
TOP_K = 4
D_FF = 1024
SWIGLU_LIMIT = 7.0
SWIGLU_ALPHA = 1.702
NORM_EPS = 1e-5

_IN_GROUPS = (("r_v", 2560, 1024), ("r_g", 3584, 1024), ("g_sb", 4608, 1024), ("g_ret", 5632, 1024),
              ("sb_q", 0, 512), ("sb_k", 512, 512), ("sb_v", 1024, 512),
              ("r_q", 1536, 512), ("r_k", 2048, 512))
OFF_R_V, OFF_R_G, OFF_G_SB, OFF_G_RET = 0, 1024, 2048, 3072
OFF_SB_Q, OFF_SB_K, OFF_SB_V, OFF_R_Q, OFF_R_K = 4096, 4608, 5120, 5632, 6144
IN_W = 6656

LANES = 128
VMEM_LIMIT = 56 * 1024 * 1024

SB_TILE = 128
SB_LOG_UNDERFLOW = -88.0
RET_CHUNK = 256
FFN_BLOCK = 128
FFN_GROUP = 4
FFN_SLOTS = 12

def _rms(x, g):
    return x * lax.rsqrt(jnp.mean(x * x, axis=-1, keepdims=True) + NORM_EPS) * g


def _params(sem):
    return pltpu.CompilerParams(dimension_semantics=sem, vmem_limit_bytes=VMEM_LIMIT)


INPROJ_COLS = 512


def _inproj_kernel(x_ref, g_ref, w_ref, o_ref):
    h = _rms(x_ref[...], g_ref[...]).astype(BF16)
    for j in range(IN_W // INPROJ_COLS):
        cols = slice(j * INPROJ_COLS, (j + 1) * INPROJ_COLS)
        o_ref[:, cols] = jnp.dot(h, w_ref[:, cols], preferred_element_type=F32).astype(o_ref.dtype)


def _inproj(x2d, g, w_bf16):
    T = x2d.shape[0]
    tm = min(512, T)
    return pl.pallas_call(
        _inproj_kernel,
        grid=(T // tm,),
        in_specs=[
            pl.BlockSpec((tm, D_MODEL), lambda i: (i, 0)),
            pl.BlockSpec((1, D_MODEL), lambda i: (0, 0)),
            pl.BlockSpec((D_MODEL, IN_W), lambda i: (0, 0)),
        ],
        out_specs=pl.BlockSpec((tm, IN_W), lambda i: (i, 0)),
        out_shape=jax.ShapeDtypeStruct((T, IN_W), BF16),
        compiler_params=_params(("parallel",)),
        name="inproj",
    )(x2d, g, w_bf16)


SB_PAIRS = SB_HEADS * SB_HEAD_DIM // LANES


def _sb_kernel(q_ref, k_ref, v_ref, o_ref, q2_ref, *state_refs):
    carry_refs, acc_refs = state_refs[:SB_PAIRS], state_refs[SB_PAIRS:]
    n = SB_TILE
    qi = pl.program_id(1)
    lane = lax.broadcasted_iota(jnp.int32, (n, LANES), 1)
    first = lane < SB_HEAD_DIM
    row = lax.broadcasted_iota(jnp.int32, (2 * n, n), 0) & (n - 1)
    col = lax.broadcasted_iota(jnp.int32, (2 * n, n), 1)
    strict = col < row
    jj = lax.broadcasted_iota(jnp.int32, (2 * n, 2 * n), 0) & (n - 1)
    cc = lax.broadcasted_iota(jnp.int32, (2 * n, 2 * n), 1)
    suffix = jnp.where((cc >= n) | (jj >= cc), 1.0, 0.0).astype(BF16)
    scale = SB_HEAD_DIM ** -0.5

    for p in range(SB_PAIRS):
        qs = (q_ref[:, p * LANES:(p + 1) * LANES].astype(F32) * scale).astype(BF16)
        zeros = jnp.zeros_like(qs)
        q2_ref[p * 2 * n:p * 2 * n + n, :] = jnp.where(first, qs, zeros)
        q2_ref[p * 2 * n + n:(p + 1) * 2 * n, :] = jnp.where(first, zeros, qs)

    def step(kt, diagonal):
        ks = pl.multiple_of(kt * n, n)
        zs, hls, css, ws, cmaxs = {}, {}, {}, {}, []

        def logits(p):
            kk = k_ref[pl.ds(ks, n), p * LANES:(p + 1) * LANES]
            z = lax.dot_general(q2_ref[p * 2 * n:(p + 1) * 2 * n, :], kk, (((1,), (1,)), ((), ())),
                                preferred_element_type=F32)
            sp = jnp.maximum(z, 0.0) + jnp.log(1.0 + jnp.exp(-jnp.abs(z)))
            if diagonal:
                sp = jnp.where(strict, sp, 0.0)
            hi = sp.astype(BF16)
            lo = (sp - hi.astype(F32)).astype(BF16)
            zs[p] = z
            hls[p] = jnp.concatenate([hi, lo], axis=1)

        def sums(p):
            css[p] = jnp.dot(hls[p], suffix, preferred_element_type=F32)

        def weights(p):
            logw = zs[p] - css[p][:, :n]
            if not diagonal:
                logw = logw + carry_refs[p][...]
            w = jnp.exp(logw)
            if diagonal:
                w = jnp.where(strict, w, 0.0)
            ws[p] = w.astype(BF16)

        def values(p):
            vv = v_ref[pl.ds(ks, n), p * LANES:(p + 1) * LANES]
            pv = jnp.dot(ws[p], vv, preferred_element_type=F32)
            acc_refs[p][...] = pv if diagonal else acc_refs[p][...] + pv
            carry = -css[p][:, n:] if diagonal else carry_refs[p][...] - css[p][:, n:]
            carry_refs[p][...] = carry
            cmaxs.append(jnp.max(carry))

        for stage in (logits, sums, weights, values):
            for p in range(SB_PAIRS):
                stage(p)
        return functools.reduce(jnp.maximum, cmaxs)

    def cond(st):
        kt, cmax = st
        return jnp.logical_and(kt >= 0, cmax > SB_LOG_UNDERFLOW)

    def body(st):
        kt, _ = st
        return kt - 1, step(kt, False)

    lax.while_loop(cond, body, (qi - 1, step(qi, True)))

    for p in range(SB_PAIRS):
        o_ref[:, p * LANES:(p + 1) * LANES] = jnp.where(
            first, acc_refs[p][:n, :], acc_refs[p][n:, :]
        ).astype(o_ref.dtype)


def _sb_attention(proj3):
    B, S, _ = proj3.shape
    n = SB_TILE
    w = SB_HEADS * SB_HEAD_DIM
    qb, kb, vb = OFF_SB_Q // w, OFF_SB_K // w, OFF_SB_V // w
    rows = SB_PAIRS * 2 * n
    return pl.pallas_call(
        _sb_kernel,
        grid=(B, S // n),
        in_specs=[
            pl.BlockSpec((None, n, w), lambda b, i: (b, i, qb)),
            pl.BlockSpec((None, S, w), lambda b, i: (b, 0, kb)),
            pl.BlockSpec((None, S, w), lambda b, i: (b, 0, vb)),
        ],
        out_specs=pl.BlockSpec((None, n, w), lambda b, i: (b, i, 0)),
        out_shape=jax.ShapeDtypeStruct((B, S, w), BF16),
        scratch_shapes=([pltpu.VMEM((rows, LANES), BF16)]
                        + [pltpu.VMEM((2 * n, LANES), F32)] * (2 * SB_PAIRS)),
        compiler_params=_params(("parallel", "arbitrary")),
        name="sb",
    )(proj3, proj3, proj3)


def _ret_tables(S):
    c = RET_CHUNK
    f = np.float32
    inv = f(ROPE_BASE) ** (-np.arange(0, RET_QK_DIM, 2, dtype=f) / f(RET_QK_DIM))
    ang = np.arange(S, dtype=f)[:, None] * inv[None, :]
    cos, sin = np.cos(ang), np.sin(ang)
    reps = LANES // RET_QK_DIM
    cos_t = np.tile(np.concatenate([cos, cos], axis=1), (1, reps))
    sin_t = np.tile(np.concatenate([-sin, sin], axis=1), (1, reps))
    log_g = np.log1p(-np.exp2(-5.0 - np.arange(RET_HEADS, dtype=np.float64)))
    pos = np.arange(c, dtype=np.float64)
    diff = pos[:, None] - pos[None, :]
    decay = np.where(diff >= 0, np.exp(log_g[:, None, None] * np.maximum(diff, 0.0)), 0.0).astype(f)
    q_dec = np.exp(log_g[:, None] * (pos + 1.0)[None, :]).astype(f)
    k_dec = np.exp(log_g[:, None] * (c - 1 - pos)[None, :]).astype(f)
    q_dec = np.ascontiguousarray(np.broadcast_to(q_dec[:, :, None], (RET_HEADS, c, LANES)))
    k_dec = np.ascontiguousarray(np.broadcast_to(k_dec[:, :, None], (RET_HEADS, c, LANES)))
    c_dec = np.ascontiguousarray(np.broadcast_to(
        np.exp(log_g * c).astype(f)[:, None, None], (RET_HEADS, LANES, LANES)))
    return cos_t, sin_t, decay, q_dec, k_dec, c_dec


def _ret_kernel(q_ref, k_ref, v_ref, g_ref, cos_ref, sin_ref, dm_ref, qd_ref, kd_ref, cd_ref,
                o_ref, state):
    c = RET_CHUNK

    @pl.when(pl.program_id(1) == 0)
    def _():
        state[...] = jnp.zeros_like(state)

    lane = lax.broadcasted_iota(jnp.int32, (c, LANES), 1)
    low_half = (lane & (RET_QK_DIM - 1)) < RET_QK_DIM // 2
    cos = cos_ref[...]
    sin = sin_ref[...]

    def rotary(x):
        swapped = jnp.where(low_half, pltpu.roll(x, LANES - RET_QK_DIM // 2, 1),
                            pltpu.roll(x, RET_QK_DIM // 2, 1))
        return x * cos + swapped * sin

    heads = range(RET_HEADS)
    cols = [slice(h * RET_V_DIM, (h + 1) * RET_V_DIM) for h in heads]
    rq, rk, qh, sc, out = {}, {}, {}, {}, {}

    for p in range(RET_HEADS * RET_QK_DIM // LANES):
        rq[p] = rotary(q_ref[:, p * LANES:(p + 1) * LANES].astype(F32))
        rk[p] = rotary(k_ref[:, p * LANES:(p + 1) * LANES].astype(F32)) * RET_QK_DIM ** -0.5
    for h in heads:
        in_head = (lane < RET_QK_DIM) if h % 2 == 0 else (lane >= RET_QK_DIM)
        qh[h] = jnp.where(in_head, rq[h // 2], 0.0)
        sc[h] = lax.dot_general(qh[h].astype(BF16), rk[h // 2].astype(BF16), (((1,), (1,)), ((), ())),
                                preferred_element_type=F32) * dm_ref[h]
    for h in heads:
        vh = v_ref[:, cols[h]]
        o = jnp.dot(sc[h].astype(BF16), vh, preferred_element_type=F32)
        out[h] = o + jnp.dot((qh[h] * qd_ref[h]).astype(BF16), state[h].astype(BF16),
                             preferred_element_type=F32)
        kv_new = lax.dot_general((rk[h // 2] * kd_ref[h]).astype(BF16), vh, (((0,), (0,)), ((), ())),
                                 preferred_element_type=F32)
        state[h] = state[h] * cd_ref[h] + kv_new
    for h in heads:
        o = out[h]
        mu = jnp.mean(o, axis=-1, keepdims=True)
        d = o - mu
        y = d * lax.rsqrt(jnp.mean(d * d, axis=-1, keepdims=True) + NORM_EPS)
        g = g_ref[:, cols[h]].astype(F32)
        o_ref[:, cols[h]] = (g * jax.nn.sigmoid(g) * y).astype(o_ref.dtype)


def _retention(proj3):
    B, S, _ = proj3.shape
    c = RET_CHUNK
    qkw, vw = RET_HEADS * RET_QK_DIM, RET_HEADS * RET_V_DIM
    qb, kb, vb, gb = OFF_R_Q // qkw, OFF_R_K // qkw, OFF_R_V // vw, OFF_R_G // vw
    cos_t, sin_t, decay, q_dec, k_dec, c_dec = _ret_tables(S)
    const3 = lambda b, i: (0, 0, 0)
    return pl.pallas_call(
        _ret_kernel,
        grid=(B, S // c),
        in_specs=[
            pl.BlockSpec((None, c, qkw), lambda b, i: (b, i, qb)),
            pl.BlockSpec((None, c, qkw), lambda b, i: (b, i, kb)),
            pl.BlockSpec((None, c, vw), lambda b, i: (b, i, vb)),
            pl.BlockSpec((None, c, vw), lambda b, i: (b, i, gb)),
            pl.BlockSpec((c, LANES), lambda b, i: (i, 0)),
            pl.BlockSpec((c, LANES), lambda b, i: (i, 0)),
            pl.BlockSpec((RET_HEADS, c, c), const3),
            pl.BlockSpec((RET_HEADS, c, LANES), const3),
            pl.BlockSpec((RET_HEADS, c, LANES), const3),
            pl.BlockSpec((RET_HEADS, LANES, LANES), const3),
        ],
        out_specs=pl.BlockSpec((None, c, vw), lambda b, i: (b, i, 0)),
        out_shape=jax.ShapeDtypeStruct((B, S, vw), BF16),
        scratch_shapes=[pltpu.VMEM((RET_HEADS, LANES, LANES), F32)],
        compiler_params=_params(("parallel", "arbitrary")),
        name="ret",
    )(proj3, proj3, proj3, proj3, cos_t, sin_t, decay, q_dec, k_dec, c_dec)


def _memkv_kernel(m_ref, g_ref, w_ref, o_ref):
    h = _rms(m_ref[...], g_ref[...]).astype(BF16)
    o_ref[...] = jnp.dot(h, w_ref[...], preferred_element_type=F32).astype(o_ref.dtype)


def _memkv(mem, g, w_bf16):
    B, M, _ = mem.shape
    return pl.pallas_call(
        _memkv_kernel,
        grid=(B,),
        in_specs=[
            pl.BlockSpec((None, M, D_MODEL), lambda b: (b, 0, 0)),
            pl.BlockSpec((1, D_MODEL), lambda b: (0, 0)),
            pl.BlockSpec((D_MODEL, 2 * D_MODEL), lambda b: (0, 0)),
        ],
        out_specs=pl.BlockSpec((None, M, 2 * D_MODEL), lambda b: (b, 0, 0)),
        out_shape=jax.ShapeDtypeStruct((B, M, 2 * D_MODEL), BF16),
        compiler_params=_params(("parallel",)),
        name="memkv",
    )(mem, g, w_bf16)


MIX_CHUNKS = 2


def _lane_pack(cols, rows, dtype):
    lane = lax.broadcasted_iota(jnp.int32, (rows, LANES), 1)
    out = jnp.zeros((rows, LANES), dtype)
    for k, c in enumerate(cols):
        out = jnp.where(lane == k, c.astype(dtype), out)
    return out


def _mix_kernel(x_ref, ysb_ref, yret_ref, gsb_ref, gret_ref, kv_ref,
                wsbo_ref, wreto_ref, wmix_ref, wcq_ref, wco_ref,
                gx_ref, gm_ref, wr_ref, br_ref,
                x2_ref, h3_ref, idx_ref, gate_ref, rank_ref, cnt_ref, cnt_scr):
    tm = x_ref.shape[0]
    cr = tm // MIX_CHUNKS
    chunks = [slice(c * cr, (c + 1) * cr) for c in range(MIX_CHUNKS)]
    dot = functools.partial(jnp.dot, preferred_element_type=F32)

    @pl.when(jnp.logical_and(pl.program_id(0) == 0, pl.program_id(1) == 0))
    def _():
        cnt_scr[...] = jnp.zeros_like(cnt_scr)

    def sigmoid(g):
        return (0.5 * jnp.tanh(0.5 * g) + 0.5).astype(F32)

    a = [dot(ysb_ref[s, :], wsbo_ref[...]) for s in chunks]
    b = [dot(yret_ref[s, :], wreto_ref[...]) for s in chunks]
    merged = [sigmoid(gsb_ref[s, :]) * ac + sigmoid(gret_ref[s, :]) * bc
              for s, ac, bc in zip(chunks, a, b)]
    x1 = [x_ref[s, :] + dot(m.astype(BF16), wmix_ref[...]) for s, m in zip(chunks, merged)]

    h2 = [_rms(v, gx_ref[...]).astype(BF16) for v in x1]
    q = [(dot(v, wcq_ref[...]) * X_HEAD_DIM ** -0.5).astype(BF16) for v in h2]
    heads = [[] for _ in chunks]
    for hh in range(X_HEADS):
        lo = hh * X_HEAD_DIM
        kh = kv_ref[:, lo:lo + X_HEAD_DIM]
        vh = kv_ref[:, D_MODEL + lo:D_MODEL + lo + X_HEAD_DIM]
        s = [lax.dot_general(qc[:, lo:lo + X_HEAD_DIM], kh, (((1,), (1,)), ((), ())),
                             preferred_element_type=F32) for qc in q]
        p = [jnp.exp(sc - jnp.max(sc, axis=-1, keepdims=True)) for sc in s]
        p = [pc * (1.0 / jnp.sum(pc, axis=-1, keepdims=True)) for pc in p]
        for c, pc in enumerate(p):
            heads[c].append(dot(pc.astype(BF16), vh).astype(BF16))
    o = [jnp.concatenate(hc, axis=1) for hc in heads]
    x2 = [v + dot(oc, wco_ref[...]) for v, oc in zip(x1, o)]
    for s, v in zip(chunks, x2):
        x2_ref[s, :] = v

    h3 = [_rms(v, gm_ref[...]) for v in x2]
    for ci, v in enumerate(h3):
        for c in range(ROW_TILES):
            h3_ref[pl.ds(ci * cr * ROW_TILES + c, cr, stride=ROW_TILES), :] = v[:, c * LANES:(c + 1) * LANES]
    logits = []
    for v in h3:
        hi = v.astype(BF16)
        lo = (v - hi.astype(F32)).astype(BF16)
        r = dot(jnp.concatenate([hi, lo], axis=0), wr_ref[...])
        logits.append(r[:cr, :N_EXPERTS] + r[:cr, N_EXPERTS:] + r[cr:, :N_EXPERTS] + br_ref[...])

    e_iota = lax.broadcasted_iota(jnp.int32, (cr, N_EXPERTS), 1).astype(F32)
    r_i = lax.broadcasted_iota(jnp.int32, (cr, cr), 0)
    c_i = lax.broadcasted_iota(jnp.int32, (cr, cr), 1)
    before = jnp.where(c_i < r_i, 1.0, 0.0).astype(BF16)
    count = cnt_scr[...]
    for s, rem in zip(chunks, logits):
        vals, idxs = [], []
        for _ in range(TOP_K):
            m = jnp.max(rem, axis=-1, keepdims=True)
            ik = jnp.min(jnp.where(rem == m, e_iota, float(N_EXPERTS)), axis=-1, keepdims=True)
            vals.append(m)
            idxs.append(ik)
            rem = jnp.where(e_iota == ik, -jnp.inf, rem)
        ex = [jnp.exp(v - vals[0]) for v in vals]
        inv_den = 1.0 / (ex[0] + ex[1] + ex[2] + ex[3])
        gates = [e * inv_den for e in ex]

        chosen = [(e_iota == ik) for ik in idxs]
        member = jnp.zeros((cr, N_EXPERTS), F32)
        for ch in chosen:
            member = member + jnp.where(ch, 1.0, 0.0)
        prefix = dot(before, member.astype(BF16)) + count
        ranks = [jnp.sum(jnp.where(ch, prefix, 0.0), axis=-1, keepdims=True) for ch in chosen]
        count = count + jnp.sum(member, axis=0, keepdims=True)

        idx_ref[s, :] = _lane_pack(idxs, cr, jnp.int32)
        rank_ref[s, :] = _lane_pack(ranks, cr, jnp.int32)
        gate_ref[s, :] = _lane_pack(gates, cr, F32)
    cnt_scr[...] = count
    cnt_ref[...] = count


def _mix(x, proj3, y_sb, y_ret, kv, wsbo, wreto, wmix, wcq, wco, gx, gm, wr, br):
    B, S, _ = x.shape
    tm = min(512, S)
    M = kv.shape[1]
    gsb_b, gret_b = OFF_G_SB // D_MODEL, OFF_G_RET // D_MODEL
    tok = lambda b, i: (b, i, 0)
    const2 = lambda b, i: (0, 0)
    row_out = lambda w, dt: jax.ShapeDtypeStruct((B, S, w), dt)
    outs = pl.pallas_call(
        _mix_kernel,
        grid=(B, S // tm),
        in_specs=[
            pl.BlockSpec((None, tm, D_MODEL), tok),
            pl.BlockSpec((None, tm, SB_HEADS * SB_HEAD_DIM), tok),
            pl.BlockSpec((None, tm, D_MODEL), tok),
            pl.BlockSpec((None, tm, D_MODEL), lambda b, i: (b, i, gsb_b)),
            pl.BlockSpec((None, tm, D_MODEL), lambda b, i: (b, i, gret_b)),
            pl.BlockSpec((None, M, 2 * D_MODEL), lambda b, i: (b, 0, 0)),
            pl.BlockSpec((SB_HEADS * SB_HEAD_DIM, D_MODEL), const2),
            pl.BlockSpec((D_MODEL, D_MODEL), const2),
            pl.BlockSpec((D_MODEL, D_MODEL), const2),
            pl.BlockSpec((D_MODEL, D_MODEL), const2),
            pl.BlockSpec((D_MODEL, D_MODEL), const2),
            pl.BlockSpec((1, D_MODEL), const2),
            pl.BlockSpec((1, D_MODEL), const2),
            pl.BlockSpec((D_MODEL, 2 * N_EXPERTS), const2),
            pl.BlockSpec((1, N_EXPERTS), const2),
        ],
        out_specs=[
            pl.BlockSpec((None, tm, D_MODEL), tok),
            pl.BlockSpec((None, tm * ROW_TILES, LANES), tok),
            pl.BlockSpec((None, tm, LANES), tok),
            pl.BlockSpec((None, tm, LANES), tok),
            pl.BlockSpec((None, tm, LANES), tok),
            pl.BlockSpec((1, N_EXPERTS), const2),
        ],
        out_shape=[
            row_out(D_MODEL, F32), jax.ShapeDtypeStruct((B, S * ROW_TILES, LANES), F32),
            row_out(LANES, jnp.int32), row_out(LANES, F32), row_out(LANES, jnp.int32),
            jax.ShapeDtypeStruct((1, N_EXPERTS), F32),
        ],
        scratch_shapes=[pltpu.VMEM((1, N_EXPERTS), F32)],
        compiler_params=_params(("arbitrary", "arbitrary")),
        name="mix",
    )(x, y_sb, y_ret, proj3, proj3, kv, wsbo, wreto, wmix, wcq, wco, gx, gm, wr, br)
    return outs


ROW_TILES = D_MODEL // LANES


def _token_tile_copy(src, s, dst, d, sem):
    rows = lambda i: pl.ds(pl.multiple_of(i * ROW_TILES, ROW_TILES), ROW_TILES)
    return pltpu.make_async_copy(src.at[rows(s)], dst.at[rows(d)], sem)


DMA_UNROLL = 16


ZERO_ROWS = 256


def _dispatch_kernel(lo_ref, hi_ref, pos_ref, h_ref, xin_ref, zbuf, sem, zsem):
    tm = h_ref.shape[0] // ROW_TILES

    def tokens(start, count):
        return pl.ds(pl.multiple_of(start * ROW_TILES, ROW_TILES), count * ROW_TILES)

    def zero_fill(act):
        def per_expert(e, carry):
            lo = lo_ref[e]
            hi = hi_ref[e]

            def chunk(i, off):
                act(pltpu.make_async_copy(zbuf, xin_ref.at[tokens(off, ZERO_ROWS)], zsem))
                return off + ZERO_ROWS

            off = lax.fori_loop(0, (hi - lo) // ZERO_ROWS, chunk, lo)
            rest = hi - off
            size = ZERO_ROWS // 2
            while size >= 1:
                @pl.when((rest & size) != 0)
                def _(off=off, size=size):
                    act(pltpu.make_async_copy(zbuf.at[tokens(0, size)], xin_ref.at[tokens(off, size)],
                                              zsem))
                off = off + (rest & size)
                size //= 2
            return carry

        lax.fori_loop(0, N_EXPERTS, per_expert, 0)

    @pl.when(pl.program_id(0) == 0)
    def _():
        zbuf[...] = jnp.zeros_like(zbuf)
        zero_fill(lambda cp: cp.start())
        zero_fill(lambda cp: cp.wait())

    parts, _, per_part = pos_ref.shape
    for part in range(parts):
        def issue(t, carry, part=part):
            for k in range(TOP_K):
                _token_tile_copy(h_ref, part * (per_part // TOP_K) + t, xin_ref,
                                 pos_ref[part, 0, t * TOP_K + k], sem).start(priority=k % 2)
            return carry

        lax.fori_loop(0, per_part // TOP_K, issue, 0, unroll=DMA_UNROLL)
    for _ in range(TOP_K):
        pltpu.make_async_copy(h_ref, xin_ref.at[tokens(0, tm)], sem).wait()


def _dispatch(h3_tm, pos3, pad_lo, pad_hi, n_rows):
    T = h3_tm.shape[0] // ROW_TILES
    per_tile = pos3.shape[2] // TOP_K
    tm = min(max(1024, per_tile), T)
    parts = tm // per_tile
    grid_spec = pltpu.PrefetchScalarGridSpec(
        num_scalar_prefetch=2,
        grid=(T // tm,),
        in_specs=[
            pl.BlockSpec((parts, 1, per_tile * TOP_K), lambda i, lo, hi: (i, 0, 0),
                         memory_space=pltpu.SMEM),
            pl.BlockSpec((tm * ROW_TILES, LANES), lambda i, lo, hi: (i, 0)),
        ],
        out_specs=pl.BlockSpec(memory_space=pl.ANY),
        scratch_shapes=[pltpu.VMEM((ZERO_ROWS * ROW_TILES, LANES), F32), pltpu.SemaphoreType.DMA(()),
                        pltpu.SemaphoreType.DMA(())],
    )
    return pl.pallas_call(
        _dispatch_kernel,
        grid_spec=grid_spec,
        out_shape=jax.ShapeDtypeStruct((n_rows * ROW_TILES, LANES), F32),
        compiler_params=_params(("arbitrary",)),
        name="dispatch",
    )(pad_lo, pad_hi, pos3, h3_tm)


def _ffn_kernel(first_ref, count_ref, nreal_ref, x_hbm, w1_ref, b1_ref, w2_ref, b2_ref, y_hbm,
                xbuf, ybuf, w1b, w2b, in_sem, out_sem):
    e = pl.program_id(0)
    tb = FFN_BLOCK
    ahead = FFN_SLOTS - FFN_GROUP
    n_real = nreal_ref[0]

    span = tb * ROW_TILES

    def rows(g):
        return pl.ds(pl.multiple_of(g * span, span), span)

    def x_copy(g):
        slot = g % FFN_SLOTS
        return pltpu.make_async_copy(x_hbm.at[rows(g)], xbuf.at[slot], in_sem.at[slot])

    def y_copy(g):
        slot = g % FFN_SLOTS
        return pltpu.make_async_copy(ybuf.at[slot], y_hbm.at[rows(g)], out_sem.at[slot])

    @pl.when(e == 0)
    def _():
        for g in range(ahead):
            @pl.when(g < n_real)
            def _(g=g):
                x_copy(g).start(priority=1)

    @pl.when(count_ref[e] > 0)
    def _():
        w1b[...] = w1_ref[...].astype(BF16)
        w2b[...] = w2_ref[...].astype(BF16)

    def blocks(g0, m):
        group = [g0 + i for i in range(m)]
        for g in group:
            x_copy(g).wait()
        for g in group:
            @pl.when(g + ahead < n_real)
            def _(g=g):
                x_copy(g + ahead).start(priority=1)

            @pl.when(g >= FFN_SLOTS)
            def _(g=g):
                y_copy(g - FFN_SLOTS).wait()

        x = jnp.concatenate(
            [jnp.concatenate([xbuf[g % FFN_SLOTS, pl.ds(c, tb, stride=ROW_TILES), :].astype(BF16)
                              for c in range(ROW_TILES)], axis=1) for g in group], axis=0)
        hc = jnp.dot(x, w1b[...], preferred_element_type=F32) + b1_ref[...]
        glu = jnp.minimum(hc[:, :D_FF], SWIGLU_LIMIT)
        lin = jnp.clip(hc[:, D_FF:], -SWIGLU_LIMIT, SWIGLU_LIMIT)
        act = glu * jax.nn.sigmoid(SWIGLU_ALPHA * glu) * (lin + 1.0)
        y = jnp.dot(act.astype(BF16), w2b[...], preferred_element_type=F32) + b2_ref[...]
        for i, g in enumerate(group):
            for c in range(ROW_TILES):
                ybuf[g % FFN_SLOTS, pl.ds(c, tb, stride=ROW_TILES), :] = (
                    y[i * tb:(i + 1) * tb, c * LANES:(c + 1) * LANES])
            y_copy(g).start(priority=1)

    first, count = first_ref[e], count_ref[e]

    def pair(j, carry):
        blocks(first + FFN_GROUP * j, FFN_GROUP)
        return carry

    lax.fori_loop(0, count // FFN_GROUP, pair, 0)

    for left in range(1, FFN_GROUP):
        @pl.when(count % FFN_GROUP == left)
        def _(left=left):
            blocks(first + count - left, left)

    @pl.when(e == N_EXPERTS - 1)
    def _():
        for back in range(FFN_SLOTS, 0, -1):
            @pl.when(n_real >= back)
            def _(back=back):
                y_copy(n_real - back).wait()

        for s in range(FFN_SLOTS):
            ybuf[s] = jnp.zeros((tb * ROW_TILES, LANES), F32)

        def tail(g, carry):
            cp = y_copy(g)
            cp.start()
            cp.wait()
            return carry

        lax.fori_loop(n_real, y_hbm.shape[0] // (tb * ROW_TILES), tail, 0)


def _ffn(xin, first_block, block_count, n_real, w1, b1, w2, b2):
    tb = FFN_BLOCK
    expert = lambda e, first, count, nr: (e, 0, 0)
    grid_spec = pltpu.PrefetchScalarGridSpec(
        num_scalar_prefetch=3,
        grid=(N_EXPERTS,),
        in_specs=[
            pl.BlockSpec(memory_space=pl.ANY),
            pl.BlockSpec((None, D_MODEL, 2 * D_FF), expert),
            pl.BlockSpec((None, 1, 2 * D_FF), expert),
            pl.BlockSpec((None, D_FF, D_MODEL), expert),
            pl.BlockSpec((None, 1, D_MODEL), expert),
        ],
        out_specs=pl.BlockSpec(memory_space=pl.ANY),
        scratch_shapes=[pltpu.VMEM((FFN_SLOTS, tb * ROW_TILES, LANES), F32),
                        pltpu.VMEM((FFN_SLOTS, tb * ROW_TILES, LANES), F32),
                        pltpu.VMEM((D_MODEL, 2 * D_FF), BF16), pltpu.VMEM((D_FF, D_MODEL), BF16),
                        pltpu.SemaphoreType.DMA((FFN_SLOTS,)),
                        pltpu.SemaphoreType.DMA((FFN_SLOTS,))],
    )
    return pl.pallas_call(
        _ffn_kernel,
        grid_spec=grid_spec,
        out_shape=jax.ShapeDtypeStruct(xin.shape, F32),
        compiler_params=_params(("arbitrary",)),
        name="ffn",
    )(first_block, block_count, n_real, xin, w1, b1.reshape(N_EXPERTS, 1, 2 * D_FF), w2,
      b2.reshape(N_EXPERTS, 1, D_MODEL))


COMBINE_PARTS = 4
COMBINE_AHEAD = 2


def _combine_kernel(pos_ref, nxt_ref, x2_ref, gate_ref, g_ref, y_ref, o_ref, *scratch):
    bufs, sem = scratch[:COMBINE_PARTS], scratch[COMBINE_PARTS]
    tm = x2_ref.shape[0]
    sub = tm // COMBINE_PARTS
    i = pl.program_id(0)
    last = i + 1 == pl.num_programs(0)

    def request(j):
        p_ref, part = (pos_ref, j) if j < COMBINE_PARTS else (nxt_ref, j - COMBINE_PARTS)
        for t in range(sub):
            for k in range(TOP_K):
                _token_tile_copy(y_ref, p_ref[0, 0, (part * sub + t) * TOP_K + k], bufs[part].at[k], t,
                                 sem.at[part]).start(priority=k % 2)

    def wait(j):
        for k in range(TOP_K):
            pltpu.make_async_copy(y_ref.at[pl.ds(0, sub * ROW_TILES)], bufs[j].at[k], sem.at[j]).wait()

    def reduce(j):
        rows = slice(j * sub, (j + 1) * sub)
        gate = gate_ref[rows, :]
        gates = [gate[:, k:k + 1] for k in range(TOP_K)]
        acc, ssq = [], jnp.zeros((sub, 1), F32)
        for c in range(ROW_TILES):
            a = x2_ref[rows, c * LANES:(c + 1) * LANES]
            for k in range(TOP_K):
                a = a + gates[k] * bufs[j][k, pl.ds(c, sub, stride=ROW_TILES), :]
            acc.append(a)
            ssq = ssq + jnp.sum(a * a, axis=-1, keepdims=True)
        inv = lax.rsqrt(ssq * (1.0 / D_MODEL) + NORM_EPS)
        for c in range(ROW_TILES):
            o_ref[rows, c * LANES:(c + 1) * LANES] = acc[c] * inv * g_ref[:, c * LANES:(c + 1) * LANES]

    @pl.when(i == 0)
    def _():
        for j in range(COMBINE_AHEAD):
            request(j)

    for j in range(COMBINE_PARTS):
        wait(j)
        if j + COMBINE_AHEAD < COMBINE_PARTS:
            request(j + COMBINE_AHEAD)
            reduce(j)
        else:
            @pl.when(jnp.logical_not(last))
            def _(j=j):
                request(j + COMBINE_AHEAD)
                reduce(j)

            @pl.when(last)
            def _(j=j):
                reduce(j)


COMBINE_TILE = 512


def _combine(x2, gate, pos3, yrows, g_final):
    T = x2.shape[0]
    n, _, per_tile = pos3.shape
    tm = per_tile // TOP_K
    return pl.pallas_call(
        _combine_kernel,
        grid=(n,),
        in_specs=[
            pl.BlockSpec((1, 1, tm * TOP_K), lambda i: (i, 0, 0), memory_space=pltpu.SMEM),
            pl.BlockSpec((1, 1, tm * TOP_K), lambda i: (jnp.minimum(i + 1, n - 1), 0, 0),
                         memory_space=pltpu.SMEM),
            pl.BlockSpec((tm, D_MODEL), lambda i: (i, 0)),
            pl.BlockSpec((tm, LANES), lambda i: (i, 0)),
            pl.BlockSpec((1, D_MODEL), lambda i: (0, 0)),
            pl.BlockSpec(memory_space=pl.ANY),
        ],
        out_specs=pl.BlockSpec((tm, D_MODEL), lambda i: (i, 0)),
        out_shape=jax.ShapeDtypeStruct((T, D_MODEL), F32),
        scratch_shapes=([pltpu.VMEM((TOP_K, tm // COMBINE_PARTS * ROW_TILES, LANES), F32)] * COMBINE_PARTS
                        + [pltpu.SemaphoreType.DMA((COMBINE_PARTS,))]),
        compiler_params=_params(("arbitrary",)),
        name="combine",
    )(pos3, pos3, x2, gate, g_final, yrows)


def kernel(x, mem, g_mix, w_in, w_sb_o, w_ret_o, w_mix_out, g_xattn, g_mem, w_cq, w_ckv, w_co,
           g_moe, w_router, b_router, w_exp_in, b_exp_in, w_exp_out, b_exp_out, g_final):
    B, S, _ = x.shape
    T = B * S
    assert w_in.shape[0] == 1, "one layer"
    row = lambda v: v.reshape(1, -1)

    w_in_k = jnp.concatenate([w_in[0][:, o:o + w] for _, o, w in _IN_GROUPS], axis=1).astype(BF16)
    proj = _inproj(x.reshape(T, D_MODEL), row(g_mix[0]), w_in_k)
    proj3 = proj.reshape(B, S, IN_W)
    y_sb = _sb_attention(proj3)
    y_ret = _retention(proj3)
    kv = _memkv(mem, row(g_mem[0]), w_ckv[0].astype(BF16))
    wr_hi = w_router[0].astype(BF16)
    wr_lo = (w_router[0] - wr_hi.astype(F32)).astype(BF16)
    x2, h3, idx, gate, rank, cnt = _mix(
        x, proj3, y_sb, y_ret, kv,
        w_sb_o[0].astype(BF16), w_ret_o[0].astype(BF16), w_mix_out[0].astype(BF16),
        w_cq[0].astype(BF16), w_co[0].astype(BF16),
        row(g_xattn[0]), row(g_moe[0]), jnp.concatenate([wr_hi, wr_lo], axis=1), row(b_router[0]))

    tb = FFN_BLOCK
    n_blocks = (T * TOP_K + N_EXPERTS * (tb - 1) + tb - 1) // tb
    counts = cnt[0].astype(jnp.int32)
    padded = ((counts + tb - 1) // tb) * tb
    pad_end = jnp.cumsum(padded)
    pad_start = pad_end - padded
    idx4 = idx.reshape(T, LANES)[:, :TOP_K]
    start4 = jnp.sum(jnp.where(idx4[:, :, None] == jnp.arange(N_EXPERTS), pad_start, 0), axis=-1)
    pos = (start4 + rank.reshape(T, LANES)[:, :TOP_K]).astype(jnp.int32).reshape(T * TOP_K)
    n_real = (pad_end[-1:] // tb).astype(jnp.int32)
    first_block = (pad_start // tb).astype(jnp.int32)
    block_count = (padded // tb).astype(jnp.int32)

    pad_lo = (pad_start + counts).astype(jnp.int32)
    pad_hi = jnp.where(jnp.arange(N_EXPERTS) == N_EXPERTS - 1, n_blocks * tb, pad_end).astype(jnp.int32)
    tile = min(COMBINE_TILE, T)
    pos3 = pos.reshape(T // tile, 1, tile * TOP_K)
    xin = _dispatch(h3.reshape(T * ROW_TILES, LANES), pos3, pad_lo, pad_hi, n_blocks * tb)
    yrows = _ffn(xin, first_block, block_count, n_real,
                 w_exp_in[0], b_exp_in[0], w_exp_out[0], b_exp_out[0])
    out = _combine(x2.reshape(T, D_MODEL), gate.reshape(T, LANES), pos3, yrows, row(g_final))
    return out.reshape(B, S, D_MODEL)
```

```python
import functools

import numpy as np
import jax
import jax.numpy as jnp
from jax import lax
from jax.experimental import pallas as pl
from jax.experimental.pallas import tpu as pltpu

F32 = jnp.float32
BF16 = jnp.bfloat16

D_MODEL = 1024
SB_HEADS = 8
SB_HEAD_DIM = 64
RET_HEADS = 8
RET_QK_DIM = 64
RET_V_DIM = 128
ROPE_BASE = 10000.0
X_HEADS = 4
X_HEAD_DIM = 256
N_EXPERTS = 32
TOP_K = 4
D_FF = 1024
SWIGLU_LIMIT = 7.0
SWIGLU_ALPHA = 1.702
NORM_EPS = 1e-5

_IN_GROUPS = (("r_v", 2560, 1024), ("r_g", 3584, 1024), ("g_sb", 4608, 1024), ("g_ret", 5632, 1024),
              ("sb_q", 0, 512), ("sb_k", 512, 512), ("sb_v", 1024, 512),
              ("r_q", 1536, 512), ("r_k", 2048, 512))
OFF_R_V, OFF_R_G, OFF_G_SB, OFF_G_RET = 0, 1024, 2048, 3072
OFF_SB_Q, OFF_SB_K, OFF_SB_V, OFF_R_Q, OFF_R_K = 4096, 4608, 5120, 5632, 6144
IN_W = 6656

LANES = 128
VMEM_LIMIT = 56 * 1024 * 1024

SB_TILE = 128
SB_LOG_UNDERFLOW = -88.0
RET_CHUNK = 256
FFN_BLOCK = 128
FFN_GROUP = 4
FFN_SLOTS = 12

def _rms(x, g):
    return x * lax.rsqrt(jnp.mean(x * x, axis=-1, keepdims=True) + NORM_EPS) * g


def _logistic(v):
    return 0.5 * jnp.tanh(0.5 * v) + 0.5


def _params(sem):
    return pltpu.CompilerParams(dimension_semantics=sem, vmem_limit_bytes=VMEM_LIMIT)


INPROJ_COLS = 512


def _inproj_kernel(x_ref, g_ref, w_ref, o_ref):
    h = _rms(x_ref[...], g_ref[...]).astype(BF16)
    for j in range(IN_W // INPROJ_COLS):
        cols = slice(j * INPROJ_COLS, (j + 1) * INPROJ_COLS)
        o_ref[:, cols] = jnp.dot(h, w_ref[:, cols], preferred_element_type=F32).astype(o_ref.dtype)


def _inproj(x2d, g, w_bf16):
    T = x2d.shape[0]
    tm = min(512, T)
    return pl.pallas_call(
        _inproj_kernel,
        grid=(T // tm,),
        in_specs=[
            pl.BlockSpec((tm, D_MODEL), lambda i: (i, 0)),
            pl.BlockSpec((1, D_MODEL), lambda i: (0, 0)),
            pl.BlockSpec((D_MODEL, IN_W), lambda i: (0, 0)),
        ],
        out_specs=pl.BlockSpec((tm, IN_W), lambda i: (i, 0)),
        out_shape=jax.ShapeDtypeStruct((T, IN_W), BF16),
        compiler_params=_params(("parallel",)),
        name="inproj",
    )(x2d, g, w_bf16)


SB_PAIRS = SB_HEADS * SB_HEAD_DIM // LANES


def _sb_kernel(q_ref, k_ref, v_ref, o_ref, q2_ref, *state_refs):
    carry_refs, acc_refs = state_refs[:SB_PAIRS], state_refs[SB_PAIRS:]
    n = SB_TILE
    qi = pl.program_id(1)
    lane = lax.broadcasted_iota(jnp.int32, (n, LANES), 1)
    first = lane < SB_HEAD_DIM
    row = lax.broadcasted_iota(jnp.int32, (2 * n, n), 0) & (n - 1)
    col = lax.broadcasted_iota(jnp.int32, (2 * n, n), 1)
    strict = col < row
    jj = lax.broadcasted_iota(jnp.int32, (2 * n, 2 * n), 0) & (n - 1)
    cc = lax.broadcasted_iota(jnp.int32, (2 * n, 2 * n), 1)
    suffix = jnp.where((cc >= n) | (jj >= cc), 1.0, 0.0).astype(BF16)
    scale = SB_HEAD_DIM ** -0.5

    for p in range(SB_PAIRS):
        qs = (q_ref[:, p * LANES:(p + 1) * LANES].astype(F32) * scale).astype(BF16)
        zeros = jnp.zeros_like(qs)
        q2_ref[p * 2 * n:p * 2 * n + n, :] = jnp.where(first, qs, zeros)
        q2_ref[p * 2 * n + n:(p + 1) * 2 * n, :] = jnp.where(first, zeros, qs)

    def step(kt, diagonal):
        ks = pl.multiple_of(kt * n, n)
        zs, hls, css, ws, cmaxs = {}, {}, {}, {}, []

        def logits(p):
            kk = k_ref[pl.ds(ks, n), p * LANES:(p + 1) * LANES]
            z = lax.dot_general(q2_ref[p * 2 * n:(p + 1) * 2 * n, :], kk, (((1,), (1,)), ((), ())),
                                preferred_element_type=F32)
            sp = jnp.maximum(z, 0.0) + jnp.log(1.0 + jnp.exp(-jnp.abs(z)))
            if diagonal:
                sp = jnp.where(strict, sp, 0.0)
            hi = sp.astype(BF16)
            lo = (sp - hi.astype(F32)).astype(BF16)
            zs[p] = z
            hls[p] = jnp.concatenate([hi, lo], axis=1)

        def sums(p):
            css[p] = jnp.dot(hls[p], suffix, preferred_element_type=F32)

        def weights(p):
            logw = zs[p] - css[p][:, :n]
            if not diagonal:
                logw = logw + carry_refs[p][...]
            w = jnp.exp(logw)
            if diagonal:
                w = jnp.where(strict, w, 0.0)
            ws[p] = w.astype(BF16)

        def values(p):
            vv = v_ref[pl.ds(ks, n), p * LANES:(p + 1) * LANES]
            pv = jnp.dot(ws[p], vv, preferred_element_type=F32)
            acc_refs[p][...] = pv if diagonal else acc_refs[p][...] + pv
            carry = -css[p][:, n:] if diagonal else carry_refs[p][...] - css[p][:, n:]
            carry_refs[p][...] = carry
            cmaxs.append(jnp.max(carry))

        for stage in (logits, sums, weights, values):
            for p in range(SB_PAIRS):
                stage(p)
        return functools.reduce(jnp.maximum, cmaxs)

    def cond(st):
        kt, cmax = st
        return jnp.logical_and(kt >= 0, cmax > SB_LOG_UNDERFLOW)

    def body(st):
        kt, _ = st
        return kt - 1, step(kt, False)

    lax.while_loop(cond, body, (qi - 1, step(qi, True)))

    for p in range(SB_PAIRS):
        o_ref[:, p * LANES:(p + 1) * LANES] = jnp.where(
            first, acc_refs[p][:n, :], acc_refs[p][n:, :]
        ).astype(o_ref.dtype)


def _sb_attention(proj3):
    B, S, _ = proj3.shape
    n = SB_TILE
    w = SB_HEADS * SB_HEAD_DIM
    qb, kb, vb = OFF_SB_Q // w, OFF_SB_K // w, OFF_SB_V // w
    rows = SB_PAIRS * 2 * n
    return pl.pallas_call(
        _sb_kernel,
        grid=(B, S // n),
        in_specs=[
            pl.BlockSpec((None, n, w), lambda b, i: (b, i, qb)),
            pl.BlockSpec((None, S, w), lambda b, i: (b, 0, kb)),
            pl.BlockSpec((None, S, w), lambda b, i: (b, 0, vb)),
        ],
        out_specs=pl.BlockSpec((None, n, w), lambda b, i: (b, i, 0)),
        out_shape=jax.ShapeDtypeStruct((B, S, w), BF16),
        scratch_shapes=([pltpu.VMEM((rows, LANES), BF16)]
                        + [pltpu.VMEM((2 * n, LANES), F32)] * (2 * SB_PAIRS)),
        compiler_params=_params(("parallel", "arbitrary")),
        name="sb",
    )(proj3, proj3, proj3)


def _ret_tables(S):
    c = RET_CHUNK
    f = np.float32
    inv = f(ROPE_BASE) ** (-np.arange(0, RET_QK_DIM, 2, dtype=f) / f(RET_QK_DIM))
    ang = np.arange(S, dtype=f)[:, None] * inv[None, :]
    cos, sin = np.cos(ang), np.sin(ang)
    reps = LANES // RET_QK_DIM
    cos_t = np.tile(np.concatenate([cos, cos], axis=1), (1, reps))
    sin_t = np.tile(np.concatenate([-sin, sin], axis=1), (1, reps))
    log_g = np.log1p(-np.exp2(-5.0 - np.arange(RET_HEADS, dtype=np.float64)))
    pos = np.arange(c, dtype=np.float64)
    diff = pos[:, None] - pos[None, :]
    decay = np.where(diff >= 0, np.exp(log_g[:, None, None] * np.maximum(diff, 0.0)), 0.0).astype(f)
    q_dec = np.exp(log_g[:, None] * (pos + 1.0)[None, :]).astype(f)
    k_dec = np.exp(log_g[:, None] * (c - 1 - pos)[None, :]).astype(f)
    q_dec = np.ascontiguousarray(np.broadcast_to(q_dec[:, :, None], (RET_HEADS, c, LANES)))
    k_dec = np.ascontiguousarray(np.broadcast_to(k_dec[:, :, None], (RET_HEADS, c, LANES)))
    c_dec = np.ascontiguousarray(np.broadcast_to(
        np.exp(log_g * c).astype(f)[:, None, None], (RET_HEADS, LANES, LANES)))
    return cos_t, sin_t, decay, q_dec, k_dec, c_dec


def _ret_kernel(q_ref, k_ref, v_ref, g_ref, cos_ref, sin_ref, dm_ref, qd_ref, kd_ref, cd_ref,
                o_ref, state):
    c = RET_CHUNK

    @pl.when(pl.program_id(1) == 0)
    def _():
        state[...] = jnp.zeros_like(state)

    lane = lax.broadcasted_iota(jnp.int32, (c, LANES), 1)
    low_half = (lane & (RET_QK_DIM - 1)) < RET_QK_DIM // 2
    cos = cos_ref[...]
    sin = sin_ref[...]

    def rotary(x):
        swapped = jnp.where(low_half, pltpu.roll(x, LANES - RET_QK_DIM // 2, 1),
                            pltpu.roll(x, RET_QK_DIM // 2, 1))
        return x * cos + swapped * sin

    heads = range(RET_HEADS)
    cols = [slice(h * RET_V_DIM, (h + 1) * RET_V_DIM) for h in heads]
    rq, rk, qh, sc, out = {}, {}, {}, {}, {}

    for p in range(RET_HEADS * RET_QK_DIM // LANES):
        rq[p] = rotary(q_ref[:, p * LANES:(p + 1) * LANES].astype(F32))
        rk[p] = rotary(k_ref[:, p * LANES:(p + 1) * LANES].astype(F32)) * RET_QK_DIM ** -0.5
    for h in heads:
        in_head = (lane < RET_QK_DIM) if h % 2 == 0 else (lane >= RET_QK_DIM)
        qh[h] = jnp.where(in_head, rq[h // 2], 0.0)
        sc[h] = lax.dot_general(qh[h].astype(BF16), rk[h // 2].astype(BF16), (((1,), (1,)), ((), ())),
                                preferred_element_type=F32) * dm_ref[h]
    for h in heads:
        vh = v_ref[:, cols[h]]
        o = jnp.dot(sc[h].astype(BF16), vh, preferred_element_type=F32)
        out[h] = o + jnp.dot((qh[h] * qd_ref[h]).astype(BF16), state[h].astype(BF16),
                             preferred_element_type=F32)
        kv_new = lax.dot_general((rk[h // 2] * kd_ref[h]).astype(BF16), vh, (((0,), (0,)), ((), ())),
                                 preferred_element_type=F32)
        state[h] = state[h] * cd_ref[h] + kv_new
    for h in heads:
        o = out[h]
        mu = jnp.mean(o, axis=-1, keepdims=True)
        d = o - mu
        y = d * lax.rsqrt(jnp.mean(d * d, axis=-1, keepdims=True) + NORM_EPS)
        g = g_ref[:, cols[h]].astype(F32)
        o_ref[:, cols[h]] = (g * _logistic(g) * y).astype(o_ref.dtype)


def _retention(proj3):
    B, S, _ = proj3.shape
    c = RET_CHUNK
    qkw, vw = RET_HEADS * RET_QK_DIM, RET_HEADS * RET_V_DIM
    qb, kb, vb, gb = OFF_R_Q // qkw, OFF_R_K // qkw, OFF_R_V // vw, OFF_R_G // vw
    cos_t, sin_t, decay, q_dec, k_dec, c_dec = _ret_tables(S)
    const3 = lambda b, i: (0, 0, 0)
    return pl.pallas_call(
        _ret_kernel,
        grid=(B, S // c),
        in_specs=[
            pl.BlockSpec((None, c, qkw), lambda b, i: (b, i, qb)),
            pl.BlockSpec((None, c, qkw), lambda b, i: (b, i, kb)),
            pl.BlockSpec((None, c, vw), lambda b, i: (b, i, vb)),
            pl.BlockSpec((None, c, vw), lambda b, i: (b, i, gb)),
            pl.BlockSpec((c, LANES), lambda b, i: (i, 0)),
            pl.BlockSpec((c, LANES), lambda b, i: (i, 0)),
            pl.BlockSpec((RET_HEADS, c, c), const3),
            pl.BlockSpec((RET_HEADS, c, LANES), const3),
            pl.BlockSpec((RET_HEADS, c, LANES), const3),
            pl.BlockSpec((RET_HEADS, LANES, LANES), const3),
        ],
        out_specs=pl.BlockSpec((None, c, vw), lambda b, i: (b, i, 0)),
        out_shape=jax.ShapeDtypeStruct((B, S, vw), BF16),
        scratch_shapes=[pltpu.VMEM((RET_HEADS, LANES, LANES), F32)],
        compiler_params=_params(("parallel", "arbitrary")),
        name="ret",
    )(proj3, proj3, proj3, proj3, cos_t, sin_t, decay, q_dec, k_dec, c_dec)


def _memkv_kernel(m_ref, g_ref, w_ref, o_ref):
    h = _rms(m_ref[...], g_ref[...]).astype(BF16)
    o_ref[...] = jnp.dot(h, w_ref[...], preferred_element_type=F32).astype(o_ref.dtype)


def _memkv(mem, g, w_bf16):
    B, M, _ = mem.shape
    return pl.pallas_call(
        _memkv_kernel,
        grid=(B,),
        in_specs=[
            pl.BlockSpec((None, M, D_MODEL), lambda b: (b, 0, 0)),
            pl.BlockSpec((1, D_MODEL), lambda b: (0, 0)),
            pl.BlockSpec((D_MODEL, 2 * D_MODEL), lambda b: (0, 0)),
        ],
        out_specs=pl.BlockSpec((None, M, 2 * D_MODEL), lambda b: (b, 0, 0)),
        out_shape=jax.ShapeDtypeStruct((B, M, 2 * D_MODEL), BF16),
        compiler_params=_params(("parallel",)),
        name="memkv",
    )(mem, g, w_bf16)


MIX_CHUNKS = 2


def _lane_pack(cols, rows, dtype):
    lane = lax.broadcasted_iota(jnp.int32, (rows, LANES), 1)
    out = jnp.zeros((rows, LANES), dtype)
    for k, c in enumerate(cols):
        out = jnp.where(lane == k, c.astype(dtype), out)
    return out


def _mix_kernel(x_ref, ysb_ref, yret_ref, gsb_ref, gret_ref, kv_ref,
                wsbo_ref, wreto_ref, wmix_ref, wcq_ref, wco_ref,
                gx_ref, gm_ref, wr_ref, br_ref,
                x2_ref, h3_ref, idx_ref, gate_ref, rank_ref, cnt_ref, cnt_scr):
    tm = x_ref.shape[0]
    cr = tm // MIX_CHUNKS
    chunks = [slice(c * cr, (c + 1) * cr) for c in range(MIX_CHUNKS)]
    dot = functools.partial(jnp.dot, preferred_element_type=F32)

    @pl.when(jnp.logical_and(pl.program_id(0) == 0, pl.program_id(1) == 0))
    def _():
        cnt_scr[...] = jnp.zeros_like(cnt_scr)

    def sigmoid(g):
        return _logistic(g).astype(F32)

    a = [dot(ysb_ref[s, :], wsbo_ref[...]) for s in chunks]
    b = [dot(yret_ref[s, :], wreto_ref[...]) for s in chunks]
    merged = [sigmoid(gsb_ref[s, :]) * ac + sigmoid(gret_ref[s, :]) * bc
              for s, ac, bc in zip(chunks, a, b)]
    x1 = [x_ref[s, :] + dot(m.astype(BF16), wmix_ref[...]) for s, m in zip(chunks, merged)]

    h2 = [_rms(v, gx_ref[...]).astype(BF16) for v in x1]
    q = [(dot(v, wcq_ref[...]) * X_HEAD_DIM ** -0.5).astype(BF16) for v in h2]
    heads = [[] for _ in chunks]
    for hh in range(X_HEADS):
        lo = hh * X_HEAD_DIM
        kh = kv_ref[:, lo:lo + X_HEAD_DIM]
        vh = kv_ref[:, D_MODEL + lo:D_MODEL + lo + X_HEAD_DIM]
        s = [lax.dot_general(qc[:, lo:lo + X_HEAD_DIM], kh, (((1,), (1,)), ((), ())),
                             preferred_element_type=F32) for qc in q]
        p = [jnp.exp(sc - jnp.max(sc, axis=-1, keepdims=True)) for sc in s]
        p = [pc * (1.0 / jnp.sum(pc, axis=-1, keepdims=True)) for pc in p]
        for c, pc in enumerate(p):
            heads[c].append(dot(pc.astype(BF16), vh).astype(BF16))
    o = [jnp.concatenate(hc, axis=1) for hc in heads]
    x2 = [v + dot(oc, wco_ref[...]) for v, oc in zip(x1, o)]
    for s, v in zip(chunks, x2):
        x2_ref[s, :] = v

    h3 = [_rms(v, gm_ref[...]) for v in x2]
    for ci, v in enumerate(h3):
        for c in range(ROW_TILES):
            h3_ref[pl.ds(ci * cr * ROW_TILES + c, cr, stride=ROW_TILES), :] = v[:, c * LANES:(c + 1) * LANES]
    logits = []
    for v in h3:
        hi = v.astype(BF16)
        lo = (v - hi.astype(F32)).astype(BF16)
        r = dot(jnp.concatenate([hi, lo], axis=0), wr_ref[...])
        logits.append(r[:cr, :N_EXPERTS] + r[:cr, N_EXPERTS:] + r[cr:, :N_EXPERTS] + br_ref[...])

    e_iota = lax.broadcasted_iota(jnp.int32, (cr, N_EXPERTS), 1).astype(F32)
    r_i = lax.broadcasted_iota(jnp.int32, (cr, cr), 0)
    c_i = lax.broadcasted_iota(jnp.int32, (cr, cr), 1)
    before = jnp.where(c_i < r_i, 1.0, 0.0).astype(BF16)
    count = cnt_scr[...]
    for s, rem in zip(chunks, logits):
        vals, idxs = [], []
        for _ in range(TOP_K):
            m = jnp.max(rem, axis=-1, keepdims=True)
            ik = jnp.min(jnp.where(rem == m, e_iota, float(N_EXPERTS)), axis=-1, keepdims=True)
            vals.append(m)
            idxs.append(ik)
            rem = jnp.where(e_iota == ik, -jnp.inf, rem)
        ex = [jnp.exp(v - vals[0]) for v in vals]
        inv_den = 1.0 / (ex[0] + ex[1] + ex[2] + ex[3])
        gates = [e * inv_den for e in ex]

        chosen = [(e_iota == ik) for ik in idxs]
        member = jnp.zeros((cr, N_EXPERTS), F32)
        for ch in chosen:
            member = member + jnp.where(ch, 1.0, 0.0)
        prefix = dot(before, member.astype(BF16)) + count
        ranks = [jnp.sum(jnp.where(ch, prefix, 0.0), axis=-1, keepdims=True) for ch in chosen]
        count = count + jnp.sum(member, axis=0, keepdims=True)

        idx_ref[s, :] = _lane_pack(idxs, cr, jnp.int32)
        rank_ref[s, :] = _lane_pack(ranks, cr, jnp.int32)
        gate_ref[s, :] = _lane_pack(gates, cr, F32)
    cnt_scr[...] = count
    cnt_ref[...] = count


def _mix(x, proj3, y_sb, y_ret, kv, wsbo, wreto, wmix, wcq, wco, gx, gm, wr, br):
    B, S, _ = x.shape
    tm = min(512, S)
    M = kv.shape[1]
    gsb_b, gret_b = OFF_G_SB // D_MODEL, OFF_G_RET // D_MODEL
    tok = lambda b, i: (b, i, 0)
    const2 = lambda b, i: (0, 0)
    row_out = lambda w, dt: jax.ShapeDtypeStruct((B, S, w), dt)
    outs = pl.pallas_call(
        _mix_kernel,
        grid=(B, S // tm),
        in_specs=[
            pl.BlockSpec((None, tm, D_MODEL), tok),
            pl.BlockSpec((None, tm, SB_HEADS * SB_HEAD_DIM), tok),
            pl.BlockSpec((None, tm, D_MODEL), tok),
            pl.BlockSpec((None, tm, D_MODEL), lambda b, i: (b, i, gsb_b)),
            pl.BlockSpec((None, tm, D_MODEL), lambda b, i: (b, i, gret_b)),
            pl.BlockSpec((None, M, 2 * D_MODEL), lambda b, i: (b, 0, 0)),
            pl.BlockSpec((SB_HEADS * SB_HEAD_DIM, D_MODEL), const2),
            pl.BlockSpec((D_MODEL, D_MODEL), const2),
            pl.BlockSpec((D_MODEL, D_MODEL), const2),
            pl.BlockSpec((D_MODEL, D_MODEL), const2),
            pl.BlockSpec((D_MODEL, D_MODEL), const2),
            pl.BlockSpec((1, D_MODEL), const2),
            pl.BlockSpec((1, D_MODEL), const2),
            pl.BlockSpec((D_MODEL, 2 * N_EXPERTS), const2),
            pl.BlockSpec((1, N_EXPERTS), const2),
        ],
        out_specs=[
            pl.BlockSpec((None, tm, D_MODEL), tok),
            pl.BlockSpec((None, tm * ROW_TILES, LANES), tok),
            pl.BlockSpec((None, tm, LANES), tok),
            pl.BlockSpec((None, tm, LANES), tok),
            pl.BlockSpec((None, tm, LANES), tok),
            pl.BlockSpec((1, N_EXPERTS), const2),
        ],
        out_shape=[
            row_out(D_MODEL, F32), jax.ShapeDtypeStruct((B, S * ROW_TILES, LANES), F32),
            row_out(LANES, jnp.int32), row_out(LANES, F32), row_out(LANES, jnp.int32),
            jax.ShapeDtypeStruct((1, N_EXPERTS), F32),
        ],
        scratch_shapes=[pltpu.VMEM((1, N_EXPERTS), F32)],
        compiler_params=_params(("arbitrary", "arbitrary")),
        name="mix",
    )(x, y_sb, y_ret, proj3, proj3, kv, wsbo, wreto, wmix, wcq, wco, gx, gm, wr, br)
    return outs


ROW_TILES = D_MODEL // LANES


def _token_tile_copy(src, s, dst, d, sem):
    rows = lambda i: pl.ds(pl.multiple_of(i * ROW_TILES, ROW_TILES), ROW_TILES)
    return pltpu.make_async_copy(src.at[rows(s)], dst.at[rows(d)], sem)


DMA_UNROLL = 16


ZERO_ROWS = 256


def _dispatch_kernel(lo_ref, hi_ref, pos_ref, h_ref, xin_ref, zbuf, sem, zsem):
    tm = h_ref.shape[0] // ROW_TILES

    def tokens(start, count):
        return pl.ds(pl.multiple_of(start * ROW_TILES, ROW_TILES), count * ROW_TILES)

    def zero_fill(act):
        def per_expert(e, carry):
            lo = lo_ref[e]
            hi = hi_ref[e]

            def chunk(i, off):
                act(pltpu.make_async_copy(zbuf, xin_ref.at[tokens(off, ZERO_ROWS)], zsem))
                return off + ZERO_ROWS

            off = lax.fori_loop(0, (hi - lo) // ZERO_ROWS, chunk, lo)
            rest = hi - off
            size = ZERO_ROWS // 2
            while size >= 1:
                @pl.when((rest & size) != 0)
                def _(off=off, size=size):
                    act(pltpu.make_async_copy(zbuf.at[tokens(0, size)], xin_ref.at[tokens(off, size)],
                                              zsem))
                off = off + (rest & size)
                size //= 2
            return carry

        lax.fori_loop(0, N_EXPERTS, per_expert, 0)

    @pl.when(pl.program_id(0) == 0)
    def _():
        zbuf[...] = jnp.zeros_like(zbuf)
        zero_fill(lambda cp: cp.start())
        zero_fill(lambda cp: cp.wait())

    parts, _, per_part = pos_ref.shape
    for part in range(parts):
        def issue(t, carry, part=part):
            for k in range(TOP_K):
                _token_tile_copy(h_ref, part * (per_part // TOP_K) + t, xin_ref,
                                 pos_ref[part, 0, t * TOP_K + k], sem).start(priority=k % 2)
            return carry

        lax.fori_loop(0, per_part // TOP_K, issue, 0, unroll=DMA_UNROLL)
    for _ in range(TOP_K):
        pltpu.make_async_copy(h_ref, xin_ref.at[tokens(0, tm)], sem).wait()


def _dispatch(h3_tm, pos3, pad_lo, pad_hi, n_rows):
    T = h3_tm.shape[0] // ROW_TILES
    per_tile = pos3.shape[2] // TOP_K
    tm = min(max(1024, per_tile), T)
    parts = tm // per_tile
    grid_spec = pltpu.PrefetchScalarGridSpec(
        num_scalar_prefetch=2,
        grid=(T // tm,),
        in_specs=[
            pl.BlockSpec((parts, 1, per_tile * TOP_K), lambda i, lo, hi: (i, 0, 0),
                         memory_space=pltpu.SMEM),
            pl.BlockSpec((tm * ROW_TILES, LANES), lambda i, lo, hi: (i, 0)),
        ],
        out_specs=pl.BlockSpec(memory_space=pl.ANY),
        scratch_shapes=[pltpu.VMEM((ZERO_ROWS * ROW_TILES, LANES), F32), pltpu.SemaphoreType.DMA(()),
                        pltpu.SemaphoreType.DMA(())],
    )
    return pl.pallas_call(
        _dispatch_kernel,
        grid_spec=grid_spec,
        out_shape=jax.ShapeDtypeStruct((n_rows * ROW_TILES, LANES), F32),
        compiler_params=_params(("arbitrary",)),
        name="dispatch",
    )(pad_lo, pad_hi, pos3, h3_tm)


def _ffn_kernel(first_ref, count_ref, nreal_ref, x_hbm, w1_ref, b1_ref, w2_ref, b2_ref, y_hbm,
                xbuf, ybuf, w1b, w2b, in_sem, out_sem):
    e = pl.program_id(0)
    tb = FFN_BLOCK
    ahead = FFN_SLOTS - FFN_GROUP
    n_real = nreal_ref[0]

    span = tb * ROW_TILES

    def rows(g):
        return pl.ds(pl.multiple_of(g * span, span), span)

    def x_copy(g):
        slot = g % FFN_SLOTS
        return pltpu.make_async_copy(x_hbm.at[rows(g)], xbuf.at[slot], in_sem.at[slot])

    def y_copy(g):
        slot = g % FFN_SLOTS
        return pltpu.make_async_copy(ybuf.at[slot], y_hbm.at[rows(g)], out_sem.at[slot])

    @pl.when(e == 0)
    def _():
        for g in range(ahead):
            @pl.when(g < n_real)
            def _(g=g):
                x_copy(g).start(priority=1)

    @pl.when(count_ref[e] > 0)
    def _():
        w1b[...] = w1_ref[...].astype(BF16)
        w2b[...] = w2_ref[...].astype(BF16)

    def blocks(g0, m):
        group = [g0 + i for i in range(m)]
        for g in group:
            x_copy(g).wait()
        for g in group:
            @pl.when(g + ahead < n_real)
            def _(g=g):
                x_copy(g + ahead).start(priority=1)

            @pl.when(g >= FFN_SLOTS)
            def _(g=g):
                y_copy(g - FFN_SLOTS).wait()

        x = jnp.concatenate(
            [jnp.concatenate([xbuf[g % FFN_SLOTS, pl.ds(c, tb, stride=ROW_TILES), :].astype(BF16)
                              for c in range(ROW_TILES)], axis=1) for g in group], axis=0)
        hc = jnp.dot(x, w1b[...], preferred_element_type=F32) + b1_ref[...]
        glu = jnp.minimum(hc[:, :D_FF], SWIGLU_LIMIT)
        lin = jnp.clip(hc[:, D_FF:], -SWIGLU_LIMIT, SWIGLU_LIMIT)
        act = glu * _logistic(SWIGLU_ALPHA * glu) * (lin + 1.0)
        y = jnp.dot(act.astype(BF16), w2b[...], preferred_element_type=F32) + b2_ref[...]
        for i, g in enumerate(group):
            for c in range(ROW_TILES):
                ybuf[g % FFN_SLOTS, pl.ds(c, tb, stride=ROW_TILES), :] = (
                    y[i * tb:(i + 1) * tb, c * LANES:(c + 1) * LANES])
            y_copy(g).start(priority=1)

    first, count = first_ref[e], count_ref[e]

    def pair(j, carry):
        blocks(first + FFN_GROUP * j, FFN_GROUP)
        return carry

    lax.fori_loop(0, count // FFN_GROUP, pair, 0)

    for left in range(1, FFN_GROUP):
        @pl.when(count % FFN_GROUP == left)
        def _(left=left):
            blocks(first + count - left, left)

    @pl.when(e == N_EXPERTS - 1)
    def _():
        for back in range(FFN_SLOTS, 0, -1):
            @pl.when(n_real >= back)
            def _(back=back):
                y_copy(n_real - back).wait()

        for s in range(FFN_SLOTS):
            ybuf[s] = jnp.zeros((tb * ROW_TILES, LANES), F32)

        def tail(g, carry):
            cp = y_copy(g)
            cp.start()
            cp.wait()
            return carry

        lax.fori_loop(n_real, y_hbm.shape[0] // (tb * ROW_TILES), tail, 0)


def _ffn(xin, first_block, block_count, n_real, w1, b1, w2, b2):
    tb = FFN_BLOCK
    expert = lambda e, first, count, nr: (e, 0, 0)
    grid_spec = pltpu.PrefetchScalarGridSpec(
        num_scalar_prefetch=3,
        grid=(N_EXPERTS,),
        in_specs=[
            pl.BlockSpec(memory_space=pl.ANY),
            pl.BlockSpec((None, D_MODEL, 2 * D_FF), expert),
            pl.BlockSpec((None, 1, 2 * D_FF), expert),
            pl.BlockSpec((None, D_FF, D_MODEL), expert),
            pl.BlockSpec((None, 1, D_MODEL), expert),
        ],
        out_specs=pl.BlockSpec(memory_space=pl.ANY),
        scratch_shapes=[pltpu.VMEM((FFN_SLOTS, tb * ROW_TILES, LANES), F32),
                        pltpu.VMEM((FFN_SLOTS, tb * ROW_TILES, LANES), F32),
                        pltpu.VMEM((D_MODEL, 2 * D_FF), BF16), pltpu.VMEM((D_FF, D_MODEL), BF16),
                        pltpu.SemaphoreType.DMA((FFN_SLOTS,)),
                        pltpu.SemaphoreType.DMA((FFN_SLOTS,))],
    )
    return pl.pallas_call(
        _ffn_kernel,
        grid_spec=grid_spec,
        out_shape=jax.ShapeDtypeStruct(xin.shape, F32),
        compiler_params=_params(("arbitrary",)),
        name="ffn",
    )(first_block, block_count, n_real, xin, w1, b1.reshape(N_EXPERTS, 1, 2 * D_FF), w2,
      b2.reshape(N_EXPERTS, 1, D_MODEL))


COMBINE_PARTS = 4
COMBINE_AHEAD = 2


def _combine_kernel(pos_ref, nxt_ref, x2_ref, gate_ref, g_ref, y_ref, o_ref, *scratch):
    bufs, sem = scratch[:COMBINE_PARTS], scratch[COMBINE_PARTS]
    tm = x2_ref.shape[0]
    sub = tm // COMBINE_PARTS
    i = pl.program_id(0)
    last = i + 1 == pl.num_programs(0)

    def request(j):
        p_ref, part = (pos_ref, j) if j < COMBINE_PARTS else (nxt_ref, j - COMBINE_PARTS)
        for t in range(sub):
            for k in range(TOP_K):
                _token_tile_copy(y_ref, p_ref[0, 0, (part * sub + t) * TOP_K + k], bufs[part].at[k], t,
                                 sem.at[part]).start(priority=k % 2)

    def wait(j):
        for k in range(TOP_K):
            pltpu.make_async_copy(y_ref.at[pl.ds(0, sub * ROW_TILES)], bufs[j].at[k], sem.at[j]).wait()

    def reduce(j):
        rows = slice(j * sub, (j + 1) * sub)
        gate = gate_ref[rows, :]
        gates = [gate[:, k:k + 1] for k in range(TOP_K)]
        acc, ssq = [], jnp.zeros((sub, 1), F32)
        for c in range(ROW_TILES):
            a = x2_ref[rows, c * LANES:(c + 1) * LANES]
            for k in range(TOP_K):
                a = a + gates[k] * bufs[j][k, pl.ds(c, sub, stride=ROW_TILES), :]
            acc.append(a)
            ssq = ssq + jnp.sum(a * a, axis=-1, keepdims=True)
        inv = lax.rsqrt(ssq * (1.0 / D_MODEL) + NORM_EPS)
        for c in range(ROW_TILES):
            o_ref[rows, c * LANES:(c + 1) * LANES] = acc[c] * inv * g_ref[:, c * LANES:(c + 1) * LANES]

    @pl.when(i == 0)
    def _():
        for j in range(COMBINE_AHEAD):
            request(j)

    for j in range(COMBINE_PARTS):
        wait(j)
        if j + COMBINE_AHEAD < COMBINE_PARTS:
            request(j + COMBINE_AHEAD)
            reduce(j)
        else:
            @pl.when(jnp.logical_not(last))
            def _(j=j):
                request(j + COMBINE_AHEAD)
                reduce(j)

            @pl.when(last)
            def _(j=j):
                reduce(j)


COMBINE_TILE = 512


def _combine(x2, gate, pos3, yrows, g_final):
    T = x2.shape[0]
    n, _, per_tile = pos3.shape
    tm = per_tile // TOP_K
    return pl.pallas_call(
        _combine_kernel,
        grid=(n,),
        in_specs=[
            pl.BlockSpec((1, 1, tm * TOP_K), lambda i: (i, 0, 0), memory_space=pltpu.SMEM),
            pl.BlockSpec((1, 1, tm * TOP_K), lambda i: (jnp.minimum(i + 1, n - 1), 0, 0),
                         memory_space=pltpu.SMEM),
            pl.BlockSpec((tm, D_MODEL), lambda i: (i, 0)),
            pl.BlockSpec((tm, LANES), lambda i: (i, 0)),
            pl.BlockSpec((1, D_MODEL), lambda i: (0, 0)),
            pl.BlockSpec(memory_space=pl.ANY),
        ],
        out_specs=pl.BlockSpec((tm, D_MODEL), lambda i: (i, 0)),
        out_shape=jax.ShapeDtypeStruct((T, D_MODEL), F32),
        scratch_shapes=([pltpu.VMEM((TOP_K, tm // COMBINE_PARTS * ROW_TILES, LANES), F32)] * COMBINE_PARTS
                        + [pltpu.SemaphoreType.DMA((COMBINE_PARTS,))]),
        compiler_params=_params(("arbitrary",)),
        name="combine",
    )(pos3, pos3, x2, gate, g_final, yrows)


def kernel(x, mem, g_mix, w_in, w_sb_o, w_ret_o, w_mix_out, g_xattn, g_mem, w_cq, w_ckv, w_co,
           g_moe, w_router, b_router, w_exp_in, b_exp_in, w_exp_out, b_exp_out, g_final):
    B, S, _ = x.shape
    T = B * S
    assert w_in.shape[0] == 1, "one layer"
    row = lambda v: v.reshape(1, -1)

    w_in_k = jnp.concatenate([w_in[0][:, o:o + w] for _, o, w in _IN_GROUPS], axis=1).astype(BF16)
    proj = _inproj(x.reshape(T, D_MODEL), row(g_mix[0]), w_in_k)
    proj3 = proj.reshape(B, S, IN_W)
    y_sb = _sb_attention(proj3)
    y_ret = _retention(proj3)
    kv = _memkv(mem, row(g_mem[0]), w_ckv[0].astype(BF16))
    wr_hi = w_router[0].astype(BF16)
    wr_lo = (w_router[0] - wr_hi.astype(F32)).astype(BF16)
    x2, h3, idx, gate, rank, cnt = _mix(
        x, proj3, y_sb, y_ret, kv,
        w_sb_o[0].astype(BF16), w_ret_o[0].astype(BF16), w_mix_out[0].astype(BF16),
        w_cq[0].astype(BF16), w_co[0].astype(BF16),
        row(g_xattn[0]), row(g_moe[0]), jnp.concatenate([wr_hi, wr_lo], axis=1), row(b_router[0]))

    tb = FFN_BLOCK
    n_blocks = (T * TOP_K + N_EXPERTS * (tb - 1) + tb - 1) // tb
    counts = cnt[0].astype(jnp.int32)
    padded = ((counts + tb - 1) // tb) * tb
    pad_end = jnp.cumsum(padded)
    pad_start = pad_end - padded
    idx4 = idx.reshape(T, LANES)[:, :TOP_K]
    start4 = jnp.sum(jnp.where(idx4[:, :, None] == jnp.arange(N_EXPERTS), pad_start, 0), axis=-1)
    pos = (start4 + rank.reshape(T, LANES)[:, :TOP_K]).astype(jnp.int32).reshape(T * TOP_K)
    n_real = (pad_end[-1:] // tb).astype(jnp.int32)
    first_block = (pad_start // tb).astype(jnp.int32)
    block_count = (padded // tb).astype(jnp.int32)

    pad_lo = (pad_start + counts).astype(jnp.int32)
    pad_hi = jnp.where(jnp.arange(N_EXPERTS) == N_EXPERTS - 1, n_blocks * tb, pad_end).astype(jnp.int32)
    tile = min(COMBINE_TILE, T)
    pos3 = pos.reshape(T // tile, 1, tile * TOP_K)
    xin = _dispatch(h3.reshape(T * ROW_TILES, LANES), pos3, pad_lo, pad_hi, n_blocks * tb)
    yrows = _ffn(xin, first_block, block_count, n_real,
                 w_exp_in[0], b_exp_in[0], w_exp_out[0], b_exp_out[0])
    out = _combine(x2.reshape(T, D_MODEL), gate.reshape(T, LANES), pos3, yrows, row(g_final))
    return out.reshape(B, S, D_MODEL)
```

```python
import functools

import numpy as np
import jax
import jax.numpy as jnp
from jax import lax
from jax.experimental import pallas as pl
from jax.experimental.pallas import tpu as pltpu

F32 = jnp.float32
BF16 = jnp.bfloat16

D_MODEL = 1024
SB_HEADS = 8
SB_HEAD_DIM = 64
RET_HEADS = 8
RET_QK_DIM = 64
RET_V_DIM = 128
ROPE_BASE = 10000.0
X_HEADS = 4
X_HEAD_DIM = 256
N_EXPERTS = 32
TOP_K = 4
D_FF = 1024
SWIGLU_LIMIT = 7.0
SWIGLU_ALPHA = 1.702
NORM_EPS = 1e-5

_IN_GROUPS = (("r_v", 2560, 1024), ("r_g", 3584, 1024), ("g_sb", 4608, 1024), ("g_ret", 5632, 1024),
              ("sb_q", 0, 512), ("sb_k", 512, 512), ("sb_v", 1024, 512),
              ("r_q", 1536, 512), ("r_k", 2048, 512))
OFF_R_V, OFF_R_G, OFF_G_SB, OFF_G_RET = 0, 1024, 2048, 3072
OFF_SB_Q, OFF_SB_K, OFF_SB_V, OFF_R_Q, OFF_R_K = 4096, 4608, 5120, 5632, 6144
IN_W = 6656

LANES = 128
VMEM_LIMIT = 56 * 1024 * 1024

SB_TILE = 128
SB_LOG_UNDERFLOW = -88.0
RET_CHUNK = 256
FFN_BLOCK = 128
FFN_GROUP = 4
FFN_SLOTS = 12

def _rms(x, g):
    return x * lax.rsqrt(jnp.mean(x * x, axis=-1, keepdims=True) + NORM_EPS) * g


def _logistic(v):
    return 0.5 * jnp.tanh(0.5 * v) + 0.5


def _params(sem):
    return pltpu.CompilerParams(dimension_semantics=sem, vmem_limit_bytes=VMEM_LIMIT)


INPROJ_COLS = 512


def _inproj_kernel(x_ref, g_ref, w_ref, o_ref):
    h = _rms(x_ref[...], g_ref[...]).astype(BF16)
    for j in range(IN_W // INPROJ_COLS):
        cols = slice(j * INPROJ_COLS, (j + 1) * INPROJ_COLS)
        o_ref[:, cols] = jnp.dot(h, w_ref[:, cols], preferred_element_type=F32).astype(o_ref.dtype)


def _inproj(x2d, g, w_bf16):
    T = x2d.shape[0]
    tm = min(512, T)
    return pl.pallas_call(
        _inproj_kernel,
        grid=(T // tm,),
        in_specs=[
            pl.BlockSpec((tm, D_MODEL), lambda i: (i, 0)),
            pl.BlockSpec((1, D_MODEL), lambda i: (0, 0)),
            pl.BlockSpec((D_MODEL, IN_W), lambda i: (0, 0)),
        ],
        out_specs=pl.BlockSpec((tm, IN_W), lambda i: (i, 0)),
        out_shape=jax.ShapeDtypeStruct((T, IN_W), BF16),
        compiler_params=_params(("parallel",)),
        name="inproj",
    )(x2d, g, w_bf16)


SB_PAIRS = SB_HEADS * SB_HEAD_DIM // LANES


def _sb_kernel(q_ref, k_ref, v_ref, o_ref, q2_ref, *state_refs):
    carry_refs, acc_refs = state_refs[:SB_PAIRS], state_refs[SB_PAIRS:]
    n = SB_TILE
    qi = pl.program_id(1)
    lane = lax.broadcasted_iota(jnp.int32, (n, LANES), 1)
    first = lane < SB_HEAD_DIM
    row = lax.broadcasted_iota(jnp.int32, (2 * n, n), 0) & (n - 1)
    col = lax.broadcasted_iota(jnp.int32, (2 * n, n), 1)
    strict = col < row
    jj = lax.broadcasted_iota(jnp.int32, (2 * n, 2 * n), 0) & (n - 1)
    cc = lax.broadcasted_iota(jnp.int32, (2 * n, 2 * n), 1)
    suffix = jnp.where((cc >= n) | (jj >= cc), 1.0, 0.0).astype(BF16)
    scale = SB_HEAD_DIM ** -0.5

    for p in range(SB_PAIRS):
        qs = (q_ref[:, p * LANES:(p + 1) * LANES].astype(F32) * scale).astype(BF16)
        zeros = jnp.zeros_like(qs)
        q2_ref[p * 2 * n:p * 2 * n + n, :] = jnp.where(first, qs, zeros)
        q2_ref[p * 2 * n + n:(p + 1) * 2 * n, :] = jnp.where(first, zeros, qs)

    def step(kt, diagonal):
        ks = pl.multiple_of(kt * n, n)
        zs, hls, css, ws, cmaxs = {}, {}, {}, {}, []

        def logits(p):
            kk = k_ref[pl.ds(ks, n), p * LANES:(p + 1) * LANES]
            z = lax.dot_general(q2_ref[p * 2 * n:(p + 1) * 2 * n, :], kk, (((1,), (1,)), ((), ())),
                                preferred_element_type=F32)
            sp = jnp.maximum(z, 0.0) + jnp.log(1.0 + jnp.exp(-jnp.abs(z)))
            if diagonal:
                sp = jnp.where(strict, sp, 0.0)
            hi = sp.astype(BF16)
            lo = (sp - hi.astype(F32)).astype(BF16)
            zs[p] = z
            hls[p] = jnp.concatenate([hi, lo], axis=1)

        def sums(p):
            css[p] = jnp.dot(hls[p], suffix, preferred_element_type=F32)

        def weights(p):
            logw = zs[p] - css[p][:, :n]
            if not diagonal:
                logw = logw + carry_refs[p][...]
            w = jnp.exp(logw)
            if diagonal:
                w = jnp.where(strict, w, 0.0)
            ws[p] = w.astype(BF16)

        def values(p):
            vv = v_ref[pl.ds(ks, n), p * LANES:(p + 1) * LANES]
            pv = jnp.dot(ws[p], vv, preferred_element_type=F32)
            acc_refs[p][...] = pv if diagonal else acc_refs[p][...] + pv
            carry = -css[p][:, n:] if diagonal else carry_refs[p][...] - css[p][:, n:]
            carry_refs[p][...] = carry
            cmaxs.append(jnp.max(carry))

        for stage in (logits, sums, weights, values):
            for p in range(SB_PAIRS):
                stage(p)
        return functools.reduce(jnp.maximum, cmaxs)

    def cond(st):
        kt, cmax = st
        return jnp.logical_and(kt >= 0, cmax > SB_LOG_UNDERFLOW)

    def body(st):
        kt, _ = st
        return kt - 1, step(kt, False)

    lax.while_loop(cond, body, (qi - 1, step(qi, True)))

    for p in range(SB_PAIRS):
        o_ref[:, p * LANES:(p + 1) * LANES] = jnp.where(
            first, acc_refs[p][:n, :], acc_refs[p][n:, :]
        ).astype(o_ref.dtype)


def _sb_attention(proj3):
    B, S, _ = proj3.shape
    n = SB_TILE
    w = SB_HEADS * SB_HEAD_DIM
    qb, kb, vb = OFF_SB_Q // w, OFF_SB_K // w, OFF_SB_V // w
    rows = SB_PAIRS * 2 * n
    return pl.pallas_call(
        _sb_kernel,
        grid=(B, S // n),
        in_specs=[
            pl.BlockSpec((None, n, w), lambda b, i: (b, i, qb)),
            pl.BlockSpec((None, S, w), lambda b, i: (b, 0, kb)),
            pl.BlockSpec((None, S, w), lambda b, i: (b, 0, vb)),
        ],
        out_specs=pl.BlockSpec((None, n, w), lambda b, i: (b, i, 0)),
        out_shape=jax.ShapeDtypeStruct((B, S, w), BF16),
        scratch_shapes=([pltpu.VMEM((rows, LANES), BF16)]
                        + [pltpu.VMEM((2 * n, LANES), F32)] * (2 * SB_PAIRS)),
        compiler_params=_params(("parallel", "arbitrary")),
        name="sb",
    )(proj3, proj3, proj3)


def _ret_tables(S):
    c = RET_CHUNK
    f = np.float32
    inv = f(ROPE_BASE) ** (-np.arange(0, RET_QK_DIM, 2, dtype=f) / f(RET_QK_DIM))
    ang = np.arange(S, dtype=f)[:, None] * inv[None, :]
    cos, sin = np.cos(ang), np.sin(ang)
    reps = LANES // RET_QK_DIM
    cos_t = np.tile(np.concatenate([cos, cos], axis=1), (1, reps))
    sin_t = np.tile(np.concatenate([-sin, sin], axis=1), (1, reps))
    log_g = np.log1p(-np.exp2(-5.0 - np.arange(RET_HEADS, dtype=np.float64)))
    pos = np.arange(c, dtype=np.float64)
    diff = pos[:, None] - pos[None, :]
    decay = np.where(diff >= 0, np.exp(log_g[:, None, None] * np.maximum(diff, 0.0)), 0.0).astype(f)
    q_dec = np.exp(log_g[:, None] * (pos + 1.0)[None, :]).astype(f)
    k_dec = np.exp(log_g[:, None] * (c - 1 - pos)[None, :]).astype(f)
    q_dec = np.ascontiguousarray(np.broadcast_to(q_dec[:, :, None], (RET_HEADS, c, LANES)))
    k_dec = np.ascontiguousarray(np.broadcast_to(k_dec[:, :, None], (RET_HEADS, c, LANES)))
    c_dec = np.ascontiguousarray(np.broadcast_to(
        np.exp(log_g * c).astype(f)[:, None, None], (RET_HEADS, LANES, LANES)))
    return cos_t, sin_t, decay, q_dec, k_dec, c_dec


def _ret_kernel(q_ref, k_ref, v_ref, g_ref, cos_ref, sin_ref, dm_ref, qd_ref, kd_ref, cd_ref,
                o_ref, state):
    c = RET_CHUNK

    @pl.when(pl.program_id(1) == 0)
    def _():
        state[...] = jnp.zeros_like(state)

    lane = lax.broadcasted_iota(jnp.int32, (c, LANES), 1)
    low_half = (lane & (RET_QK_DIM - 1)) < RET_QK_DIM // 2
    cos = cos_ref[...]
    sin = sin_ref[...]

    def rotary(x):
        swapped = jnp.where(low_half, pltpu.roll(x, LANES - RET_QK_DIM // 2, 1),
                            pltpu.roll(x, RET_QK_DIM // 2, 1))
        return x * cos + swapped * sin

    heads = range(RET_HEADS)
    cols = [slice(h * RET_V_DIM, (h + 1) * RET_V_DIM) for h in heads]
    rq, rk, qh, sc, out = {}, {}, {}, {}, {}

    for p in range(RET_HEADS * RET_QK_DIM // LANES):
        rq[p] = rotary(q_ref[:, p * LANES:(p + 1) * LANES].astype(F32))
        rk[p] = rotary(k_ref[:, p * LANES:(p + 1) * LANES].astype(F32)) * RET_QK_DIM ** -0.5
    for h in heads:
        in_head = (lane < RET_QK_DIM) if h % 2 == 0 else (lane >= RET_QK_DIM)
        qh[h] = jnp.where(in_head, rq[h // 2], 0.0)
        sc[h] = lax.dot_general(qh[h].astype(BF16), rk[h // 2].astype(BF16), (((1,), (1,)), ((), ())),
                                preferred_element_type=F32) * dm_ref[h]
    for h in heads:
        vh = v_ref[:, cols[h]]
        o = jnp.dot(sc[h].astype(BF16), vh, preferred_element_type=F32)
        out[h] = o + jnp.dot((qh[h] * qd_ref[h]).astype(BF16), state[h].astype(BF16),
                             preferred_element_type=F32)
        kv_new = lax.dot_general((rk[h // 2] * kd_ref[h]).astype(BF16), vh, (((0,), (0,)), ((), ())),
                                 preferred_element_type=F32)
        state[h] = state[h] * cd_ref[h] + kv_new
    for h in heads:
        o = out[h]
        mu = jnp.mean(o, axis=-1, keepdims=True)
        d = o - mu
        y = d * lax.rsqrt(jnp.mean(d * d, axis=-1, keepdims=True) + NORM_EPS)
        g = g_ref[:, cols[h]].astype(F32)
        o_ref[:, cols[h]] = (g * _logistic(g) * y).astype(o_ref.dtype)


def _retention(proj3):
    B, S, _ = proj3.shape
    c = RET_CHUNK
    qkw, vw = RET_HEADS * RET_QK_DIM, RET_HEADS * RET_V_DIM
    qb, kb, vb, gb = OFF_R_Q // qkw, OFF_R_K // qkw, OFF_R_V // vw, OFF_R_G // vw
    cos_t, sin_t, decay, q_dec, k_dec, c_dec = _ret_tables(S)
    const3 = lambda b, i: (0, 0, 0)
    return pl.pallas_call(
        _ret_kernel,
        grid=(B, S // c),
        in_specs=[
            pl.BlockSpec((None, c, qkw), lambda b, i: (b, i, qb)),
            pl.BlockSpec((None, c, qkw), lambda b, i: (b, i, kb)),
            pl.BlockSpec((None, c, vw), lambda b, i: (b, i, vb)),
            pl.BlockSpec((None, c, vw), lambda b, i: (b, i, gb)),
            pl.BlockSpec((c, LANES), lambda b, i: (i, 0)),
            pl.BlockSpec((c, LANES), lambda b, i: (i, 0)),
            pl.BlockSpec((RET_HEADS, c, c), const3),
            pl.BlockSpec((RET_HEADS, c, LANES), const3),
            pl.BlockSpec((RET_HEADS, c, LANES), const3),
            pl.BlockSpec((RET_HEADS, LANES, LANES), const3),
        ],
        out_specs=pl.BlockSpec((None, c, vw), lambda b, i: (b, i, 0)),
        out_shape=jax.ShapeDtypeStruct((B, S, vw), BF16),
        scratch_shapes=[pltpu.VMEM((RET_HEADS, LANES, LANES), F32)],
        compiler_params=_params(("parallel", "arbitrary")),
        name="ret",
    )(proj3, proj3, proj3, proj3, cos_t, sin_t, decay, q_dec, k_dec, c_dec)


def _memkv_kernel(m_ref, g_ref, w_ref, o_ref):
    h = _rms(m_ref[...], g_ref[...]).astype(BF16)
    o_ref[...] = jnp.dot(h, w_ref[...], preferred_element_type=F32).astype(o_ref.dtype)


def _memkv(mem, g, w_bf16):
    B, M, _ = mem.shape
    return pl.pallas_call(
        _memkv_kernel,
        grid=(B,),
        in_specs=[
            pl.BlockSpec((None, M, D_MODEL), lambda b: (b, 0, 0)),
            pl.BlockSpec((1, D_MODEL), lambda b: (0, 0)),
            pl.BlockSpec((D_MODEL, 2 * D_MODEL), lambda b: (0, 0)),
        ],
        out_specs=pl.BlockSpec((None, M, 2 * D_MODEL), lambda b: (b, 0, 0)),
        out_shape=jax.ShapeDtypeStruct((B, M, 2 * D_MODEL), BF16),
        compiler_params=_params(("parallel",)),
        name="memkv",
    )(mem, g, w_bf16)


MIX_CHUNKS = 2


def _lane_pack(cols, rows, dtype):
    lane = lax.broadcasted_iota(jnp.int32, (rows, LANES), 1)
    out = jnp.zeros((rows, LANES), dtype)
    for k, c in enumerate(cols):
        out = jnp.where(lane == k, c.astype(dtype), out)
    return out


def _mix_kernel(x_ref, ysb_ref, yret_ref, gsb_ref, gret_ref, kv_ref,
                wsbo_ref, wreto_ref, wmix_ref, wcq_ref, wco_ref,
                gx_ref, gm_ref, wr_ref, br_ref,
                x2_ref, h3_ref, idx_ref, gate_ref, rank_ref, cnt_ref, cnt_scr):
    tm = x_ref.shape[0]
    cr = tm // MIX_CHUNKS
    chunks = [slice(c * cr, (c + 1) * cr) for c in range(MIX_CHUNKS)]
    dot = functools.partial(jnp.dot, preferred_element_type=F32)

    @pl.when(jnp.logical_and(pl.program_id(0) == 0, pl.program_id(1) == 0))
    def _():
        cnt_scr[...] = jnp.zeros_like(cnt_scr)

    def sigmoid(g):
        return _logistic(g).astype(F32)

    a = [dot(ysb_ref[s, :], wsbo_ref[...]) for s in chunks]
    b = [dot(yret_ref[s, :], wreto_ref[...]) for s in chunks]
    merged = [sigmoid(gsb_ref[s, :]) * ac + sigmoid(gret_ref[s, :]) * bc
              for s, ac, bc in zip(chunks, a, b)]
    x1 = [x_ref[s, :] + dot(m.astype(BF16), wmix_ref[...]) for s, m in zip(chunks, merged)]

    h2 = [_rms(v, gx_ref[...]).astype(BF16) for v in x1]
    q = [(dot(v, wcq_ref[...]) * X_HEAD_DIM ** -0.5).astype(BF16) for v in h2]
    heads = [[] for _ in chunks]
    for hh in range(X_HEADS):
        lo = hh * X_HEAD_DIM
        kh = kv_ref[:, lo:lo + X_HEAD_DIM]
        vh = kv_ref[:, D_MODEL + lo:D_MODEL + lo + X_HEAD_DIM]
        s = [lax.dot_general(qc[:, lo:lo + X_HEAD_DIM], kh, (((1,), (1,)), ((), ())),
                             preferred_element_type=F32) for qc in q]
        p = [jnp.exp(sc - jnp.max(sc, axis=-1, keepdims=True)) for sc in s]
        p = [pc * (1.0 / jnp.sum(pc, axis=-1, keepdims=True)) for pc in p]
        for c, pc in enumerate(p):
            heads[c].append(dot(pc.astype(BF16), vh).astype(BF16))
    o = [jnp.concatenate(hc, axis=1) for hc in heads]
    x2 = [v + dot(oc, wco_ref[...]) for v, oc in zip(x1, o)]
    for s, v in zip(chunks, x2):
        x2_ref[s, :] = v

    h3 = [_rms(v, gm_ref[...]) for v in x2]
    for ci, v in enumerate(h3):
        for c in range(ROW_TILES):
            h3_ref[pl.ds(ci * cr * ROW_TILES + c, cr, stride=ROW_TILES), :] = v[:, c * LANES:(c + 1) * LANES]
    logits = []
    for v in h3:
        hi = v.astype(BF16)
        lo = (v - hi.astype(F32)).astype(BF16)
        r = dot(jnp.concatenate([hi, lo], axis=0), wr_ref[...])
        logits.append(r[:cr, :N_EXPERTS] + r[:cr, N_EXPERTS:] + r[cr:, :N_EXPERTS] + br_ref[...])

    e_iota = lax.broadcasted_iota(jnp.int32, (cr, N_EXPERTS), 1).astype(F32)
    r_i = lax.broadcasted_iota(jnp.int32, (cr, cr), 0)
    c_i = lax.broadcasted_iota(jnp.int32, (cr, cr), 1)
    before = jnp.where(c_i < r_i, 1.0, 0.0).astype(BF16)
    count = cnt_scr[...]
    for s, rem in zip(chunks, logits):
        vals, idxs = [], []
        for _ in range(TOP_K):
            m = jnp.max(rem, axis=-1, keepdims=True)
            ik = jnp.min(jnp.where(rem == m, e_iota, float(N_EXPERTS)), axis=-1, keepdims=True)
            vals.append(m)
            idxs.append(ik)
            rem = jnp.where(e_iota == ik, -jnp.inf, rem)
        ex = [jnp.exp(v - vals[0]) for v in vals]
        inv_den = 1.0 / (ex[0] + ex[1] + ex[2] + ex[3])
        gates = [e * inv_den for e in ex]

        chosen = [(e_iota == ik) for ik in idxs]
        member = jnp.zeros((cr, N_EXPERTS), F32)
        for ch in chosen:
            member = member + jnp.where(ch, 1.0, 0.0)
        prefix = dot(before, member.astype(BF16)) + count
        ranks = [jnp.sum(jnp.where(ch, prefix, 0.0), axis=-1, keepdims=True) for ch in chosen]
        count = count + jnp.sum(member, axis=0, keepdims=True)

        idx_ref[s, :] = _lane_pack(idxs, cr, jnp.int32)
        rank_ref[s, :] = _lane_pack(ranks, cr, jnp.int32)
        gate_ref[s, :] = _lane_pack(gates, cr, F32)
    cnt_scr[...] = count
    cnt_ref[...] = count


def _mix(x, proj3, y_sb, y_ret, kv, wsbo, wreto, wmix, wcq, wco, gx, gm, wr, br):
    B, S, _ = x.shape
    tm = min(512, S)
    M = kv.shape[1]
    gsb_b, gret_b = OFF_G_SB // D_MODEL, OFF_G_RET // D_MODEL
    tok = lambda b, i: (b, i, 0)
    const2 = lambda b, i: (0, 0)
    row_out = lambda w, dt: jax.ShapeDtypeStruct((B, S, w), dt)
    outs = pl.pallas_call(
        _mix_kernel,
        grid=(B, S // tm),
        in_specs=[
            pl.BlockSpec((None, tm, D_MODEL), tok),
            pl.BlockSpec((None, tm, SB_HEADS * SB_HEAD_DIM), tok),
            pl.BlockSpec((None, tm, D_MODEL), tok),
            pl.BlockSpec((None, tm, D_MODEL), lambda b, i: (b, i, gsb_b)),
            pl.BlockSpec((None, tm, D_MODEL), lambda b, i: (b, i, gret_b)),
            pl.BlockSpec((None, M, 2 * D_MODEL), lambda b, i: (b, 0, 0)),
            pl.BlockSpec((SB_HEADS * SB_HEAD_DIM, D_MODEL), const2),
            pl.BlockSpec((D_MODEL, D_MODEL), const2),
            pl.BlockSpec((D_MODEL, D_MODEL), const2),
            pl.BlockSpec((D_MODEL, D_MODEL), const2),
            pl.BlockSpec((D_MODEL, D_MODEL), const2),
            pl.BlockSpec((1, D_MODEL), const2),
            pl.BlockSpec((1, D_MODEL), const2),
            pl.BlockSpec((D_MODEL, 2 * N_EXPERTS), const2),
            pl.BlockSpec((1, N_EXPERTS), const2),
        ],
        out_specs=[
            pl.BlockSpec((None, tm, D_MODEL), tok),
            pl.BlockSpec((None, tm * ROW_TILES, LANES), tok),
            pl.BlockSpec((None, tm, LANES), tok),
            pl.BlockSpec((None, tm, LANES), tok),
            pl.BlockSpec((None, tm, LANES), tok),
            pl.BlockSpec((1, N_EXPERTS), const2),
        ],
        out_shape=[
            row_out(D_MODEL, F32), jax.ShapeDtypeStruct((B, S * ROW_TILES, LANES), F32),
            row_out(LANES, jnp.int32), row_out(LANES, F32), row_out(LANES, jnp.int32),
            jax.ShapeDtypeStruct((1, N_EXPERTS), F32),
        ],
        scratch_shapes=[pltpu.VMEM((1, N_EXPERTS), F32)],
        compiler_params=_params(("arbitrary", "arbitrary")),
        name="mix",
    )(x, y_sb, y_ret, proj3, proj3, kv, wsbo, wreto, wmix, wcq, wco, gx, gm, wr, br)
    return outs


ROW_TILES = D_MODEL // LANES


def _token_tile_copy(src, s, dst, d, sem):
    rows = lambda i: pl.ds(pl.multiple_of(i * ROW_TILES, ROW_TILES), ROW_TILES)
    return pltpu.make_async_copy(src.at[rows(s)], dst.at[rows(d)], sem)


DMA_UNROLL = 16


ZERO_ROWS = 256


def _dispatch_kernel(lo_ref, hi_ref, pos_ref, h_ref, xin_ref, zbuf, sem, zsem):
    tm = h_ref.shape[0] // ROW_TILES

    def tokens(start, count):
        return pl.ds(pl.multiple_of(start * ROW_TILES, ROW_TILES), count * ROW_TILES)

    def zero_fill(act):
        def per_expert(e, carry):
            lo = lo_ref[e]
            hi = hi_ref[e]

            def chunk(i, off):
                act(pltpu.make_async_copy(zbuf, xin_ref.at[tokens(off, ZERO_ROWS)], zsem))
                return off + ZERO_ROWS

            off = lax.fori_loop(0, (hi - lo) // ZERO_ROWS, chunk, lo)
            rest = hi - off
            size = ZERO_ROWS // 2
            while size >= 1:
                @pl.when((rest & size) != 0)
                def _(off=off, size=size):
                    act(pltpu.make_async_copy(zbuf.at[tokens(0, size)], xin_ref.at[tokens(off, size)],
                                              zsem))
                off = off + (rest & size)
                size //= 2
            return carry

        lax.fori_loop(0, N_EXPERTS, per_expert, 0)

    @pl.when(pl.program_id(0) == 0)
    def _():
        zbuf[...] = jnp.zeros_like(zbuf)
        zero_fill(lambda cp: cp.start())
        zero_fill(lambda cp: cp.wait())

    parts, _, per_part = pos_ref.shape
    for part in range(parts):
        def issue(t, carry, part=part):
            for k in range(TOP_K):
                _token_tile_copy(h_ref, part * (per_part // TOP_K) + t, xin_ref,
                                 pos_ref[part, 0, t * TOP_K + k], sem).start(priority=k % 2)
            return carry

        lax.fori_loop(0, per_part // TOP_K, issue, 0, unroll=DMA_UNROLL)
    for _ in range(TOP_K):
        pltpu.make_async_copy(h_ref, xin_ref.at[tokens(0, tm)], sem).wait()


def _dispatch(h3_tm, pos3, pad_lo, pad_hi, n_rows):
    T = h3_tm.shape[0] // ROW_TILES
    per_tile = pos3.shape[2] // TOP_K
    tm = min(max(1024, per_tile), T)
    parts = tm // per_tile
    grid_spec = pltpu.PrefetchScalarGridSpec(
        num_scalar_prefetch=2,
        grid=(T // tm,),
        in_specs=[
            pl.BlockSpec((parts, 1, per_tile * TOP_K), lambda i, lo, hi: (i, 0, 0),
                         memory_space=pltpu.SMEM),
            pl.BlockSpec((tm * ROW_TILES, LANES), lambda i, lo, hi: (i, 0)),
        ],
        out_specs=pl.BlockSpec(memory_space=pl.ANY),
        scratch_shapes=[pltpu.VMEM((ZERO_ROWS * ROW_TILES, LANES), F32), pltpu.SemaphoreType.DMA(()),
                        pltpu.SemaphoreType.DMA(())],
    )
    return pl.pallas_call(
        _dispatch_kernel,
        grid_spec=grid_spec,
        out_shape=jax.ShapeDtypeStruct((n_rows * ROW_TILES, LANES), F32),
        compiler_params=_params(("arbitrary",)),
        name="dispatch",
    )(pad_lo, pad_hi, pos3, h3_tm)


def _ffn_kernel(first_ref, count_ref, nreal_ref, x_hbm, w1_ref, b1_ref, w2_ref, b2_ref, y_hbm,
                xbuf, ybuf, w1b, w2b, in_sem, out_sem):
    e = pl.program_id(0)
    tb = FFN_BLOCK
    ahead = FFN_SLOTS - FFN_GROUP
    n_real = nreal_ref[0]

    span = tb * ROW_TILES

    def rows(g):
        return pl.ds(pl.multiple_of(g * span, span), span)

    def x_copy(g):
        slot = g % FFN_SLOTS
        return pltpu.make_async_copy(x_hbm.at[rows(g)], xbuf.at[slot], in_sem.at[slot])

    def y_copy(g):
        slot = g % FFN_SLOTS
        return pltpu.make_async_copy(ybuf.at[slot], y_hbm.at[rows(g)], out_sem.at[slot])

    @pl.when(e == 0)
    def _():
        for g in range(ahead):
            @pl.when(g < n_real)
            def _(g=g):
                x_copy(g).start(priority=1)

    @pl.when(count_ref[e] > 0)
    def _():
        w1b[...] = w1_ref[...].astype(BF16)
        w2b[...] = w2_ref[...].astype(BF16)

    def blocks(g0, m):
        group = [g0 + i for i in range(m)]
        for g in group:
            x_copy(g).wait()
        for g in group:
            @pl.when(g + ahead < n_real)
            def _(g=g):
                x_copy(g + ahead).start(priority=1)

            @pl.when(g >= FFN_SLOTS)
            def _(g=g):
                y_copy(g - FFN_SLOTS).wait()

        x = jnp.concatenate(
            [jnp.concatenate([xbuf[g % FFN_SLOTS, pl.ds(c, tb, stride=ROW_TILES), :].astype(BF16)
                              for c in range(ROW_TILES)], axis=1) for g in group], axis=0)
        hc = jnp.dot(x, w1b[...], preferred_element_type=F32) + b1_ref[...]
        glu = jnp.minimum(hc[:, :D_FF], SWIGLU_LIMIT)
        lin = jnp.clip(hc[:, D_FF:], -SWIGLU_LIMIT, SWIGLU_LIMIT)
        gate = jnp.tanh((0.5 * SWIGLU_ALPHA) * glu) + 1.0
        act = (0.5 * glu) * gate * (lin + 1.0)
        y = jnp.dot(act.astype(BF16), w2b[...], preferred_element_type=F32) + b2_ref[...]
        for i, g in enumerate(group):
            for c in range(ROW_TILES):
                ybuf[g % FFN_SLOTS, pl.ds(c, tb, stride=ROW_TILES), :] = (
                    y[i * tb:(i + 1) * tb, c * LANES:(c + 1) * LANES])
            y_copy(g).start(priority=1)

    first, count = first_ref[e], count_ref[e]

    def pair(j, carry):
        blocks(first + FFN_GROUP * j, FFN_GROUP)
        return carry

    lax.fori_loop(0, count // FFN_GROUP, pair, 0)

    for left in range(1, FFN_GROUP):
        @pl.when(count % FFN_GROUP == left)
        def _(left=left):
            blocks(first + count - left, left)

    @pl.when(e == N_EXPERTS - 1)
    def _():
        for back in range(FFN_SLOTS, 0, -1):
            @pl.when(n_real >= back)
            def _(back=back):
                y_copy(n_real - back).wait()

        for s in range(FFN_SLOTS):
            ybuf[s] = jnp.zeros((tb * ROW_TILES, LANES), F32)

        def tail(g, carry):
            cp = y_copy(g)
            cp.start()
            cp.wait()
            return carry

        lax.fori_loop(n_real, y_hbm.shape[0] // (tb * ROW_TILES), tail, 0)


def _ffn(xin, first_block, block_count, n_real, w1, b1, w2, b2):
    tb = FFN_BLOCK
    expert = lambda e, first, count, nr: (e, 0, 0)
    grid_spec = pltpu.PrefetchScalarGridSpec(
        num_scalar_prefetch=3,
        grid=(N_EXPERTS,),
        in_specs=[
            pl.BlockSpec(memory_space=pl.ANY),
            pl.BlockSpec((None, D_MODEL, 2 * D_FF), expert),
            pl.BlockSpec((None, 1, 2 * D_FF), expert),
            pl.BlockSpec((None, D_FF, D_MODEL), expert),
            pl.BlockSpec((None, 1, D_MODEL), expert),
        ],
        out_specs=pl.BlockSpec(memory_space=pl.ANY),
        scratch_shapes=[pltpu.VMEM((FFN_SLOTS, tb * ROW_TILES, LANES), F32),
                        pltpu.VMEM((FFN_SLOTS, tb * ROW_TILES, LANES), F32),
                        pltpu.VMEM((D_MODEL, 2 * D_FF), BF16), pltpu.VMEM((D_FF, D_MODEL), BF16),
                        pltpu.SemaphoreType.DMA((FFN_SLOTS,)),
                        pltpu.SemaphoreType.DMA((FFN_SLOTS,))],
    )
    return pl.pallas_call(
        _ffn_kernel,
        grid_spec=grid_spec,
        out_shape=jax.ShapeDtypeStruct(xin.shape, F32),
        compiler_params=_params(("arbitrary",)),
        name="ffn",
    )(first_block, block_count, n_real, xin, w1, b1.reshape(N_EXPERTS, 1, 2 * D_FF), w2,
      b2.reshape(N_EXPERTS, 1, D_MODEL))


COMBINE_PARTS = 4
COMBINE_AHEAD = 2


def _combine_kernel(pos_ref, nxt_ref, x2_ref, gate_ref, g_ref, y_ref, o_ref, *scratch):
    bufs, sem = scratch[:COMBINE_PARTS], scratch[COMBINE_PARTS]
    tm = x2_ref.shape[0]
    sub = tm // COMBINE_PARTS
    i = pl.program_id(0)
    last = i + 1 == pl.num_programs(0)

    def request(j):
        p_ref, part = (pos_ref, j) if j < COMBINE_PARTS else (nxt_ref, j - COMBINE_PARTS)
        for t in range(sub):
            for k in range(TOP_K):
                _token_tile_copy(y_ref, p_ref[0, 0, (part * sub + t) * TOP_K + k], bufs[part].at[k], t,
                                 sem.at[part]).start(priority=k % 2)

    def wait(j):
        for k in range(TOP_K):
            pltpu.make_async_copy(y_ref.at[pl.ds(0, sub * ROW_TILES)], bufs[j].at[k], sem.at[j]).wait()

    def reduce(j):
        rows = slice(j * sub, (j + 1) * sub)
        gate = gate_ref[rows, :]
        gates = [gate[:, k:k + 1] for k in range(TOP_K)]
        acc, ssq = [], jnp.zeros((sub, 1), F32)
        for c in range(ROW_TILES):
            a = x2_ref[rows, c * LANES:(c + 1) * LANES]
            for k in range(TOP_K):
                a = a + gates[k] * bufs[j][k, pl.ds(c, sub, stride=ROW_TILES), :]
            acc.append(a)
            ssq = ssq + jnp.sum(a * a, axis=-1, keepdims=True)
        inv = lax.rsqrt(ssq * (1.0 / D_MODEL) + NORM_EPS)
        for c in range(ROW_TILES):
            o_ref[rows, c * LANES:(c + 1) * LANES] = acc[c] * inv * g_ref[:, c * LANES:(c + 1) * LANES]

    @pl.when(i == 0)
    def _():
        for j in range(COMBINE_AHEAD):
            request(j)

    for j in range(COMBINE_PARTS):
        wait(j)
        if j + COMBINE_AHEAD < COMBINE_PARTS:
            request(j + COMBINE_AHEAD)
            reduce(j)
        else:
            @pl.when(jnp.logical_not(last))
            def _(j=j):
                request(j + COMBINE_AHEAD)
                reduce(j)

            @pl.when(last)
            def _(j=j):
                reduce(j)


COMBINE_TILE = 512


def _combine(x2, gate, pos3, yrows, g_final):
    T = x2.shape[0]
    n, _, per_tile = pos3.shape
    tm = per_tile // TOP_K
    return pl.pallas_call(
        _combine_kernel,
        grid=(n,),
        in_specs=[
            pl.BlockSpec((1, 1, tm * TOP_K), lambda i: (i, 0, 0), memory_space=pltpu.SMEM),
            pl.BlockSpec((1, 1, tm * TOP_K), lambda i: (jnp.minimum(i + 1, n - 1), 0, 0),
                         memory_space=pltpu.SMEM),
            pl.BlockSpec((tm, D_MODEL), lambda i: (i, 0)),
            pl.BlockSpec((tm, LANES), lambda i: (i, 0)),
            pl.BlockSpec((1, D_MODEL), lambda i: (0, 0)),
            pl.BlockSpec(memory_space=pl.ANY),
        ],
        out_specs=pl.BlockSpec((tm, D_MODEL), lambda i: (i, 0)),
        out_shape=jax.ShapeDtypeStruct((T, D_MODEL), F32),
        scratch_shapes=([pltpu.VMEM((TOP_K, tm // COMBINE_PARTS * ROW_TILES, LANES), F32)] * COMBINE_PARTS
                        + [pltpu.SemaphoreType.DMA((COMBINE_PARTS,))]),
        compiler_params=_params(("arbitrary",)),
        name="combine",
    )(pos3, pos3, x2, gate, g_final, yrows)


def kernel(x, mem, g_mix, w_in, w_sb_o, w_ret_o, w_mix_out, g_xattn, g_mem, w_cq, w_ckv, w_co,
           g_moe, w_router, b_router, w_exp_in, b_exp_in, w_exp_out, b_exp_out, g_final):
    B, S, _ = x.shape
    T = B * S
    assert w_in.shape[0] == 1, "one layer"
    row = lambda v: v.reshape(1, -1)

    w_in_k = jnp.concatenate([w_in[0][:, o:o + w] for _, o, w in _IN_GROUPS], axis=1).astype(BF16)
    proj = _inproj(x.reshape(T, D_MODEL), row(g_mix[0]), w_in_k)
    proj3 = proj.reshape(B, S, IN_W)
    y_sb = _sb_attention(proj3)
    y_ret = _retention(proj3)
    kv = _memkv(mem, row(g_mem[0]), w_ckv[0].astype(BF16))
    wr_hi = w_router[0].astype(BF16)
    wr_lo = (w_router[0] - wr_hi.astype(F32)).astype(BF16)
    x2, h3, idx, gate, rank, cnt = _mix(
        x, proj3, y_sb, y_ret, kv,
        w_sb_o[0].astype(BF16), w_ret_o[0].astype(BF16), w_mix_out[0].astype(BF16),
        w_cq[0].astype(BF16), w_co[0].astype(BF16),
        row(g_xattn[0]), row(g_moe[0]), jnp.concatenate([wr_hi, wr_lo], axis=1), row(b_router[0]))

    tb = FFN_BLOCK
    n_blocks = (T * TOP_K + N_EXPERTS * (tb - 1) + tb - 1) // tb
    counts = cnt[0].astype(jnp.int32)
    padded = ((counts + tb - 1) // tb) * tb
    pad_end = jnp.cumsum(padded)
    pad_start = pad_end - padded
    idx4 = idx.reshape(T, LANES)[:, :TOP_K]
    start4 = jnp.sum(jnp.where(idx4[:, :, None] == jnp.arange(N_EXPERTS), pad_start, 0), axis=-1)
    pos = (start4 + rank.reshape(T, LANES)[:, :TOP_K]).astype(jnp.int32).reshape(T * TOP_K)
    n_real = (pad_end[-1:] // tb).astype(jnp.int32)
    first_block = (pad_start // tb).astype(jnp.int32)
    block_count = (padded // tb).astype(jnp.int32)

    pad_lo = (pad_start + counts).astype(jnp.int32)
    pad_hi = jnp.where(jnp.arange(N_EXPERTS) == N_EXPERTS - 1, n_blocks * tb, pad_end).astype(jnp.int32)
    tile = min(COMBINE_TILE, T)
    pos3 = pos.reshape(T // tile, 1, tile * TOP_K)
    xin = _dispatch(h3.reshape(T * ROW_TILES, LANES), pos3, pad_lo, pad_hi, n_blocks * tb)
    yrows = _ffn(xin, first_block, block_count, n_real,
                 w_exp_in[0], b_exp_in[0], w_exp_out[0], b_exp_out[0])
    out = _combine(x2.reshape(T, D_MODEL), gate.reshape(T, LANES), pos3, yrows, row(g_final))
    return out.reshape(B, S, D_MODEL)
```

```python
import functools

import numpy as np
import jax
import jax.numpy as jnp
from jax import lax
from jax.experimental import pallas as pl
from jax.experimental.pallas import tpu as pltpu

F32 = jnp.float32
BF16 = jnp.bfloat16

D_MODEL = 1024
SB_HEADS = 8
SB_HEAD_DIM = 64
RET_HEADS = 8
RET_QK_DIM = 64
RET_V_DIM = 128
ROPE_BASE = 10000.0
X_HEADS = 4
X_HEAD_DIM = 256
N_EXPERTS = 32
TOP_K = 4
D_FF = 1024
SWIGLU_LIMIT = 7.0
SWIGLU_ALPHA = 1.702
NORM_EPS = 1e-5

_IN_GROUPS = (("r_v", 2560, 1024), ("r_g", 3584, 1024), ("g_sb", 4608, 1024), ("g_ret", 5632, 1024),
              ("sb_q", 0, 512), ("sb_k", 512, 512), ("sb_v", 1024, 512),
              ("r_q", 1536, 512), ("r_k", 2048, 512))
OFF_R_V, OFF_R_G, OFF_G_SB, OFF_G_RET = 0, 1024, 2048, 3072
OFF_SB_Q, OFF_SB_K, OFF_SB_V, OFF_R_Q, OFF_R_K = 4096, 4608, 5120, 5632, 6144
IN_W = 6656

LANES = 128
VMEM_LIMIT = 56 * 1024 * 1024

SB_TILE = 128
SB_LOG_UNDERFLOW = -88.0
RET_CHUNK = 256
FFN_BLOCK = 128
FFN_GROUP = 4
FFN_SLOTS = 12

def _rms(x, g):
    return x * lax.rsqrt(jnp.mean(x * x, axis=-1, keepdims=True) + NORM_EPS) * g


def _logistic(v):
    return 0.5 * jnp.tanh(0.5 * v) + 0.5


def _params(sem):
    return pltpu.CompilerParams(dimension_semantics=sem, vmem_limit_bytes=VMEM_LIMIT)


INPROJ_COLS = 512


def _inproj_kernel(x_ref, g_ref, w_ref, o_ref):
    h = _rms(x_ref[...], g_ref[...]).astype(BF16)
    for j in range(IN_W // INPROJ_COLS):
        cols = slice(j * INPROJ_COLS, (j + 1) * INPROJ_COLS)
        o_ref[:, cols] = jnp.dot(h, w_ref[:, cols], preferred_element_type=F32).astype(o_ref.dtype)


def _inproj(x2d, g, w_bf16):
    T = x2d.shape[0]
    tm = min(512, T)
    return pl.pallas_call(
        _inproj_kernel,
        grid=(T // tm,),
        in_specs=[
            pl.BlockSpec((tm, D_MODEL), lambda i: (i, 0)),
            pl.BlockSpec((1, D_MODEL), lambda i: (0, 0)),
            pl.BlockSpec((D_MODEL, IN_W), lambda i: (0, 0)),
        ],
        out_specs=pl.BlockSpec((tm, IN_W), lambda i: (i, 0)),
        out_shape=jax.ShapeDtypeStruct((T, IN_W), BF16),
        compiler_params=_params(("parallel",)),
        name="inproj",
    )(x2d, g, w_bf16)


SB_PAIRS = SB_HEADS * SB_HEAD_DIM // LANES


def _sb_kernel(q_ref, k_ref, v_ref, o_ref, q2_ref, *state_refs):
    carry_refs, acc_refs = state_refs[:SB_PAIRS], state_refs[SB_PAIRS:]
    n = SB_TILE
    qi = pl.program_id(1)
    lane = lax.broadcasted_iota(jnp.int32, (n, LANES), 1)
    first = lane < SB_HEAD_DIM
    row = lax.broadcasted_iota(jnp.int32, (2 * n, n), 0) & (n - 1)
    col = lax.broadcasted_iota(jnp.int32, (2 * n, n), 1)
    strict = col < row
    jj = lax.broadcasted_iota(jnp.int32, (2 * n, 2 * n), 0) & (n - 1)
    cc = lax.broadcasted_iota(jnp.int32, (2 * n, 2 * n), 1)
    suffix = jnp.where((cc >= n) | (jj >= cc), 1.0, 0.0).astype(BF16)
    scale = SB_HEAD_DIM ** -0.5

    for p in range(SB_PAIRS):
        qs = (q_ref[:, p * LANES:(p + 1) * LANES].astype(F32) * scale).astype(BF16)
        zeros = jnp.zeros_like(qs)
        q2_ref[p * 2 * n:p * 2 * n + n, :] = jnp.where(first, qs, zeros)
        q2_ref[p * 2 * n + n:(p + 1) * 2 * n, :] = jnp.where(first, zeros, qs)

    def step(kt, diagonal):
        ks = pl.multiple_of(kt * n, n)
        zs, hls, css, ws, cmaxs = {}, {}, {}, {}, []

        def logits(p):
            kk = k_ref[pl.ds(ks, n), p * LANES:(p + 1) * LANES]
            z = lax.dot_general(q2_ref[p * 2 * n:(p + 1) * 2 * n, :], kk, (((1,), (1,)), ((), ())),
                                preferred_element_type=F32)
            sp = jnp.maximum(z, 0.0) + jnp.log(1.0 + jnp.exp(-jnp.abs(z)))
            if diagonal:
                sp = jnp.where(strict, sp, 0.0)
            hi = sp.astype(BF16)
            lo = (sp - hi.astype(F32)).astype(BF16)
            zs[p] = z
            hls[p] = jnp.concatenate([hi, lo], axis=1)

        def sums(p):
            css[p] = jnp.dot(hls[p], suffix, preferred_element_type=F32)

        def weights(p):
            logw = zs[p] - css[p][:, :n]
            if not diagonal:
                logw = logw + carry_refs[p][...]
            w = jnp.exp(logw)
            if diagonal:
                w = jnp.where(strict, w, 0.0)
            ws[p] = w.astype(BF16)

        def values(p):
            vv = v_ref[pl.ds(ks, n), p * LANES:(p + 1) * LANES]
            pv = jnp.dot(ws[p], vv, preferred_element_type=F32)
            acc_refs[p][...] = pv if diagonal else acc_refs[p][...] + pv
            carry = -css[p][:, n:] if diagonal else carry_refs[p][...] - css[p][:, n:]
            carry_refs[p][...] = carry
            cmaxs.append(jnp.max(carry))

        for stage in (logits, sums, weights, values):
            for p in range(SB_PAIRS):
                stage(p)
        return functools.reduce(jnp.maximum, cmaxs)

    def cond(st):
        kt, cmax = st
        return jnp.logical_and(kt >= 0, cmax > SB_LOG_UNDERFLOW)

    def body(st):
        kt, _ = st
        return kt - 1, step(kt, False)

    lax.while_loop(cond, body, (qi - 1, step(qi, True)))

    for p in range(SB_PAIRS):
        o_ref[:, p * LANES:(p + 1) * LANES] = jnp.where(
            first, acc_refs[p][:n, :], acc_refs[p][n:, :]
        ).astype(o_ref.dtype)


def _sb_attention(proj3):
    B, S, _ = proj3.shape
    n = SB_TILE
    w = SB_HEADS * SB_HEAD_DIM
    qb, kb, vb = OFF_SB_Q // w, OFF_SB_K // w, OFF_SB_V // w
    rows = SB_PAIRS * 2 * n
    return pl.pallas_call(
        _sb_kernel,
        grid=(B, S // n),
        in_specs=[
            pl.BlockSpec((None, n, w), lambda b, i: (b, i, qb)),
            pl.BlockSpec((None, S, w), lambda b, i: (b, 0, kb)),
            pl.BlockSpec((None, S, w), lambda b, i: (b, 0, vb)),
        ],
        out_specs=pl.BlockSpec((None, n, w), lambda b, i: (b, i, 0)),
        out_shape=jax.ShapeDtypeStruct((B, S, w), BF16),
        scratch_shapes=([pltpu.VMEM((rows, LANES), BF16)]
                        + [pltpu.VMEM((2 * n, LANES), F32)] * (2 * SB_PAIRS)),
        compiler_params=_params(("parallel", "arbitrary")),
        name="sb",
    )(proj3, proj3, proj3)


def _ret_tables(S):
    c = RET_CHUNK
    f = np.float32
    inv = f(ROPE_BASE) ** (-np.arange(0, RET_QK_DIM, 2, dtype=f) / f(RET_QK_DIM))
    ang = np.arange(S, dtype=f)[:, None] * inv[None, :]
    cos, sin = np.cos(ang), np.sin(ang)
    reps = LANES // RET_QK_DIM
    cos_t = np.tile(np.concatenate([cos, cos], axis=1), (1, reps))
    sin_t = np.tile(np.concatenate([-sin, sin], axis=1), (1, reps))
    log_g = np.log1p(-np.exp2(-5.0 - np.arange(RET_HEADS, dtype=np.float64)))
    pos = np.arange(c, dtype=np.float64)
    diff = pos[:, None] - pos[None, :]
    decay = np.where(diff >= 0, np.exp(log_g[:, None, None] * np.maximum(diff, 0.0)), 0.0).astype(f)
    q_dec = np.exp(log_g[:, None] * (pos + 1.0)[None, :]).astype(f)
    k_dec = np.exp(log_g[:, None] * (c - 1 - pos)[None, :]).astype(f)
    q_dec = np.ascontiguousarray(np.broadcast_to(q_dec[:, :, None], (RET_HEADS, c, LANES)))
    k_dec = np.ascontiguousarray(np.broadcast_to(k_dec[:, :, None], (RET_HEADS, c, LANES)))
    c_dec = np.ascontiguousarray(np.broadcast_to(
        np.exp(log_g * c).astype(f)[:, None, None], (RET_HEADS, LANES, LANES)))
    return cos_t, sin_t, decay, q_dec, k_dec, c_dec


def _ret_kernel(q_ref, k_ref, v_ref, g_ref, cos_ref, sin_ref, dm_ref, qd_ref, kd_ref, cd_ref,
                o_ref, state):
    c = RET_CHUNK

    @pl.when(pl.program_id(1) == 0)
    def _():
        state[...] = jnp.zeros_like(state)

    lane = lax.broadcasted_iota(jnp.int32, (c, LANES), 1)
    low_half = (lane & (RET_QK_DIM - 1)) < RET_QK_DIM // 2
    cos = cos_ref[...]
    sin = sin_ref[...]

    def rotary(x):
        swapped = jnp.where(low_half, pltpu.roll(x, LANES - RET_QK_DIM // 2, 1),
                            pltpu.roll(x, RET_QK_DIM // 2, 1))
        return x * cos + swapped * sin

    heads = range(RET_HEADS)
    cols = [slice(h * RET_V_DIM, (h + 1) * RET_V_DIM) for h in heads]
    rq, rk, qh, sc, out = {}, {}, {}, {}, {}

    for p in range(RET_HEADS * RET_QK_DIM // LANES):
        rq[p] = rotary(q_ref[:, p * LANES:(p + 1) * LANES].astype(F32))
        rk[p] = rotary(k_ref[:, p * LANES:(p + 1) * LANES].astype(F32)) * RET_QK_DIM ** -0.5
    for h in heads:
        in_head = (lane < RET_QK_DIM) if h % 2 == 0 else (lane >= RET_QK_DIM)
        qh[h] = jnp.where(in_head, rq[h // 2], 0.0)
        sc[h] = lax.dot_general(qh[h].astype(BF16), rk[h // 2].astype(BF16), (((1,), (1,)), ((), ())),
                                preferred_element_type=F32) * dm_ref[h]
    for h in heads:
        vh = v_ref[:, cols[h]]
        o = jnp.dot(sc[h].astype(BF16), vh, preferred_element_type=F32)
        out[h] = o + jnp.dot((qh[h] * qd_ref[h]).astype(BF16), state[h].astype(BF16),
                             preferred_element_type=F32)
        kv_new = lax.dot_general((rk[h // 2] * kd_ref[h]).astype(BF16), vh, (((0,), (0,)), ((), ())),
                                 preferred_element_type=F32)
        state[h] = state[h] * cd_ref[h] + kv_new
    for h in heads:
        o = out[h]
        mu = jnp.mean(o, axis=-1, keepdims=True)
        d = o - mu
        y = d * lax.rsqrt(jnp.mean(d * d, axis=-1, keepdims=True) + NORM_EPS)
        g = g_ref[:, cols[h]].astype(F32)
        o_ref[:, cols[h]] = (g * _logistic(g) * y).astype(o_ref.dtype)


def _retention(proj3):
    B, S, _ = proj3.shape
    c = RET_CHUNK
    qkw, vw = RET_HEADS * RET_QK_DIM, RET_HEADS * RET_V_DIM
    qb, kb, vb, gb = OFF_R_Q // qkw, OFF_R_K // qkw, OFF_R_V // vw, OFF_R_G // vw
    cos_t, sin_t, decay, q_dec, k_dec, c_dec = _ret_tables(S)
    const3 = lambda b, i: (0, 0, 0)
    return pl.pallas_call(
        _ret_kernel,
        grid=(B, S // c),
        in_specs=[
            pl.BlockSpec((None, c, qkw), lambda b, i: (b, i, qb)),
            pl.BlockSpec((None, c, qkw), lambda b, i: (b, i, kb)),
            pl.BlockSpec((None, c, vw), lambda b, i: (b, i, vb)),
            pl.BlockSpec((None, c, vw), lambda b, i: (b, i, gb)),
            pl.BlockSpec((c, LANES), lambda b, i: (i, 0)),
            pl.BlockSpec((c, LANES), lambda b, i: (i, 0)),
            pl.BlockSpec((RET_HEADS, c, c), const3),
            pl.BlockSpec((RET_HEADS, c, LANES), const3),
            pl.BlockSpec((RET_HEADS, c, LANES), const3),
            pl.BlockSpec((RET_HEADS, LANES, LANES), const3),
        ],
        out_specs=pl.BlockSpec((None, c, vw), lambda b, i: (b, i, 0)),
        out_shape=jax.ShapeDtypeStruct((B, S, vw), BF16),
        scratch_shapes=[pltpu.VMEM((RET_HEADS, LANES, LANES), F32)],
        compiler_params=_params(("parallel", "arbitrary")),
        name="ret",
    )(proj3, proj3, proj3, proj3, cos_t, sin_t, decay, q_dec, k_dec, c_dec)


MIX_CHUNKS = 2


def _lane_pack(cols, rows, dtype):
    lane = lax.broadcasted_iota(jnp.int32, (rows, LANES), 1)
    out = jnp.zeros((rows, LANES), dtype)
    for k, c in enumerate(cols):
        out = jnp.where(lane == k, c.astype(dtype), out)
    return out


def _mix_kernel(x_ref, ysb_ref, yret_ref, gsb_ref, gret_ref, mem_ref, gmem_ref, wckv_ref,
                wsbo_ref, wreto_ref, wmix_ref, wcq_ref, wco_ref,
                gx_ref, gm_ref, wr_ref, br_ref,
                x2_ref, h3_ref, idx_ref, gate_ref, rank_ref, cnt_ref, cnt_scr, kv_ref):
    tm = x_ref.shape[0]
    cr = tm // MIX_CHUNKS
    chunks = [slice(c * cr, (c + 1) * cr) for c in range(MIX_CHUNKS)]
    dot = functools.partial(jnp.dot, preferred_element_type=F32)

    @pl.when(pl.program_id(1) == 0)
    def _():
        hm = _rms(mem_ref[...], gmem_ref[...]).astype(BF16)
        kv_ref[...] = dot(hm, wckv_ref[...]).astype(BF16)

    @pl.when(jnp.logical_and(pl.program_id(0) == 0, pl.program_id(1) == 0))
    def _():
        cnt_scr[...] = jnp.zeros_like(cnt_scr)

    def sigmoid(g):
        return _logistic(g).astype(F32)

    a = [dot(ysb_ref[s, :], wsbo_ref[...]) for s in chunks]
    b = [dot(yret_ref[s, :], wreto_ref[...]) for s in chunks]
    merged = [sigmoid(gsb_ref[s, :]) * ac + sigmoid(gret_ref[s, :]) * bc
              for s, ac, bc in zip(chunks, a, b)]
    x1 = [x_ref[s, :] + dot(m.astype(BF16), wmix_ref[...]) for s, m in zip(chunks, merged)]

    h2 = [_rms(v, gx_ref[...]).astype(BF16) for v in x1]
    q = [(dot(v, wcq_ref[...]) * X_HEAD_DIM ** -0.5).astype(BF16) for v in h2]
    heads = [[] for _ in chunks]
    for hh in range(X_HEADS):
        lo = hh * X_HEAD_DIM
        kh = kv_ref[:, lo:lo + X_HEAD_DIM]
        vh = kv_ref[:, D_MODEL + lo:D_MODEL + lo + X_HEAD_DIM]
        s = [lax.dot_general(qc[:, lo:lo + X_HEAD_DIM], kh, (((1,), (1,)), ((), ())),
                             preferred_element_type=F32) for qc in q]
        p = [jnp.exp(sc - jnp.max(sc, axis=-1, keepdims=True)) for sc in s]
        p = [pc * (1.0 / jnp.sum(pc, axis=-1, keepdims=True)) for pc in p]
        for c, pc in enumerate(p):
            heads[c].append(dot(pc.astype(BF16), vh).astype(BF16))
    o = [jnp.concatenate(hc, axis=1) for hc in heads]
    x2 = [v + dot(oc, wco_ref[...]) for v, oc in zip(x1, o)]
    for s, v in zip(chunks, x2):
        x2_ref[s, :] = v

    h3 = [_rms(v, gm_ref[...]) for v in x2]
    for ci, v in enumerate(h3):
        for c in range(ROW_TILES):
            h3_ref[pl.ds(ci * cr * ROW_TILES + c, cr, stride=ROW_TILES), :] = v[:, c * LANES:(c + 1) * LANES]
    logits = []
    for v in h3:
        hi = v.astype(BF16)
        lo = (v - hi.astype(F32)).astype(BF16)
        r = dot(jnp.concatenate([hi, lo], axis=0), wr_ref[...])
        logits.append(r[:cr, :N_EXPERTS] + r[:cr, N_EXPERTS:] + r[cr:, :N_EXPERTS] + br_ref[...])

    e_iota = lax.broadcasted_iota(jnp.int32, (cr, N_EXPERTS), 1).astype(F32)
    r_i = lax.broadcasted_iota(jnp.int32, (cr, cr), 0)
    c_i = lax.broadcasted_iota(jnp.int32, (cr, cr), 1)
    before = jnp.where(c_i < r_i, 1.0, 0.0).astype(BF16)
    count = cnt_scr[...]
    for s, rem in zip(chunks, logits):
        vals, idxs = [], []
        for _ in range(TOP_K):
            m = jnp.max(rem, axis=-1, keepdims=True)
            ik = jnp.min(jnp.where(rem == m, e_iota, float(N_EXPERTS)), axis=-1, keepdims=True)
            vals.append(m)
            idxs.append(ik)
            rem = jnp.where(e_iota == ik, -jnp.inf, rem)
        ex = [jnp.exp(v - vals[0]) for v in vals]
        inv_den = 1.0 / (ex[0] + ex[1] + ex[2] + ex[3])
        gates = [e * inv_den for e in ex]

        chosen = [(e_iota == ik) for ik in idxs]
        member = jnp.zeros((cr, N_EXPERTS), F32)
        for ch in chosen:
            member = member + jnp.where(ch, 1.0, 0.0)
        prefix = dot(before, member.astype(BF16)) + count
        ranks = [jnp.sum(jnp.where(ch, prefix, 0.0), axis=-1, keepdims=True) for ch in chosen]
        count = count + jnp.sum(member, axis=0, keepdims=True)

        idx_ref[s, :] = _lane_pack(idxs, cr, jnp.int32)
        rank_ref[s, :] = _lane_pack(ranks, cr, jnp.int32)
        gate_ref[s, :] = _lane_pack(gates, cr, F32)
    cnt_scr[...] = count
    cnt_ref[...] = count


def _mix(x, proj3, y_sb, y_ret, mem, gmem, wckv, wsbo, wreto, wmix, wcq, wco, gx, gm, wr, br):
    B, S, _ = x.shape
    tm = min(512, S)
    M = mem.shape[1]
    gsb_b, gret_b = OFF_G_SB // D_MODEL, OFF_G_RET // D_MODEL
    tok = lambda b, i: (b, i, 0)
    const2 = lambda b, i: (0, 0)
    row_out = lambda w, dt: jax.ShapeDtypeStruct((B, S, w), dt)
    outs = pl.pallas_call(
        _mix_kernel,
        grid=(B, S // tm),
        in_specs=[
            pl.BlockSpec((None, tm, D_MODEL), tok),
            pl.BlockSpec((None, tm, SB_HEADS * SB_HEAD_DIM), tok),
            pl.BlockSpec((None, tm, D_MODEL), tok),
            pl.BlockSpec((None, tm, D_MODEL), lambda b, i: (b, i, gsb_b)),
            pl.BlockSpec((None, tm, D_MODEL), lambda b, i: (b, i, gret_b)),
            pl.BlockSpec((None, M, D_MODEL), lambda b, i: (b, 0, 0)),
            pl.BlockSpec((1, D_MODEL), const2),
            pl.BlockSpec((D_MODEL, 2 * D_MODEL), const2),
            pl.BlockSpec((SB_HEADS * SB_HEAD_DIM, D_MODEL), const2),
            pl.BlockSpec((D_MODEL, D_MODEL), const2),
            pl.BlockSpec((D_MODEL, D_MODEL), const2),
            pl.BlockSpec((D_MODEL, D_MODEL), const2),
            pl.BlockSpec((D_MODEL, D_MODEL), const2),
            pl.BlockSpec((1, D_MODEL), const2),
            pl.BlockSpec((1, D_MODEL), const2),
            pl.BlockSpec((D_MODEL, 2 * N_EXPERTS), const2),
            pl.BlockSpec((1, N_EXPERTS), const2),
        ],
        out_specs=[
            pl.BlockSpec((None, tm, D_MODEL), tok),
            pl.BlockSpec((None, tm * ROW_TILES, LANES), tok),
            pl.BlockSpec((None, tm, LANES), tok),
            pl.BlockSpec((None, tm, LANES), tok),
            pl.BlockSpec((None, tm, LANES), tok),
            pl.BlockSpec((1, N_EXPERTS), const2),
        ],
        out_shape=[
            row_out(D_MODEL, F32), jax.ShapeDtypeStruct((B, S * ROW_TILES, LANES), F32),
            row_out(LANES, jnp.int32), row_out(LANES, F32), row_out(LANES, jnp.int32),
            jax.ShapeDtypeStruct((1, N_EXPERTS), F32),
        ],
        scratch_shapes=[pltpu.VMEM((1, N_EXPERTS), F32), pltpu.VMEM((M, 2 * D_MODEL), BF16)],
        compiler_params=_params(("arbitrary", "arbitrary")),
        name="mix",
    )(x, y_sb, y_ret, proj3, proj3, mem, gmem, wckv, wsbo, wreto, wmix, wcq, wco, gx, gm, wr, br)
    return outs


ROW_TILES = D_MODEL // LANES


def _token_tile_copy(src, s, dst, d, sem):
    rows = lambda i: pl.ds(pl.multiple_of(i * ROW_TILES, ROW_TILES), ROW_TILES)
    return pltpu.make_async_copy(src.at[rows(s)], dst.at[rows(d)], sem)


DMA_UNROLL = 16


ZERO_ROWS = 256


def _dispatch_kernel(lo_ref, hi_ref, pos_ref, h_ref, xin_ref, zbuf, sem, zsem):
    tm = h_ref.shape[0] // ROW_TILES

    def tokens(start, count):
        return pl.ds(pl.multiple_of(start * ROW_TILES, ROW_TILES), count * ROW_TILES)

    def zero_fill(act):
        def per_expert(e, carry):
            lo = lo_ref[e]
            hi = hi_ref[e]

            def chunk(i, off):
                act(pltpu.make_async_copy(zbuf, xin_ref.at[tokens(off, ZERO_ROWS)], zsem))
                return off + ZERO_ROWS

            off = lax.fori_loop(0, (hi - lo) // ZERO_ROWS, chunk, lo)
            rest = hi - off
            size = ZERO_ROWS // 2
            while size >= 1:
                @pl.when((rest & size) != 0)
                def _(off=off, size=size):
                    act(pltpu.make_async_copy(zbuf.at[tokens(0, size)], xin_ref.at[tokens(off, size)],
                                              zsem))
                off = off + (rest & size)
                size //= 2
            return carry

        lax.fori_loop(0, N_EXPERTS, per_expert, 0)

    @pl.when(pl.program_id(0) == 0)
    def _():
        zbuf[...] = jnp.zeros_like(zbuf)
        zero_fill(lambda cp: cp.start())
        zero_fill(lambda cp: cp.wait())

    parts, _, per_part = pos_ref.shape
    for part in range(parts):
        def issue(t, carry, part=part):
            for k in range(TOP_K):
                _token_tile_copy(h_ref, part * (per_part // TOP_K) + t, xin_ref,
                                 pos_ref[part, 0, t * TOP_K + k], sem).start(priority=k % 2)
            return carry

        lax.fori_loop(0, per_part // TOP_K, issue, 0, unroll=DMA_UNROLL)
    for _ in range(TOP_K):
        pltpu.make_async_copy(h_ref, xin_ref.at[tokens(0, tm)], sem).wait()


def _dispatch(h3_tm, pos3, pad_lo, pad_hi, n_rows):
    T = h3_tm.shape[0] // ROW_TILES
    per_tile = pos3.shape[2] // TOP_K
    tm = min(max(1024, per_tile), T)
    parts = tm // per_tile
    grid_spec = pltpu.PrefetchScalarGridSpec(
        num_scalar_prefetch=2,
        grid=(T // tm,),
        in_specs=[
            pl.BlockSpec((parts, 1, per_tile * TOP_K), lambda i, lo, hi: (i, 0, 0),
                         memory_space=pltpu.SMEM),
            pl.BlockSpec((tm * ROW_TILES, LANES), lambda i, lo, hi: (i, 0)),
        ],
        out_specs=pl.BlockSpec(memory_space=pl.ANY),
        scratch_shapes=[pltpu.VMEM((ZERO_ROWS * ROW_TILES, LANES), F32), pltpu.SemaphoreType.DMA(()),
                        pltpu.SemaphoreType.DMA(())],
    )
    return pl.pallas_call(
        _dispatch_kernel,
        grid_spec=grid_spec,
        out_shape=jax.ShapeDtypeStruct((n_rows * ROW_TILES, LANES), F32),
        compiler_params=_params(("arbitrary",)),
        name="dispatch",
    )(pad_lo, pad_hi, pos3, h3_tm)


def _ffn_kernel(first_ref, count_ref, nreal_ref, x_hbm, w1_ref, b1_ref, w2_ref, b2_ref, y_hbm,
                xbuf, ybuf, w1b, w2b, in_sem, out_sem):
    e = pl.program_id(0)
    tb = FFN_BLOCK
    ahead = FFN_SLOTS - FFN_GROUP
    n_real = nreal_ref[0]

    span = tb * ROW_TILES

    def rows(g):
        return pl.ds(pl.multiple_of(g * span, span), span)

    def x_copy(g):
        slot = g % FFN_SLOTS
        return pltpu.make_async_copy(x_hbm.at[rows(g)], xbuf.at[slot], in_sem.at[slot])

    def y_copy(g):
        slot = g % FFN_SLOTS
        return pltpu.make_async_copy(ybuf.at[slot], y_hbm.at[rows(g)], out_sem.at[slot])

    @pl.when(e == 0)
    def _():
        for g in range(ahead):
            @pl.when(g < n_real)
            def _(g=g):
                x_copy(g).start(priority=1)

    @pl.when(count_ref[e] > 0)
    def _():
        w1b[...] = w1_ref[...].astype(BF16)
        w2b[...] = w2_ref[...].astype(BF16)

    def blocks(g0, m):
        group = [g0 + i for i in range(m)]
        for g in group:
            x_copy(g).wait()
        for g in group:
            @pl.when(g + ahead < n_real)
            def _(g=g):
                x_copy(g + ahead).start(priority=1)

            @pl.when(g >= FFN_SLOTS)
            def _(g=g):
                y_copy(g - FFN_SLOTS).wait()

        x = jnp.concatenate(
            [jnp.concatenate([xbuf[g % FFN_SLOTS, pl.ds(c, tb, stride=ROW_TILES), :].astype(BF16)
                              for c in range(ROW_TILES)], axis=1) for g in group], axis=0)
        hc = jnp.dot(x, w1b[...], preferred_element_type=F32) + b1_ref[...]
        glu = jnp.minimum(hc[:, :D_FF], SWIGLU_LIMIT)
        lin = jnp.clip(hc[:, D_FF:], -SWIGLU_LIMIT, SWIGLU_LIMIT)
        gate = jnp.tanh((0.5 * SWIGLU_ALPHA) * glu) + 1.0
        act = (0.5 * glu) * gate * (lin + 1.0)
        y = jnp.dot(act.astype(BF16), w2b[...], preferred_element_type=F32) + b2_ref[...]
        for i, g in enumerate(group):
            for c in range(ROW_TILES):
                ybuf[g % FFN_SLOTS, pl.ds(c, tb, stride=ROW_TILES), :] = (
                    y[i * tb:(i + 1) * tb, c * LANES:(c + 1) * LANES])
            y_copy(g).start(priority=1)

    first, count = first_ref[e], count_ref[e]

    def pair(j, carry):
        blocks(first + FFN_GROUP * j, FFN_GROUP)
        return carry

    lax.fori_loop(0, count // FFN_GROUP, pair, 0)

    for left in range(1, FFN_GROUP):
        @pl.when(count % FFN_GROUP == left)
        def _(left=left):
            blocks(first + count - left, left)

    @pl.when(e == N_EXPERTS - 1)
    def _():
        for back in range(FFN_SLOTS, 0, -1):
            @pl.when(n_real >= back)
            def _(back=back):
                y_copy(n_real - back).wait()

        for s in range(FFN_SLOTS):
            ybuf[s] = jnp.zeros((tb * ROW_TILES, LANES), F32)

        def tail(g, carry):
            cp = y_copy(g)
            cp.start()
            cp.wait()
            return carry

        lax.fori_loop(n_real, y_hbm.shape[0] // (tb * ROW_TILES), tail, 0)


def _ffn(xin, first_block, block_count, n_real, w1, b1, w2, b2):
    tb = FFN_BLOCK
    expert = lambda e, first, count, nr: (e, 0, 0)
    grid_spec = pltpu.PrefetchScalarGridSpec(
        num_scalar_prefetch=3,
        grid=(N_EXPERTS,),
        in_specs=[
            pl.BlockSpec(memory_space=pl.ANY),
            pl.BlockSpec((None, D_MODEL, 2 * D_FF), expert),
            pl.BlockSpec((None, 1, 2 * D_FF), expert),
            pl.BlockSpec((None, D_FF, D_MODEL), expert),
            pl.BlockSpec((None, 1, D_MODEL), expert),
        ],
        out_specs=pl.BlockSpec(memory_space=pl.ANY),
        scratch_shapes=[pltpu.VMEM((FFN_SLOTS, tb * ROW_TILES, LANES), F32),
                        pltpu.VMEM((FFN_SLOTS, tb * ROW_TILES, LANES), F32),
                        pltpu.VMEM((D_MODEL, 2 * D_FF), BF16), pltpu.VMEM((D_FF, D_MODEL), BF16),
                        pltpu.SemaphoreType.DMA((FFN_SLOTS,)),
                        pltpu.SemaphoreType.DMA((FFN_SLOTS,))],
    )
    return pl.pallas_call(
        _ffn_kernel,
        grid_spec=grid_spec,
        out_shape=jax.ShapeDtypeStruct(xin.shape, F32),
        compiler_params=_params(("arbitrary",)),
        name="ffn",
    )(first_block, block_count, n_real, xin, w1, b1.reshape(N_EXPERTS, 1, 2 * D_FF), w2,
      b2.reshape(N_EXPERTS, 1, D_MODEL))


COMBINE_PARTS = 4
COMBINE_AHEAD = 2


def _combine_kernel(pos_ref, nxt_ref, x2_ref, gate_ref, g_ref, y_ref, o_ref, *scratch):
    bufs, sem = scratch[:COMBINE_PARTS], scratch[COMBINE_PARTS]
    tm = x2_ref.shape[0]
    sub = tm // COMBINE_PARTS
    i = pl.program_id(0)
    last = i + 1 == pl.num_programs(0)

    def request(j):
        p_ref, part = (pos_ref, j) if j < COMBINE_PARTS else (nxt_ref, j - COMBINE_PARTS)
        for t in range(sub):
            for k in range(TOP_K):
                _token_tile_copy(y_ref, p_ref[0, 0, (part * sub + t) * TOP_K + k], bufs[part].at[k], t,
                                 sem.at[part]).start(priority=k % 2)

    def wait(j):
        for k in range(TOP_K):
            pltpu.make_async_copy(y_ref.at[pl.ds(0, sub * ROW_TILES)], bufs[j].at[k], sem.at[j]).wait()

    def reduce(j):
        rows = slice(j * sub, (j + 1) * sub)
        gate = gate_ref[rows, :]
        gates = [gate[:, k:k + 1] for k in range(TOP_K)]
        acc, ssq = [], jnp.zeros((sub, 1), F32)
        for c in range(ROW_TILES):
            a = x2_ref[rows, c * LANES:(c + 1) * LANES]
            for k in range(TOP_K):
                a = a + gates[k] * bufs[j][k, pl.ds(c, sub, stride=ROW_TILES), :]
            acc.append(a)
            ssq = ssq + jnp.sum(a * a, axis=-1, keepdims=True)
        inv = lax.rsqrt(ssq * (1.0 / D_MODEL) + NORM_EPS)
        for c in range(ROW_TILES):
            o_ref[rows, c * LANES:(c + 1) * LANES] = acc[c] * inv * g_ref[:, c * LANES:(c + 1) * LANES]

    @pl.when(i == 0)
    def _():
        for j in range(COMBINE_AHEAD):
            request(j)

    for j in range(COMBINE_PARTS):
        wait(j)
        if j + COMBINE_AHEAD < COMBINE_PARTS:
            request(j + COMBINE_AHEAD)
            reduce(j)
        else:
            @pl.when(jnp.logical_not(last))
            def _(j=j):
                request(j + COMBINE_AHEAD)
                reduce(j)

            @pl.when(last)
            def _(j=j):
                reduce(j)


COMBINE_TILE = 512


def _combine(x2, gate, pos3, yrows, g_final):
    T = x2.shape[0]
    n, _, per_tile = pos3.shape
    tm = per_tile // TOP_K
    return pl.pallas_call(
        _combine_kernel,
        grid=(n,),
        in_specs=[
            pl.BlockSpec((1, 1, tm * TOP_K), lambda i: (i, 0, 0), memory_space=pltpu.SMEM),
            pl.BlockSpec((1, 1, tm * TOP_K), lambda i: (jnp.minimum(i + 1, n - 1), 0, 0),
                         memory_space=pltpu.SMEM),
            pl.BlockSpec((tm, D_MODEL), lambda i: (i, 0)),
            pl.BlockSpec((tm, LANES), lambda i: (i, 0)),
            pl.BlockSpec((1, D_MODEL), lambda i: (0, 0)),
            pl.BlockSpec(memory_space=pl.ANY),
        ],
        out_specs=pl.BlockSpec((tm, D_MODEL), lambda i: (i, 0)),
        out_shape=jax.ShapeDtypeStruct((T, D_MODEL), F32),
        scratch_shapes=([pltpu.VMEM((TOP_K, tm // COMBINE_PARTS * ROW_TILES, LANES), F32)] * COMBINE_PARTS
                        + [pltpu.SemaphoreType.DMA((COMBINE_PARTS,))]),
        compiler_params=_params(("arbitrary",)),
        name="combine",
    )(pos3, pos3, x2, gate, g_final, yrows)


def kernel(x, mem, g_mix, w_in, w_sb_o, w_ret_o, w_mix_out, g_xattn, g_mem, w_cq, w_ckv, w_co,
           g_moe, w_router, b_router, w_exp_in, b_exp_in, w_exp_out, b_exp_out, g_final):
    B, S, _ = x.shape
    T = B * S
    assert w_in.shape[0] == 1, "one layer"
    row = lambda v: v.reshape(1, -1)

    w_in_k = jnp.concatenate([w_in[0][:, o:o + w] for _, o, w in _IN_GROUPS], axis=1).astype(BF16)
    proj = _inproj(x.reshape(T, D_MODEL), row(g_mix[0]), w_in_k)
    proj3 = proj.reshape(B, S, IN_W)
    y_sb = _sb_attention(proj3)
    y_ret = _retention(proj3)
    wr_hi = w_router[0].astype(BF16)
    wr_lo = (w_router[0] - wr_hi.astype(F32)).astype(BF16)
    x2, h3, idx, gate, rank, cnt = _mix(
        x, proj3, y_sb, y_ret, mem, row(g_mem[0]), w_ckv[0].astype(BF16),
        w_sb_o[0].astype(BF16), w_ret_o[0].astype(BF16), w_mix_out[0].astype(BF16),
        w_cq[0].astype(BF16), w_co[0].astype(BF16),
        row(g_xattn[0]), row(g_moe[0]), jnp.concatenate([wr_hi, wr_lo], axis=1), row(b_router[0]))

    tb = FFN_BLOCK
    n_blocks = (T * TOP_K + N_EXPERTS * (tb - 1) + tb - 1) // tb
    counts = cnt[0].astype(jnp.int32)
    padded = ((counts + tb - 1) // tb) * tb
    pad_end = jnp.cumsum(padded)
    pad_start = pad_end - padded
    idx4 = idx.reshape(T, LANES)[:, :TOP_K]
    start4 = jnp.sum(jnp.where(idx4[:, :, None] == jnp.arange(N_EXPERTS), pad_start, 0), axis=-1)
    pos = (start4 + rank.reshape(T, LANES)[:, :TOP_K]).astype(jnp.int32).reshape(T * TOP_K)
    n_real = (pad_end[-1:] // tb).astype(jnp.int32)
    first_block = (pad_start // tb).astype(jnp.int32)
    block_count = (padded // tb).astype(jnp.int32)

    pad_lo = (pad_start + counts).astype(jnp.int32)
    pad_hi = jnp.where(jnp.arange(N_EXPERTS) == N_EXPERTS - 1, n_blocks * tb, pad_end).astype(jnp.int32)
    tile = min(COMBINE_TILE, T)
    pos3 = pos.reshape(T // tile, 1, tile * TOP_K)
    xin = _dispatch(h3.reshape(T * ROW_TILES, LANES), pos3, pad_lo, pad_hi, n_blocks * tb)
    yrows = _ffn(xin, first_block, block_count, n_real,
                 w_exp_in[0], b_exp_in[0], w_exp_out[0], b_exp_out[0])
    out = _combine(x2.reshape(T, D_MODEL), gate.reshape(T, LANES), pos3, yrows, row(g_final))
    return out.reshape(B, S, D_MODEL)
```
